```python
import math
import jax, jax.numpy as jnp
from jax import lax
import numpy as np

D_MODEL = 1024
BATCH = 16
SEQ = 256
DEPTH = 2
DEC_BATCH = 8
DEC_SEQ = 1024
PAST_LEN = 512

GRID_W = 64
HEAD_DIM = 64
N_Q_HEADS = 8
N_KV_HEADS = 2
GQA_GROUP = N_Q_HEADS // N_KV_HEADS
ATTN_W = N_Q_HEADS * HEAD_DIM
KV_W = N_KV_HEADS * HEAD_DIM
ROPE_AXIS_PAIRS = HEAD_DIM // 4
ROPE_THETA = 10000.0
Q_BLOCK = 128
SSM_W = D_MODEL // 2
SSM_GROUP = 16
N_SSM_GROUPS = SSM_W // SSM_GROUP
STATE_P = 64
DT_MIN = 1e-3
DT_MAX = 1e-1
FOURIER_W = D_MODEL // 2
FOURIER_GROUPS = 4
FOURIER_GW = FOURIER_W // FOURIER_GROUPS
N_BRANCH = 3
D_IN = ATTN_W + 2 * KV_W + SSM_W + FOURIER_W + N_BRANCH * D_MODEL
SPLITS = [ATTN_W, ATTN_W + KV_W, ATTN_W + 2 * KV_W, ATTN_W + 2 * KV_W + SSM_W,
          ATTN_W + 2 * KV_W + SSM_W + FOURIER_W]
D_FF = ((8 * D_MODEL // 3 + 127) // 128) * 128
CONV_W = 3
EPS = 1e-6
ALPHA = (2 * DEPTH) ** 0.25
BETA = (8 * DEPTH) ** -0.25

kernel_name = "hybrid_diffusion_prefix_trunk_step"


def _layer_norm(x, g, b):
    xf = x.astype(jnp.float32)
    mu = jnp.mean(xf, axis=-1, keepdims=True)
    xc = xf - mu
    var = jnp.mean(xc * xc, axis=-1, keepdims=True)
    y = xc * lax.rsqrt(var + EPS) * g.astype(jnp.float32) + b.astype(jnp.float32)
    return y.astype(x.dtype)


def _rms_norm(x, g):
    xf = x.astype(jnp.float32)
    y = xf * lax.rsqrt(jnp.mean(xf * xf, axis=-1, keepdims=True) + EPS) * g.astype(jnp.float32)
    return y.astype(x.dtype)


def _axial_rope_tables(n_tokens):
    rows = n_tokens // GRID_W
    row = jnp.repeat(jnp.arange(rows, dtype=jnp.float32), GRID_W)
    col = jnp.tile(jnp.arange(GRID_W, dtype=jnp.float32), rows)
    freqs = ROPE_THETA ** (-jnp.arange(ROPE_AXIS_PAIRS, dtype=jnp.float32) / ROPE_AXIS_PAIRS)
    ang = jnp.concatenate([row[:, None] * freqs, col[:, None] * freqs], axis=-1)
    return jnp.cos(ang), jnp.sin(ang)


def _apply_rope(x, cos, sin):
    b, l, h, _ = x.shape
    xr = x.astype(jnp.float32).reshape(b, l, h, HEAD_DIM // 2, 2)
    x0, x1 = xr[..., 0], xr[..., 1]
    c = cos[None, :, None, :]
    s = sin[None, :, None, :]
    out = jnp.stack([x0 * c - x1 * s, x0 * s + x1 * c], axis=-1)
    return out.reshape(b, l, h, HEAD_DIM).astype(x.dtype)


def _block_attention(q, k, v):
    bsz, lq = q.shape[:2]
    n_blocks = lq // Q_BLOCK
    qb = q.reshape(bsz, n_blocks, Q_BLOCK, N_KV_HEADS, GQA_GROUP, HEAD_DIM).transpose(1, 0, 2, 3, 4, 5)
    scale = HEAD_DIM ** -0.5

    def one_block(q_blk):
        s = jnp.einsum("bqhgd,bkhd->bhgqk", q_blk, k).astype(jnp.float32) * scale
        p = jax.nn.softmax(s, axis=-1).astype(v.dtype)
        return jnp.einsum("bhgqk,bkhd->bqhgd", p, v)

    o = lax.map(one_block, qb)
    return o.transpose(1, 0, 2, 3, 4, 5).reshape(bsz, lq, ATTN_W)


def _zoh(a_re, a_im, log_dt, b_re, b_im):
    a_re = a_re.astype(jnp.float32)
    a_im = a_im.astype(jnp.float32)
    b_re = b_re.astype(jnp.float32)
    b_im = b_im.astype(jnp.float32)
    dt = jnp.exp(log_dt.astype(jnp.float32))[:, None]
    mag = jnp.exp(a_re * dt)
    ang = a_im * dt
    lb_re = mag * jnp.cos(ang)
    lb_im = mag * jnp.sin(ang)
    den = a_re * a_re + a_im * a_im
    n_re = lb_re - 1.0
    n_im = lb_im
    f_re = (n_re * a_re + n_im * a_im) / den
    f_im = (n_im * a_re - n_re * a_im) / den
    bb_re = f_re[..., None] * b_re - f_im[..., None] * b_im
    bb_im = f_re[..., None] * b_im + f_im[..., None] * b_re
    return lb_re, lb_im, bb_re, bb_im


def _complex_linear_combine(e1, e2):
    a1r, a1i, b1r, b1i = e1
    a2r, a2i, b2r, b2i = e2
    return (a2r * a1r - a2i * a1i, a2r * a1i + a2i * a1r,
            a2r * b1r - a2i * b1i + b2r, a2r * b1i + a2i * b1r + b2i)


def _s5_bidirectional(u, a_re, a_im, log_dt, b_re, b_im, c_re, c_im, d, h0):
    bsz, L = u.shape[:2]
    uf = u.astype(jnp.float32)
    ug = uf.reshape(bsz, L, N_SSM_GROUPS, SSM_GROUP)
    h0 = h0.astype(jnp.float32)
    y = d.astype(jnp.float32) * uf
    finals = []
    for di in range(2):
        lb_re, lb_im, bb_re, bb_im = _zoh(a_re[di], a_im[di], log_dt[di], b_re[di], b_im[di])
        bu_re = jnp.einsum("blgn,gpn->blgp", ug, bb_re)
        bu_im = jnp.einsum("blgn,gpn->blgp", ug, bb_im)
        if di == 1:
            bu_re = jnp.flip(bu_re, axis=1)
            bu_im = jnp.flip(bu_im, axis=1)
        hr, hi = h0[:, di, ..., 0], h0[:, di, ..., 1]
        bu_re = bu_re.at[:, 0].add(lb_re * hr - lb_im * hi)
        bu_im = bu_im.at[:, 0].add(lb_re * hi + lb_im * hr)
        a_r = jnp.broadcast_to(lb_re, bu_re.shape)
        a_i = jnp.broadcast_to(lb_im, bu_im.shape)
        _, _, s_re, s_im = lax.associative_scan(_complex_linear_combine, (a_r, a_i, bu_re, bu_im), axis=1)
        finals.append(jnp.stack([s_re[:, -1], s_im[:, -1]], axis=-1))
        if di == 1:
            s_re = jnp.flip(s_re, axis=1)
            s_im = jnp.flip(s_im, axis=1)
        yd = (jnp.einsum("blgp,gnp->blgn", s_re, c_re[di].astype(jnp.float32))
              - jnp.einsum("blgp,gnp->blgn", s_im, c_im[di].astype(jnp.float32)))
        y = y + yd.reshape(bsz, L, SSM_W)
    state = jnp.stack(finals, axis=1)
    return y.astype(u.dtype), state


def _fourier_mix(f):
    bsz, L = f.shape[:2]
    fg = f.astype(jnp.float32).reshape(bsz, L, FOURIER_GROUPS, FOURIER_GW)
    out = jnp.fft.fft2(fg, axes=(1, 3), norm="ortho").real
    return out.reshape(bsz, L, FOURIER_W).astype(f.dtype)


def _dwconv3(h, w, b):
    hp = jnp.pad(h, ((0, 0), (1, 1), (0, 0)))
    return hp[:, :-2] * w[0] + hp[:, 1:-1] * w[1] + hp[:, 2:] * w[2] + b


def _trunk_layer(x, mod, p, rope, ctx_k, ctx_v, h0):
    bsz, L = x.shape[:2]
    sh1, sc1, g1, sh2, sc2, g2 = jnp.split(mod[:, None, :], 6, axis=-1)
    u = x * (1.0 + sc1) + sh1
    z = u @ p["w_in"]
    q, k, v, s_in, f_in, gates = jnp.split(z, SPLITS, axis=-1)
    q = _rms_norm(q.reshape(bsz, L, N_Q_HEADS, HEAD_DIM), p["q_norm_g"])
    k = _rms_norm(k.reshape(bsz, L, N_KV_HEADS, HEAD_DIM), p["k_norm_g"])
    v = v.reshape(bsz, L, N_KV_HEADS, HEAD_DIM)
    if rope is None:
        keys, vals = k, v
    else:
        cos, sin = rope
        q = _apply_rope(q, cos, sin)
        k = _apply_rope(k, cos, sin)
        keys = jnp.concatenate([k, ctx_k.astype(k.dtype)], axis=1)
        vals = jnp.concatenate([v, ctx_v.astype(v.dtype)], axis=1)
    attn = _block_attention(q, keys, vals)
    ssm_y, ssm_state = _s5_bidirectional(s_in, p["ssm_a_re"], p["ssm_a_im"], p["ssm_log_dt"],
                                         p["ssm_b_re"], p["ssm_b_im"], p["ssm_c_re"], p["ssm_c_im"],
                                         p["ssm_d"], h0)
    ssm_y = jax.nn.gelu(ssm_y)
    ssm_y = ssm_y * jax.nn.sigmoid(ssm_y @ p["w_glu"] + p["b_glu"])
    four = _fourier_mix(f_in)
    g_a, g_s, g_f = jnp.split(jax.nn.sigmoid(gates), N_BRANCH, axis=-1)
    merged = (g_a * (attn @ p["w_br_attn"]) + g_s * (ssm_y @ p["w_br_ssm"])
              + g_f * (four @ p["w_br_four"]))
    mix_out = merged @ p["w_out"]
    x = _layer_norm(ALPHA * x + g1 * mix_out, p["ln1_g"], p["ln1_b"])
    u2 = x * (1.0 + sc2) + sh2
    h = _dwconv3(u2 @ p["w_up"], p["conv_w"], p["conv_b"])
    ha, hb = jnp.split(h, 2, axis=-1)
    ffn_out = (jax.nn.silu(ha) * hb) @ p["w_down"]
    x = _layer_norm(ALPHA * x + g2 * ffn_out, p["ln2_g"], p["ln2_b"])
    return x, k, v, ssm_state


def setup_inputs(seed: int = 0) -> dict:
    key = jax.random.key(seed)
    ks = jax.random.split(key, 40)
    f32 = jnp.float32
    nrm = lambda k_, shape, s: jax.random.normal(k_, shape, f32) * s
    G, P, N = N_SSM_GROUPS, STATE_P, SSM_GROUP
    n_idx = jnp.arange(P, dtype=f32)
    inp = {}
    inp["x_prompt"] = nrm(ks[0], (BATCH, SEQ, D_MODEL), 1.0)
    inp["x_sample"] = nrm(ks[1], (DEC_BATCH, DEC_SEQ, D_MODEL), 1.0)
    inp["c"] = nrm(ks[2], (DEC_BATCH, D_MODEL), 1.0)
    inp["cache_k"] = nrm(ks[3], (DEC_BATCH, DEPTH, PAST_LEN, N_KV_HEADS, HEAD_DIM), 1.0)
    inp["cache_v"] = nrm(ks[4], (DEC_BATCH, DEPTH, PAST_LEN, N_KV_HEADS, HEAD_DIM), 1.0)
    inp["state_ssm"] = nrm(ks[5], (DEC_BATCH, DEPTH, 2, G, P, 2), 0.3)
    inp["c_ctx"] = nrm(ks[6], (D_MODEL,), 1.0)
    inp["w_ada"] = nrm(ks[7], (DEPTH, D_MODEL, 6 * D_MODEL), 0.5 * D_MODEL ** -0.5)
    inp["b_ada"] = nrm(ks[8], (DEPTH, 6 * D_MODEL), 0.01)
    inp["w_in"] = nrm(ks[9], (DEPTH, D_MODEL, D_IN), D_MODEL ** -0.5)
    inp["q_norm_g"] = 1.0 + nrm(ks[10], (DEPTH, HEAD_DIM), 0.01)
    inp["k_norm_g"] = 1.0 + nrm(ks[11], (DEPTH, HEAD_DIM), 0.01)
    inp["ssm_a_re"] = -0.5 + nrm(ks[12], (DEPTH, 2, G, P), 0.01)
    inp["ssm_a_im"] = math.pi * n_idx + nrm(ks[13], (DEPTH, 2, G, P), 0.01)
    inp["ssm_log_dt"] = jax.random.uniform(ks[14], (DEPTH, 2, G), f32, math.log(DT_MIN), math.log(DT_MAX))
    inp["ssm_b_re"] = nrm(ks[15], (DEPTH, 2, G, P, N), (2 * N) ** -0.5)
    inp["ssm_b_im"] = nrm(ks[16], (DEPTH, 2, G, P, N), (2 * N) ** -0.5)
    inp["ssm_c_re"] = nrm(ks[17], (DEPTH, 2, G, N, P), (2 * P) ** -0.5)
    inp["ssm_c_im"] = nrm(ks[18], (DEPTH, 2, G, N, P), (2 * P) ** -0.5)
    inp["ssm_d"] = nrm(ks[19], (DEPTH, SSM_W), 1.0)
    inp["w_glu"] = nrm(ks[20], (DEPTH, SSM_W, SSM_W), SSM_W ** -0.5)
    inp["b_glu"] = nrm(ks[21], (DEPTH, SSM_W), 0.01)
    inp["w_br_attn"] = nrm(ks[22], (DEPTH, ATTN_W, D_MODEL), ATTN_W ** -0.5)
    inp["w_br_ssm"] = nrm(ks[23], (DEPTH, SSM_W, D_MODEL), SSM_W ** -0.5)
    inp["w_br_four"] = nrm(ks[24], (DEPTH, FOURIER_W, D_MODEL), FOURIER_W ** -0.5)
    inp["w_out"] = nrm(ks[25], (DEPTH, D_MODEL, D_MODEL), BETA * D_MODEL ** -0.5)
    inp["ln1_g"] = 1.0 + nrm(ks[26], (DEPTH, D_MODEL), 0.01)
    inp["ln1_b"] = nrm(ks[27], (DEPTH, D_MODEL), 0.01)
    inp["w_up"] = nrm(ks[28], (DEPTH, D_MODEL, 2 * D_FF), D_MODEL ** -0.5)
    inp["conv_w"] = nrm(ks[29], (DEPTH, CONV_W, 2 * D_FF), CONV_W ** -0.5)
    inp["conv_b"] = nrm(ks[30], (DEPTH, 2 * D_FF), 0.01)
    inp["w_down"] = nrm(ks[31], (DEPTH, D_FF, D_MODEL), BETA * D_FF ** -0.5)
    inp["ln2_g"] = 1.0 + nrm(ks[32], (DEPTH, D_MODEL), 0.01)
    inp["ln2_b"] = nrm(ks[33], (DEPTH, D_MODEL), 0.01)
    return inp


def reference(x_prompt, x_sample, c, cache_k, cache_v, state_ssm, c_ctx,
              w_ada, b_ada, w_in, q_norm_g, k_norm_g,
              ssm_a_re, ssm_a_im, ssm_log_dt, ssm_b_re, ssm_b_im, ssm_c_re, ssm_c_im, ssm_d,
              w_glu, b_glu, w_br_attn, w_br_ssm, w_br_four, w_out, ln1_g, ln1_b,
              w_up, conv_w, conv_b, w_down, ln2_g, ln2_b):
    stacked = dict(w_in=w_in, q_norm_g=q_norm_g, k_norm_g=k_norm_g,
                   ssm_a_re=ssm_a_re, ssm_a_im=ssm_a_im, ssm_log_dt=ssm_log_dt,
                   ssm_b_re=ssm_b_re, ssm_b_im=ssm_b_im, ssm_c_re=ssm_c_re, ssm_c_im=ssm_c_im,
                   ssm_d=ssm_d, w_glu=w_glu, b_glu=b_glu, w_br_attn=w_br_attn, w_br_ssm=w_br_ssm,
                   w_br_four=w_br_four, w_out=w_out, ln1_g=ln1_g, ln1_b=ln1_b,
                   w_up=w_up, conv_w=conv_w, conv_b=conv_b, w_down=w_down, ln2_g=ln2_g, ln2_b=ln2_b)

    h = x_prompt
    h0_ctx = jnp.zeros((x_prompt.shape[0], 2, N_SSM_GROUPS, STATE_P, 2), jnp.float32)
    ks_out, vs_out, ss_out = [], [], []
    for l in range(DEPTH):
        p = {name: arr[l] for name, arr in stacked.items()}
        mod_ctx = jax.nn.silu(c_ctx)[None, :] @ w_ada[l] + b_ada[l]
        h, k_l, v_l, s_l = _trunk_layer(h, mod_ctx, p, None, None, None, h0_ctx)
        ks_out.append(k_l)
        vs_out.append(v_l)
        ss_out.append(s_l)
    new_cache_k = jnp.stack(ks_out, axis=1)
    new_cache_v = jnp.stack(vs_out, axis=1)
    new_state_ssm = jnp.stack(ss_out, axis=1)

    rope = _axial_rope_tables(x_sample.shape[1])
    zx = x_sample
    for l in range(DEPTH):
        p = {name: arr[l] for name, arr in stacked.items()}
        mod = jax.nn.silu(c) @ w_ada[l] + b_ada[l]
        zx, _, _, _ = _trunk_layer(zx, mod, p, rope, cache_k[:, l], cache_v[:, l], state_ssm[:, l])

    return (h, zx, new_cache_k, new_cache_v, new_state_ssm)
```

```python
import functools
import math

import jax
import jax.numpy as jnp
import numpy as np
from jax import lax
from jax.experimental import pallas as pl
from jax.experimental.pallas import tpu as pltpu

F32 = jnp.float32
BF16 = jnp.bfloat16

D_MODEL = 1024
DEPTH = 2
GRID_W = 64
HEAD_DIM = 64
N_Q_HEADS = 8
N_KV_HEADS = 2
GQA_GROUP = N_Q_HEADS // N_KV_HEADS
ATTN_W = N_Q_HEADS * HEAD_DIM
KV_W = N_KV_HEADS * HEAD_DIM
ROPE_AXIS_PAIRS = HEAD_DIM // 4
ROPE_THETA = 10000.0
SSM_W = D_MODEL // 2
SSM_GROUP = 16
N_SSM_GROUPS = SSM_W // SSM_GROUP
STATE_P = 64
N_STATES = N_SSM_GROUPS * STATE_P
FOURIER_W = D_MODEL // 2
FOURIER_GROUPS = 4
FOURIER_GW = FOURIER_W // FOURIER_GROUPS
N_BRANCH = 3
GATE_W = N_BRANCH * D_MODEL
D_IN = ATTN_W + 2 * KV_W + SSM_W + FOURIER_W + GATE_W
OFF_K = ATTN_W
OFF_V = ATTN_W + KV_W
OFF_S = ATTN_W + 2 * KV_W
OFF_F = OFF_S + SSM_W
OFF_G = OFF_F + FOURIER_W
D_FF = ((8 * D_MODEL // 3 + 127) // 128) * 128
EPS = 1e-6
ALPHA = (2 * DEPTH) ** 0.25

V7X_VMEM_BYTES = 64 * 1024 * 1024
VMEM_LIMIT_BYTES = V7X_VMEM_BYTES - 8 * 1024 * 1024
LANES = 128
SSM_BUNDLE_GROUPS = LANES // SSM_GROUP
SSM_BUNDLES = N_SSM_GROUPS // SSM_BUNDLE_GROUPS
BUNDLE_STATES = SSM_BUNDLE_GROUPS * STATE_P

ROW_TILE = 256
FFN_ROWS = 1024
FFN_CHUNK = 256
SCAN_ROWS = 512
SCAN_LANES = 512
MOD_COLS = 1536
MOD_ROWS = 16


def _params(*sem):
    return pltpu.CompilerParams(dimension_semantics=sem, vmem_limit_bytes=VMEM_LIMIT_BYTES)


def _dot(a, b):
    return jnp.dot(a, b, preferred_element_type=F32)


def _sigmoid(x):
    return 1.0 / (1.0 + jnp.exp(-x))


def _mod_kernel(c_ref, w_ref, b_ref, o_ref):
    c = c_ref[...]
    s = (c * _sigmoid(c)).astype(BF16)
    o_ref[...] = _dot(s, w_ref[...].astype(BF16)) + b_ref[...]


def _mod_call(cvec, w_ada, b_ada):
    n = 6 * D_MODEL
    return pl.pallas_call(
        _mod_kernel,
        grid=(DEPTH, n // MOD_COLS),
        in_specs=[
            pl.BlockSpec((MOD_ROWS, D_MODEL), lambda l, j: (0, 0)),
            pl.BlockSpec((None, D_MODEL, MOD_COLS), lambda l, j: (l, 0, j)),
            pl.BlockSpec((None, 1, MOD_COLS), lambda l, j: (l, 0, j)),
        ],
        out_specs=pl.BlockSpec((None, MOD_ROWS, MOD_COLS), lambda l, j: (l, 0, j)),
        out_shape=jax.ShapeDtypeStruct((DEPTH, MOD_ROWS, n), F32),
        compiler_params=_params("parallel", "parallel"),
        name="mod",
    )(cvec, w_ada, b_ada.reshape(DEPTH, 1, n))


def _swap_pairs(x):
    w = x.shape[-1]
    nxt = pltpu.roll(x, w - 1, axis=1)
    prv = pltpu.roll(x, 1, axis=1)
    lane = lax.broadcasted_iota(jnp.int32, x.shape, 1)
    return jnp.where((lane & 1) == 0, nxt, prv)


def _inproj_kernel(*refs, rope):
    if rope:
        (x_ref, sc_ref, sh_ref, w_ref, gq_ref, gk_ref, ones_ref, cos_ref, sin_ref,
         q_ref, k_ref, v_ref, s_ref, f_ref, g_ref) = refs
    else:
        (x_ref, sc_ref, sh_ref, w_ref, gq_ref, gk_ref, ones_ref,
         q_ref, k_ref, v_ref, s_ref, f_ref, g_ref) = refs
    u = (x_ref[...] * (1.0 + sc_ref[...]) + sh_ref[...]).astype(BF16)

    zq = _dot(u, w_ref[:, 0:OFF_K])
    zk = _dot(u, w_ref[:, OFF_K:OFF_V])
    zv = _dot(u, w_ref[:, OFF_V:OFF_S])
    msq_q = _dot((zq * zq).astype(BF16), ones_ref[...])
    msq_k = _dot((zk * zk).astype(BF16), ones_ref[0:KV_W, 0:KV_W])
    q = zq * lax.rsqrt(msq_q + EPS) * gq_ref[...]
    k = zk * lax.rsqrt(msq_k + EPS) * gk_ref[...]
    if rope:
        cos = cos_ref[...]
        sin = sin_ref[...]
        q = q * cos + _swap_pairs(q) * sin
        k = k * cos[:, 0:KV_W] + _swap_pairs(k) * sin[:, 0:KV_W]
    q_ref[...] = (q * (HEAD_DIM ** -0.5)).astype(q_ref.dtype)
    k_ref[...] = k.astype(k_ref.dtype)
    v_ref[...] = zv.astype(v_ref.dtype)
    s_ref[...] = _dot(u, w_ref[:, OFF_S:OFF_F]).astype(s_ref.dtype)
    f_ref[...] = _dot(u, w_ref[:, OFF_F:OFF_G]).astype(f_ref.dtype)
    g_ref[...] = _sigmoid(_dot(u, w_ref[:, OFF_G:D_IN])).astype(g_ref.dtype)


def _mod_spec(chunk, tiles_per_seq, per_seq):
    if per_seq:
        return pl.BlockSpec((None, 1, D_MODEL), lambda i: (i // tiles_per_seq, 0, chunk))
    return pl.BlockSpec((None, 1, D_MODEL), lambda i: (0, 0, chunk))


def _inproj_call(x, mod3, w_in, gq, gk, ones_blk, rope_tabs, seq_len, kv_dtype):
    rows = x.shape[0]
    tps = seq_len // ROW_TILE
    per_seq = mod3.shape[0] > 1
    rope = rope_tabs is not None
    row = lambda w: pl.BlockSpec((ROW_TILE, w), lambda i: (i, 0))
    full = lambda a: pl.BlockSpec(a.shape, lambda i: (0,) * a.ndim)
    in_specs = [row(D_MODEL), _mod_spec(1, tps, per_seq), _mod_spec(0, tps, per_seq),
                full(w_in), full(gq), full(gk), full(ones_blk)]
    args = [x, mod3, mod3, w_in, gq, gk, ones_blk]
    if rope:
        tab = pl.BlockSpec((ROW_TILE, ATTN_W), lambda i: (i % tps, 0))
        in_specs += [tab, tab]
        args += list(rope_tabs)
    widths = (ATTN_W, KV_W, KV_W, SSM_W, FOURIER_W, GATE_W)
    dtypes = (BF16, kv_dtype, kv_dtype, BF16, BF16, BF16)
    return pl.pallas_call(
        functools.partial(_inproj_kernel, rope=rope),
        grid=(rows // ROW_TILE,),
        in_specs=in_specs,
        out_specs=[row(w) for w in widths],
        out_shape=[jax.ShapeDtypeStruct((rows, w), d) for w, d in zip(widths, dtypes)],
        compiler_params=_params("parallel"),
        name="in_proj",
    )(*args)


def _attn_kernel(*refs, has_cache):
    if has_cache:
        q_ref, k_ref, v_ref, ck_ref, cv_ref, o_ref = refs
    else:
        q_ref, k_ref, v_ref, o_ref = refs
    qb = q_ref.shape[0]
    q = q_ref[...].astype(F32)
    contract_last = (((1,), (1,)), ((), ()))
    outs = [None] * N_Q_HEADS
    for j in range(N_KV_HEADS):
        lo, hi = j * HEAD_DIM, (j + 1) * HEAD_DIM
        heads = range(j * GQA_GROUP, (j + 1) * GQA_GROUP)
        qs = jnp.concatenate([q[:, h * HEAD_DIM:(h + 1) * HEAD_DIM] for h in heads], axis=0).astype(BF16)
        kj = k_ref[:, lo:hi].astype(BF16)
        vj = v_ref[:, lo:hi].astype(BF16)
        s1 = lax.dot_general(qs, kj, contract_last, preferred_element_type=F32)
        m = jnp.max(s1, axis=-1, keepdims=True)
        if has_cache:
            ckj = ck_ref[:, lo:hi].astype(BF16)
            cvj = cv_ref[:, lo:hi].astype(BF16)
            s2 = lax.dot_general(qs, ckj, contract_last, preferred_element_type=F32)
            m = jnp.maximum(m, jnp.max(s2, axis=-1, keepdims=True))
        p1 = jnp.exp(s1 - m)
        den = jnp.sum(p1, axis=-1, keepdims=True)
        o = _dot(p1.astype(BF16), vj)
        if has_cache:
            p2 = jnp.exp(s2 - m)
            den = den + jnp.sum(p2, axis=-1, keepdims=True)
            o = o + _dot(p2.astype(BF16), cvj)
        o = o * (1.0 / den)
        for g, h in enumerate(heads):
            outs[h] = o[g * qb:(g + 1) * qb]
    o_ref[...] = jnp.concatenate(outs, axis=1).astype(o_ref.dtype)


def _attn_call(q, k, v, cache_k, cache_v, batch, seq_len):
    rows = q.shape[0]
    qb = ROW_TILE
    nq = seq_len // qb
    has_cache = cache_k is not None
    in_specs = [pl.BlockSpec((qb, ATTN_W), lambda b, i: (b * nq + i, 0)),
                pl.BlockSpec((seq_len, KV_W), lambda b, i: (b, 0)),
                pl.BlockSpec((seq_len, KV_W), lambda b, i: (b, 0))]
    args = [q, k, v]
    if has_cache:
        past = cache_k.shape[0] // batch
        in_specs += [pl.BlockSpec((past, KV_W), lambda b, i: (b, 0))] * 2
        args += [cache_k, cache_v]
    return pl.pallas_call(
        functools.partial(_attn_kernel, has_cache=has_cache),
        grid=(batch, nq),
        in_specs=in_specs,
        out_specs=pl.BlockSpec((qb, ATTN_W), lambda b, i: (b * nq + i, 0)),
        out_shape=jax.ShapeDtypeStruct((rows, ATTN_W), BF16),
        compiler_params=_params("parallel", "parallel"),
        name="attention",
    )(*args)


def _fourier_kernel(f_ref, wc_ref, wl_ref, o_ref):
    f = f_ref[...]
    parts = [_dot(f[:, g * FOURIER_GW:(g + 1) * FOURIER_GW], wc_ref[...]) for g in range(FOURIER_GROUPS)]
    xc = jnp.concatenate([p[:, 0:FOURIER_GW] for p in parts], axis=1)
    xs = jnp.concatenate([p[:, FOURIER_GW:2 * FOURIER_GW] for p in parts], axis=1)
    stacked = jnp.concatenate([xc, xs], axis=0).astype(BF16)
    o_ref[...] = _dot(wl_ref[...], stacked).astype(o_ref.dtype)


def _dft_tables(seq_len):
    c = np.arange(FOURIER_GW)
    ang_c = 2.0 * np.pi * np.outer(c, c) / FOURIER_GW
    wc = np.concatenate([np.cos(ang_c), np.sin(ang_c)], axis=1) / math.sqrt(FOURIER_GW)
    t = np.arange(seq_len)
    ang_l = 2.0 * np.pi * np.outer(t, t) / seq_len
    wl = np.concatenate([np.cos(ang_l), -np.sin(ang_l)], axis=1) / math.sqrt(seq_len)
    return (jnp.asarray(wc, F32).astype(BF16), jnp.asarray(wl, F32).astype(BF16))


def _fourier_call(f, batch, seq_len):
    wc, wl = _dft_tables(seq_len)
    return pl.pallas_call(
        _fourier_kernel,
        grid=(batch,),
        in_specs=[pl.BlockSpec((seq_len, FOURIER_W), lambda b: (b, 0)),
                  pl.BlockSpec(wc.shape, lambda b: (0, 0)),
                  pl.BlockSpec(wl.shape, lambda b: (0, 0))],
        out_specs=pl.BlockSpec((seq_len, FOURIER_W), lambda b: (b, 0)),
        out_shape=jax.ShapeDtypeStruct(f.shape, BF16),
        compiler_params=_params("parallel"),
        name="fourier",
    )(f, wc, wl)


def _zoh_kernel(are_ref, aim_ref, ldt_ref, bre_ref, bim_ref, lre_ref, lim_ref, bbre_ref, bbim_ref):
    a_re = are_ref[...]
    a_im = aim_ref[...]
    dt = jnp.exp(ldt_ref[...])
    mag = jnp.exp(a_re * dt)
    ang = a_im * dt
    lb_re = mag * jnp.cos(ang)
    lb_im = mag * jnp.sin(ang)
    den = a_re * a_re + a_im * a_im
    n_re = lb_re - 1.0
    n_im = lb_im
    f_re = (n_re * a_re + n_im * a_im) / den
    f_im = (n_im * a_re - n_re * a_im) / den
    lre_ref[...] = lb_re
    lim_ref[...] = lb_im
    b_re = bre_ref[...]
    b_im = bim_ref[...]
    bbre_ref[...] = f_re[:, None, :] * b_re - f_im[:, None, :] * b_im
    bbim_ref[...] = f_re[:, None, :] * b_im + f_im[:, None, :] * b_re


def _zoh_call(a_re, a_im, log_dt, b_re, b_im):
    rows = DEPTH * 2 * N_SSM_GROUPS
    are = a_re.reshape(rows, STATE_P)
    aim = a_im.reshape(rows, STATE_P)
    ldt = log_dt.reshape(rows, 1)
    bre = jnp.swapaxes(b_re, -1, -2).reshape(rows, SSM_GROUP, STATE_P)
    bim = jnp.swapaxes(b_im, -1, -2).reshape(rows, SSM_GROUP, STATE_P)
    full = lambda a: pl.BlockSpec(a.shape, lambda: (0,) * a.ndim)
    outs = [jax.ShapeDtypeStruct((rows, STATE_P), F32)] * 2 + [jax.ShapeDtypeStruct((rows, SSM_GROUP, STATE_P), F32)] * 2
    return pl.pallas_call(
        _zoh_kernel,
        in_specs=[full(are), full(aim), full(ldt), full(bre), full(bim)],
        out_specs=[full(o) for o in outs],
        out_shape=outs,
        name="s5_zoh",
    )(are, aim, ldt, bre, bim)


def _s5_weights(lb_re, lb_im, bb_re, bb_im, c_re, c_im):
    shp = (DEPTH, 2, SSM_BUNDLES, SSM_BUNDLE_GROUPS)
    eye = jnp.eye(SSM_BUNDLE_GROUPS, dtype=F32)
    lam_re = lb_re.reshape(DEPTH, 2, 1, N_STATES)
    lam_im = lb_im.reshape(DEPTH, 2, 1, N_STATES)

    def in_blocks(bb):
        bb = bb.reshape(*shp, SSM_GROUP, STATE_P)
        blk = bb[:, :, :, :, :, None, :] * eye[None, None, None, :, None, :, None]
        return blk.reshape(DEPTH, 2, SSM_BUNDLES, LANES, BUNDLE_STATES)

    def out_blocks(c):
        c = c.reshape(*shp, SSM_GROUP, STATE_P)
        ct = jnp.swapaxes(c, -1, -2)
        blk = ct[:, :, :, :, :, None, :] * eye[None, None, None, :, None, :, None]
        return blk.reshape(DEPTH, 2, SSM_BUNDLES, BUNDLE_STATES, LANES)

    bw = jnp.concatenate([in_blocks(bb_re), in_blocks(bb_im)], axis=-1).astype(BF16)
    cw = jnp.concatenate([out_blocks(c_re), -out_blocks(c_im)], axis=-2).astype(BF16)
    return lam_re, lam_im, bw, cw


def _scan_chunk(u_ref, bw_ref, cw_ref, lre_ref, lim_ref, h0_ref, hs, hc, *, batch, reverse):
    steps = u_ref.shape[0] // batch

    @pl.when(pl.program_id(0) == 0)
    def _():
        hc[...] = h0_ref[...]

    u = u_ref[...]
    for k in range(SSM_BUNDLES):
        r = _dot(u[:, k * LANES:(k + 1) * LANES], bw_ref[k])
        hs[:, k * BUNDLE_STATES:(k + 1) * BUNDLE_STATES] = r[:, 0:BUNDLE_STATES]
        hs[:, N_STATES + k * BUNDLE_STATES:N_STATES + (k + 1) * BUNDLE_STATES] = r[:, BUNDLE_STATES:]

    for blk in range(N_STATES // SCAN_LANES):
        re = slice(blk * SCAN_LANES, (blk + 1) * SCAN_LANES)
        im = slice(N_STATES + blk * SCAN_LANES, N_STATES + (blk + 1) * SCAN_LANES)
        lr = jnp.broadcast_to(lre_ref[:, re], (batch, SCAN_LANES))
        li = jnp.broadcast_to(lim_ref[:, re], (batch, SCAN_LANES))

        def step(t, carry, re=re, im=im, lr=lr, li=li):
            hr, hi = carry
            tt = steps - 1 - t if reverse else t
            row = pl.multiple_of(tt * batch, batch)
            nr = lr * hr - li * hi + hs[pl.ds(row, batch), re]
            ni = lr * hi + li * hr + hs[pl.ds(row, batch), im]
            hs[pl.ds(row, batch), re] = nr
            hs[pl.ds(row, batch), im] = ni
            return nr, ni

        hr, hi = lax.fori_loop(0, steps, step, (hc[:, re], hc[:, im]), unroll=8)
        hc[:, re] = hr
        hc[:, im] = hi

    ys = []
    for k in range(SSM_BUNDLES):
        h_re = hs[:, k * BUNDLE_STATES:(k + 1) * BUNDLE_STATES].astype(BF16)
        h_im = hs[:, N_STATES + k * BUNDLE_STATES:N_STATES + (k + 1) * BUNDLE_STATES].astype(BF16)
        ys.append(_dot(h_re, cw_ref[k, 0:BUNDLE_STATES, :]) + _dot(h_im, cw_ref[k, BUNDLE_STATES:, :]))
    return u, ys


def _s5_fwd_kernel(u_ref, bw_ref, cw_ref, lre_ref, lim_ref, h0_ref, y_ref, hfin_ref, hs, hc, *, batch):
    _, ys = _scan_chunk(u_ref, bw_ref, cw_ref, lre_ref, lim_ref, h0_ref, hs, hc, batch=batch, reverse=False)
    for k, y in enumerate(ys):
        y_ref[:, k * LANES:(k + 1) * LANES] = y.astype(y_ref.dtype)
    hfin_ref[...] = hc[...]


def _gelu_tanh(x):
    return 0.5 * x * (1.0 + jnp.tanh(math.sqrt(2.0 / math.pi) * (x + 0.044715 * (x * x * x))))


def _s5_bwd_kernel(u_ref, yf_ref, bw_ref, cw_ref, lre_ref, lim_ref, h0_ref, d_ref, wg_ref, bg_ref,
                   o_ref, hfin_ref, hs, hc, *, batch):
    u, ys = _scan_chunk(u_ref, bw_ref, cw_ref, lre_ref, lim_ref, h0_ref, hs, hc, batch=batch, reverse=True)
    y = d_ref[...] * u.astype(F32) + yf_ref[...].astype(F32) + jnp.concatenate(ys, axis=1)
    y = _gelu_tanh(y)
    gate = _sigmoid(_dot(y.astype(BF16), wg_ref[...]) + bg_ref[...])
    o_ref[...] = (y * gate).astype(o_ref.dtype)
    hfin_ref[...] = hc[...]


def _s5_calls(u_tm, batch, lam_re, lam_im, bw, cw, h0, d_skip, w_glu, b_glu):
    rows = u_tm.shape[0]
    n_chunks = rows // SCAN_ROWS
    chunk = lambda w, rev: pl.BlockSpec((SCAN_ROWS, w), (lambda i: (n_chunks - 1 - i, 0)) if rev else (lambda i: (i, 0)))
    full = lambda a: pl.BlockSpec(a.shape, lambda i: (0,) * a.ndim)
    state = jax.ShapeDtypeStruct((batch, 2 * N_STATES), F32)
    scratch = [pltpu.VMEM((SCAN_ROWS, 2 * N_STATES), F32), pltpu.VMEM((batch, 2 * N_STATES), F32)]

    yf, hfin_f = pl.pallas_call(
        functools.partial(_s5_fwd_kernel, batch=batch),
        grid=(n_chunks,),
        in_specs=[chunk(SSM_W, False), full(bw[0]), full(cw[0]), full(lam_re[0]), full(lam_im[0]), full(h0[0])],
        out_specs=[chunk(SSM_W, False), full(h0[0])],
        out_shape=[jax.ShapeDtypeStruct((rows, SSM_W), BF16), state],
        scratch_shapes=scratch,
        compiler_params=_params("arbitrary"),
        name="s5_fwd",
    )(u_tm, bw[0], cw[0], lam_re[0], lam_im[0], h0[0])

    out, hfin_b = pl.pallas_call(
        functools.partial(_s5_bwd_kernel, batch=batch),
        grid=(n_chunks,),
        in_specs=[chunk(SSM_W, True), chunk(SSM_W, True), full(bw[1]), full(cw[1]), full(lam_re[1]),
                  full(lam_im[1]), full(h0[1]), full(d_skip), full(w_glu), full(b_glu)],
        out_specs=[chunk(SSM_W, True), full(h0[1])],
        out_shape=[jax.ShapeDtypeStruct((rows, SSM_W), BF16), state],
        scratch_shapes=scratch,
        compiler_params=_params("arbitrary"),
        name="s5_bwd",
    )(u_tm, yf, bw[1], cw[1], lam_re[1], lam_im[1], h0[1], d_skip, w_glu, b_glu)
    return out, hfin_f, hfin_b


def _layer_norm_rows(y, g, b):
    mu = jnp.mean(y, axis=-1, keepdims=True)
    yc = y - mu
    var = jnp.mean(yc * yc, axis=-1, keepdims=True)
    return yc * lax.rsqrt(var + EPS) * g + b


def _merge_kernel(x_ref, a_ref, s_ref, f_ref, g_ref, g1_ref, wa_ref, ws_ref, wf_ref, wo_ref, lg_ref, lb_ref, o_ref):
    merged = (g_ref[:, 0:D_MODEL].astype(F32) * _dot(a_ref[...], wa_ref[...])
              + g_ref[:, D_MODEL:2 * D_MODEL].astype(F32) * _dot(s_ref[...], ws_ref[...])
              + g_ref[:, 2 * D_MODEL:3 * D_MODEL].astype(F32) * _dot(f_ref[...], wf_ref[...]))
    mix = _dot(merged.astype(BF16), wo_ref[...])
    y = ALPHA * x_ref[...] + g1_ref[...] * mix
    o_ref[...] = _layer_norm_rows(y, lg_ref[...], lb_ref[...])


def _merge_call(x, attn, ssm, four, gates, mod3, w_ba, w_bs, w_bf, w_out, ln_g, ln_b, seq_len):
    rows = x.shape[0]
    tps = seq_len // ROW_TILE
    per_seq = mod3.shape[0] > 1
    row = lambda w: pl.BlockSpec((ROW_TILE, w), lambda i: (i, 0))
    full = lambda a: pl.BlockSpec(a.shape, lambda i: (0,) * a.ndim)
    return pl.pallas_call(
        _merge_kernel,
        grid=(rows // ROW_TILE,),
        in_specs=[row(D_MODEL), row(ATTN_W), row(SSM_W), row(FOURIER_W), row(GATE_W), _mod_spec(2, tps, per_seq),
                  full(w_ba), full(w_bs), full(w_bf), full(w_out), full(ln_g), full(ln_b)],
        out_specs=row(D_MODEL),
        out_shape=jax.ShapeDtypeStruct((rows, D_MODEL), F32),
        compiler_params=_params("parallel"),
        name="merge",
    )(x, attn, ssm, four, gates, mod3, w_ba, w_bs, w_bf, w_out, ln_g, ln_b)


def _ffn_kernel(x_ref, sc_ref, sh_ref, g2_ref, wa_ref, wb_ref, cwa_ref, cwb_ref, cba_ref, cbb_ref, wd_ref,
                lg_ref, lb_ref, o_ref, u2, acc, *, seq_len):
    c = pl.program_id(1)
    rows = x_ref.shape[0]

    @pl.when(c == 0)
    def _():
        u2[...] = (x_ref[...] * (1.0 + sc_ref[...]) + sh_ref[...]).astype(BF16)
        acc[...] = jnp.zeros_like(acc)

    u = u2[...]
    pos = lax.broadcasted_iota(jnp.int32, (rows, 1), 0) % seq_len
    first = pos == 0
    last = pos == seq_len - 1

    def conv(h, cw_ref, cb_ref):
        prev = jnp.where(first, 0.0, pltpu.roll(h, 1, axis=0))
        nxt = jnp.where(last, 0.0, pltpu.roll(h, rows - 1, axis=0))
        return prev * cw_ref[0:1, :] + h * cw_ref[1:2, :] + nxt * cw_ref[2:3, :] + cb_ref[...]

    a = conv(_dot(u, wa_ref[...]), cwa_ref, cba_ref)
    b = conv(_dot(u, wb_ref[...]), cwb_ref, cbb_ref)
    gated = (a * _sigmoid(a)) * b
    acc[...] += _dot(gated.astype(BF16), wd_ref[...])

    @pl.when(c == pl.num_programs(1) - 1)
    def _():
        y = ALPHA * x_ref[...] + g2_ref[...] * acc[...]
        o_ref[...] = _layer_norm_rows(y, lg_ref[...], lb_ref[...])


def _ffn_call(x, mod3, w_up, conv_w, conv_b, w_down, ln_g, ln_b, seq_len):
    rows = x.shape[0]
    tile = FFN_ROWS
    n_chunks = D_FF // FFN_CHUNK
    seqs_per_tile = tile // seq_len
    per_seq = mod3.shape[0] > 1
    if per_seq:
        mod = lambda ch: pl.BlockSpec((None, 1, D_MODEL), lambda i, c: (i * seqs_per_tile, 0, ch))
    else:
        mod = lambda ch: pl.BlockSpec((None, 1, D_MODEL), lambda i, c: (0, 0, ch))
    rowspec = pl.BlockSpec((tile, D_MODEL), lambda i, c: (i, 0))
    vec = lambda a: pl.BlockSpec(a.shape, lambda i, c: (0, 0))
    return pl.pallas_call(
        functools.partial(_ffn_kernel, seq_len=seq_len),
        grid=(rows // tile, n_chunks),
        in_specs=[rowspec, mod(4), mod(3), mod(5),
                  pl.BlockSpec((D_MODEL, FFN_CHUNK), lambda i, c: (0, c)),
                  pl.BlockSpec((D_MODEL, FFN_CHUNK), lambda i, c: (0, n_chunks + c)),
                  pl.BlockSpec((3, FFN_CHUNK), lambda i, c: (0, c)),
                  pl.BlockSpec((3, FFN_CHUNK), lambda i, c: (0, n_chunks + c)),
                  pl.BlockSpec((1, FFN_CHUNK), lambda i, c: (0, c)),
                  pl.BlockSpec((1, FFN_CHUNK), lambda i, c: (0, n_chunks + c)),
                  pl.BlockSpec((FFN_CHUNK, D_MODEL), lambda i, c: (c, 0)),
                  vec(ln_g), vec(ln_b)],
        out_specs=rowspec,
        out_shape=jax.ShapeDtypeStruct((rows, D_MODEL), F32),
        scratch_shapes=[pltpu.VMEM((tile, D_MODEL), BF16), pltpu.VMEM((tile, D_MODEL), F32)],
        compiler_params=_params("parallel", "arbitrary"),
        name="conv_mlp",
    )(x, mod3, mod3, mod3, w_up, w_up, conv_w, conv_w, conv_b, conv_b, w_down, ln_g, ln_b)


def _rope_tables(n_tokens):
    rows = n_tokens // GRID_W
    row = np.repeat(np.arange(rows, dtype=np.float64), GRID_W)
    col = np.tile(np.arange(GRID_W, dtype=np.float64), rows)
    freqs = ROPE_THETA ** (-np.arange(ROPE_AXIS_PAIRS, dtype=np.float64) / ROPE_AXIS_PAIRS)
    ang = np.concatenate([row[:, None] * freqs, col[:, None] * freqs], axis=-1)
    cos = np.repeat(np.cos(ang), 2, axis=-1)
    sin = np.repeat(np.sin(ang), 2, axis=-1) * np.tile(np.array([-1.0, 1.0]), HEAD_DIM // 2)
    return (jnp.asarray(np.tile(cos, (1, N_Q_HEADS)), F32), jnp.asarray(np.tile(sin, (1, N_Q_HEADS)), F32))


def _to_time_major(a, batch, seq_len):
    return a.reshape(batch, seq_len, -1).transpose(1, 0, 2).reshape(seq_len * batch, -1)


def _to_batch_major(a, batch, seq_len):
    return a.reshape(seq_len, batch, -1).transpose(1, 0, 2).reshape(batch * seq_len, -1)


def _trunk_layer(x, mod3, lw, batch, seq_len, rope_tabs, cache_k, cache_v, h0, kv_dtype):
    q, k, v, s_in, f_in, gates = _inproj_call(x, mod3, lw["w_in"], lw["gq"], lw["gk"], lw["ones"], rope_tabs,
                                              seq_len, kv_dtype)
    attn = _attn_call(q, k, v, cache_k, cache_v, batch, seq_len)
    four = _fourier_call(f_in, batch, seq_len)
    ssm_tm, hfin_f, hfin_b = _s5_calls(_to_time_major(s_in, batch, seq_len), batch, lw["lam_re"], lw["lam_im"],
                                       lw["bw"], lw["cw"], h0, lw["d"], lw["w_glu"], lw["b_glu"])
    ssm = _to_batch_major(ssm_tm, batch, seq_len)
    x1 = _merge_call(x, attn, ssm, four, gates, mod3, lw["w_ba"], lw["w_bs"], lw["w_bf"], lw["w_out"],
                     lw["ln1_g"], lw["ln1_b"], seq_len)
    x2 = _ffn_call(x1, mod3, lw["w_up"], lw["conv_w"], lw["conv_b"], lw["w_down"], lw["ln2_g"], lw["ln2_b"], seq_len)
    return x2, k, v, hfin_f, hfin_b


def _split_state(h):
    b = h.shape[0]
    re = h[..., 0].reshape(b, 2, N_STATES)
    im = h[..., 1].reshape(b, 2, N_STATES)
    return jnp.concatenate([re, im], axis=-1).transpose(1, 0, 2)


def _join_state(hf, hb):
    def one(h):
        b = h.shape[0]
        re = h[:, 0:N_STATES].reshape(b, N_SSM_GROUPS, STATE_P)
        im = h[:, N_STATES:].reshape(b, N_SSM_GROUPS, STATE_P)
        return jnp.stack([re, im], axis=-1)
    return jnp.stack([one(hf), one(hb)], axis=1)


def kernel(x_prompt, x_sample, c, cache_k, cache_v, state_ssm, c_ctx, w_ada, b_ada, w_in, q_norm_g, k_norm_g, ssm_a_re, ssm_a_im, ssm_log_dt, ssm_b_re, ssm_b_im, ssm_c_re, ssm_c_im, ssm_d, w_glu, b_glu, w_br_attn, w_br_ssm, w_br_four, w_out, ln1_g, ln1_b, w_up, conv_w, conv_b, w_down, ln2_g, ln2_b):
    bc, lc, _ = x_prompt.shape
    bd, ld, _ = x_sample.shape
    past = cache_k.shape[2]

    cvec = jnp.zeros((MOD_ROWS, D_MODEL), F32).at[0:bd].set(c).at[bd].set(c_ctx)
    mod = _mod_call(cvec, w_ada, b_ada)

    lb_re, lb_im, bb_re, bb_im = _zoh_call(ssm_a_re, ssm_a_im, ssm_log_dt, ssm_b_re, ssm_b_im)
    lam_re, lam_im, bw, cw = _s5_weights(lb_re, lb_im, bb_re, bb_im, ssm_c_re, ssm_c_im)

    head_mean = np.kron(np.eye(N_Q_HEADS), np.full((HEAD_DIM, HEAD_DIM), 1.0 / HEAD_DIM))
    ones_blk = jnp.asarray(head_mean, F32).astype(BF16)
    rope_tabs = _rope_tables(ld)
    vec = lambda a: a.reshape(1, -1)

    layers = []
    for l in range(DEPTH):
        layers.append(dict(
            w_in=w_in[l].astype(BF16), gq=vec(jnp.tile(q_norm_g[l], N_Q_HEADS)), gk=vec(jnp.tile(k_norm_g[l], N_KV_HEADS)),
            ones=ones_blk, lam_re=lam_re[l], lam_im=lam_im[l], bw=bw[l], cw=cw[l], d=vec(ssm_d[l]),
            w_glu=w_glu[l].astype(BF16), b_glu=vec(b_glu[l]), w_ba=w_br_attn[l].astype(BF16),
            w_bs=w_br_ssm[l].astype(BF16), w_bf=w_br_four[l].astype(BF16), w_out=w_out[l].astype(BF16),
            ln1_g=vec(ln1_g[l]), ln1_b=vec(ln1_b[l]), w_up=w_up[l].astype(BF16), conv_w=conv_w[l],
            conv_b=vec(conv_b[l]), w_down=w_down[l].astype(BF16), ln2_g=vec(ln2_g[l]), ln2_b=vec(ln2_b[l])))

    h = x_prompt.reshape(bc * lc, D_MODEL)
    h0_ctx = jnp.zeros((2, bc, 2 * N_STATES), F32)
    ks, vs, ss = [], [], []
    for l in range(DEPTH):
        mod_ctx = mod[l, bd:bd + 1].reshape(1, 1, 6 * D_MODEL)
        h, k_l, v_l, hf, hb = _trunk_layer(h, mod_ctx, layers[l], bc, lc, None, None, None, h0_ctx, F32)
        ks.append(k_l.reshape(bc, lc, N_KV_HEADS, HEAD_DIM))
        vs.append(v_l.reshape(bc, lc, N_KV_HEADS, HEAD_DIM))
        ss.append(_join_state(hf, hb))
    new_cache_k = jnp.stack(ks, axis=1)
    new_cache_v = jnp.stack(vs, axis=1)
    new_state = jnp.stack(ss, axis=1)

    zx = x_sample.reshape(bd * ld, D_MODEL)
    for l in range(DEPTH):
        mod_dec = mod[l, 0:bd].reshape(bd, 1, 6 * D_MODEL)
        ck = cache_k[:, l].reshape(bd * past, KV_W)
        cv = cache_v[:, l].reshape(bd * past, KV_W)
        zx, _, _, _, _ = _trunk_layer(zx, mod_dec, layers[l], bd, ld, rope_tabs, ck, cv,
                                      _split_state(state_ssm[:, l]), BF16)

    return (h.reshape(bc, lc, D_MODEL), zx.reshape(bd, ld, D_MODEL), new_cache_k, new_cache_v, new_state)
```

```python
import functools
import math

import jax
import jax.numpy as jnp
import numpy as np
from jax import lax
from jax.experimental import pallas as pl
from jax.experimental.pallas import tpu as pltpu

F32 = jnp.float32
BF16 = jnp.bfloat16

D_MODEL = 1024
DEPTH = 2
GRID_W = 64
HEAD_DIM = 64
N_Q_HEADS = 8
N_KV_HEADS = 2
GQA_GROUP = N_Q_HEADS // N_KV_HEADS
ATTN_W = N_Q_HEADS * HEAD_DIM
KV_W = N_KV_HEADS * HEAD_DIM
ROPE_AXIS_PAIRS = HEAD_DIM // 4
ROPE_THETA = 10000.0
SSM_W = D_MODEL // 2
SSM_GROUP = 16
N_SSM_GROUPS = SSM_W // SSM_GROUP
STATE_P = 64
N_STATES = N_SSM_GROUPS * STATE_P
FOURIER_W = D_MODEL // 2
FOURIER_GROUPS = 4
FOURIER_GW = FOURIER_W // FOURIER_GROUPS
N_BRANCH = 3
GATE_W = N_BRANCH * D_MODEL
D_IN = ATTN_W + 2 * KV_W + SSM_W + FOURIER_W + GATE_W
OFF_K = ATTN_W
OFF_V = ATTN_W + KV_W
OFF_S = ATTN_W + 2 * KV_W
OFF_F = OFF_S + SSM_W
OFF_G = OFF_F + FOURIER_W
D_FF = ((8 * D_MODEL // 3 + 127) // 128) * 128
EPS = 1e-6
ALPHA = (2 * DEPTH) ** 0.25

V7X_VMEM_BYTES = 64 * 1024 * 1024
VMEM_LIMIT_BYTES = V7X_VMEM_BYTES - 8 * 1024 * 1024
LANES = 128
SUBLANES = 8
SSM_BUNDLE_GROUPS = LANES // SSM_GROUP
SSM_BUNDLES = N_SSM_GROUPS // SSM_BUNDLE_GROUPS
BUNDLE_STATES = SSM_BUNDLE_GROUPS * STATE_P

ROW_TILE = 256
ATTN_Q_ROWS = 512
FFN_ROWS = 1024
FFN_CHUNK = 256
FFN_ROW_BLOCKS = 4
SCAN_ROWS = 512
SCAN_LANES = 512
MOD_COLS = 1536
MOD_ROWS = 16


def _params(*sem):
    return pltpu.CompilerParams(dimension_semantics=sem, vmem_limit_bytes=VMEM_LIMIT_BYTES)


def _dot(a, b):
    return jnp.dot(a, b, preferred_element_type=F32)


def _sigmoid(x):
    return 1.0 / (1.0 + jnp.exp(-x))


def _layer_spec(arr, layer, n_grid):
    zeros = (0,) * (arr.ndim - 1)
    if n_grid == 1:
        return pl.BlockSpec((None,) + arr.shape[1:], lambda i: (layer,) + zeros)
    return pl.BlockSpec((None,) + arr.shape[1:], lambda i, j: (layer,) + zeros)


def _mod_spec(chunk, tiles_per_seq, per_seq):
    if per_seq:
        return pl.BlockSpec((None, 1, D_MODEL), lambda i: (i // tiles_per_seq, 0, chunk))
    return pl.BlockSpec((None, 1, D_MODEL), lambda i: (0, 0, chunk))


def _mod_kernel(c_ref, w_ref, b_ref, o_ref):
    c = c_ref[...]
    s = (c * _sigmoid(c)).astype(BF16)
    o_ref[...] = _dot(s, w_ref[...].astype(BF16)) + b_ref[...]


def _mod_call(cvec, w_ada, b_ada):
    n = 6 * D_MODEL
    return pl.pallas_call(
        _mod_kernel,
        grid=(DEPTH, n // MOD_COLS),
        in_specs=[
            pl.BlockSpec((MOD_ROWS, D_MODEL), lambda l, j: (0, 0)),
            pl.BlockSpec((None, D_MODEL, MOD_COLS), lambda l, j: (l, 0, j)),
            pl.BlockSpec((None, 1, MOD_COLS), lambda l, j: (l, 0, j)),
        ],
        out_specs=pl.BlockSpec((None, MOD_ROWS, MOD_COLS), lambda l, j: (l, 0, j)),
        out_shape=jax.ShapeDtypeStruct((DEPTH, MOD_ROWS, n), F32),
        compiler_params=_params("parallel", "parallel"),
        name="mod",
    )(cvec, w_ada, b_ada.reshape(DEPTH, 1, n))


def _swap_pairs(x):
    w = x.shape[-1]
    nxt = pltpu.roll(x, w - 1, axis=1)
    prv = pltpu.roll(x, 1, axis=1)
    lane = lax.broadcasted_iota(jnp.int32, x.shape, 1)
    return jnp.where((lane & 1) == 0, nxt, prv)


def _inproj_kernel(*refs, rope):
    if rope:
        (x_ref, sc_ref, sh_ref, w_ref, gq_ref, gk_ref, ones_ref, cos_ref, sin_ref,
         q_ref, k_ref, v_ref, s_ref, f_ref) = refs
    else:
        (x_ref, sc_ref, sh_ref, w_ref, gq_ref, gk_ref, ones_ref,
         q_ref, k_ref, v_ref, s_ref, f_ref) = refs
    u = (x_ref[...] * (1.0 + sc_ref[...]) + sh_ref[...]).astype(BF16)

    zq = _dot(u, w_ref[:, 0:OFF_K])
    zk = _dot(u, w_ref[:, OFF_K:OFF_V])
    zv = _dot(u, w_ref[:, OFF_V:OFF_S])
    msq_q = _dot((zq * zq).astype(BF16), ones_ref[...])
    msq_k = _dot((zk * zk).astype(BF16), ones_ref[0:KV_W, 0:KV_W])
    q = zq * lax.rsqrt(msq_q + EPS) * gq_ref[...]
    k = zk * lax.rsqrt(msq_k + EPS) * gk_ref[...]
    if rope:
        cos = cos_ref[...]
        sin = sin_ref[...]
        q = q * cos + _swap_pairs(q) * sin
        k = k * cos[:, 0:KV_W] + _swap_pairs(k) * sin[:, 0:KV_W]
    q_ref[...] = (q * (HEAD_DIM ** -0.5)).astype(q_ref.dtype)
    k_ref[...] = k.astype(k_ref.dtype)
    v_ref[...] = zv.astype(v_ref.dtype)
    s_ref[...] = _dot(u, w_ref[:, OFF_S:OFF_F]).astype(s_ref.dtype)
    f_ref[...] = _dot(u, w_ref[:, OFF_F:OFF_G]).astype(f_ref.dtype)


def _inproj_call(x, mod3, w_mix, gq, gk, ones_blk, rope_tabs, layer, seq_len, kv_dtype):
    rows = x.shape[0]
    tps = seq_len // ROW_TILE
    per_seq = mod3.shape[0] > 1
    rope = rope_tabs is not None
    row = lambda w: pl.BlockSpec((ROW_TILE, w), lambda i: (i, 0))
    in_specs = [row(D_MODEL), _mod_spec(1, tps, per_seq), _mod_spec(0, tps, per_seq),
                _layer_spec(w_mix, layer, 1), _layer_spec(gq, layer, 1), _layer_spec(gk, layer, 1),
                pl.BlockSpec(ones_blk.shape, lambda i: (0, 0))]
    args = [x, mod3, mod3, w_mix, gq, gk, ones_blk]
    if rope:
        tab = pl.BlockSpec((ROW_TILE, ATTN_W), lambda i: (i % tps, 0))
        in_specs += [tab, tab]
        args += list(rope_tabs)
    widths = (ATTN_W, KV_W, KV_W, SSM_W, FOURIER_W)
    dtypes = (BF16, kv_dtype, kv_dtype, BF16, BF16)
    return pl.pallas_call(
        functools.partial(_inproj_kernel, rope=rope),
        grid=(rows // ROW_TILE,),
        in_specs=in_specs,
        out_specs=[row(w) for w in widths],
        out_shape=[jax.ShapeDtypeStruct((rows, w), d) for w, d in zip(widths, dtypes)],
        compiler_params=_params("parallel"),
        name="in_proj",
    )(*args)


def _attn_kernel(*refs, has_cache):
    if has_cache:
        q_ref, k_ref, v_ref, ck_ref, cv_ref, o_ref = refs
    else:
        q_ref, k_ref, v_ref, o_ref = refs
    q = q_ref[...].astype(F32)
    contract_last = (((1,), (1,)), ((), ()))
    outs = []
    for h in range(N_Q_HEADS):
        j = h // GQA_GROUP
        lo, hi = j * HEAD_DIM, (j + 1) * HEAD_DIM
        qh = q[:, h * HEAD_DIM:(h + 1) * HEAD_DIM].astype(BF16)
        s1 = lax.dot_general(qh, k_ref[:, lo:hi].astype(BF16), contract_last, preferred_element_type=F32)
        m = jnp.max(s1, axis=-1, keepdims=True)
        if has_cache:
            s2 = lax.dot_general(qh, ck_ref[:, lo:hi], contract_last, preferred_element_type=F32)
            m = jnp.maximum(m, jnp.max(s2, axis=-1, keepdims=True))
        p1 = jnp.exp(s1 - m)
        den = jnp.sum(p1, axis=-1, keepdims=True)
        o = _dot(p1.astype(BF16), v_ref[:, lo:hi].astype(BF16))
        if has_cache:
            p2 = jnp.exp(s2 - m)
            den = den + jnp.sum(p2, axis=-1, keepdims=True)
            o = o + _dot(p2.astype(BF16), cv_ref[:, lo:hi])
        outs.append(o * (1.0 / den))
    o_ref[...] = jnp.concatenate(outs, axis=1).astype(o_ref.dtype)


def _attn_call(q, k, v, cache_k, cache_v, layer, batch, seq_len):
    rows = q.shape[0]
    qb = min(ATTN_Q_ROWS, seq_len)
    nq = seq_len // qb
    has_cache = cache_k is not None
    in_specs = [pl.BlockSpec((qb, ATTN_W), lambda b, i: (b * nq + i, 0)),
                pl.BlockSpec((seq_len, KV_W), lambda b, i: (b, 0)),
                pl.BlockSpec((seq_len, KV_W), lambda b, i: (b, 0))]
    args = [q, k, v]
    if has_cache:
        past = cache_k.shape[2]
        in_specs += [pl.BlockSpec((None, None, past, KV_W), lambda b, i: (b, layer, 0, 0))] * 2
        args += [cache_k, cache_v]
    return pl.pallas_call(
        functools.partial(_attn_kernel, has_cache=has_cache),
        grid=(batch, nq),
        in_specs=in_specs,
        out_specs=pl.BlockSpec((qb, ATTN_W), lambda b, i: (b * nq + i, 0)),
        out_shape=jax.ShapeDtypeStruct((rows, ATTN_W), BF16),
        compiler_params=_params("parallel", "parallel"),
        name="attention",
    )(*args)


def _fourier_kernel(f_ref, wc_ref, wl_ref, o_ref):
    f = f_ref[...]
    parts = [_dot(f[:, g * FOURIER_GW:(g + 1) * FOURIER_GW], wc_ref[...]) for g in range(FOURIER_GROUPS)]
    xc = jnp.concatenate([p[:, 0:FOURIER_GW] for p in parts], axis=1)
    xs = jnp.concatenate([p[:, FOURIER_GW:2 * FOURIER_GW] for p in parts], axis=1)
    stacked = jnp.concatenate([xc, xs], axis=0).astype(BF16)
    o_ref[...] = _dot(wl_ref[...], stacked).astype(o_ref.dtype)


def _dft_tables(seq_len):
    c = np.arange(FOURIER_GW)
    ang_c = 2.0 * np.pi * np.outer(c, c) / FOURIER_GW
    wc = np.concatenate([np.cos(ang_c), np.sin(ang_c)], axis=1) / math.sqrt(FOURIER_GW)
    t = np.arange(seq_len)
    ang_l = 2.0 * np.pi * np.outer(t, t) / seq_len
    wl = np.concatenate([np.cos(ang_l), -np.sin(ang_l)], axis=1) / math.sqrt(seq_len)
    return (jnp.asarray(wc, F32).astype(BF16), jnp.asarray(wl, F32).astype(BF16))


def _fourier_call(f, batch, seq_len):
    wc, wl = _dft_tables(seq_len)
    return pl.pallas_call(
        _fourier_kernel,
        grid=(batch,),
        in_specs=[pl.BlockSpec((seq_len, FOURIER_W), lambda b: (b, 0)),
                  pl.BlockSpec(wc.shape, lambda b: (0, 0)),
                  pl.BlockSpec(wl.shape, lambda b: (0, 0))],
        out_specs=pl.BlockSpec((seq_len, FOURIER_W), lambda b: (b, 0)),
        out_shape=jax.ShapeDtypeStruct(f.shape, BF16),
        compiler_params=_params("parallel"),
        name="fourier",
    )(f, wc, wl)


def _zoh_kernel(are_ref, aim_ref, ldt_ref, bre_ref, bim_ref, lre_ref, lim_ref, bbre_ref, bbim_ref):
    a_re = are_ref[...]
    a_im = aim_ref[...]
    dt = jnp.exp(ldt_ref[...])
    mag = jnp.exp(a_re * dt)
    ang = a_im * dt
    lb_re = mag * jnp.cos(ang)
    lb_im = mag * jnp.sin(ang)
    den = a_re * a_re + a_im * a_im
    n_re = lb_re - 1.0
    n_im = lb_im
    f_re = (n_re * a_re + n_im * a_im) / den
    f_im = (n_im * a_re - n_re * a_im) / den
    lre_ref[...] = lb_re
    lim_ref[...] = lb_im
    b_re = bre_ref[...]
    b_im = bim_ref[...]
    bbre_ref[...] = f_re[:, None, :] * b_re - f_im[:, None, :] * b_im
    bbim_ref[...] = f_re[:, None, :] * b_im + f_im[:, None, :] * b_re


def _zoh_call(a_re, a_im, log_dt, b_re, b_im):
    rows = DEPTH * 2 * N_SSM_GROUPS
    are = a_re.reshape(rows, STATE_P)
    aim = a_im.reshape(rows, STATE_P)
    ldt = log_dt.reshape(rows, 1)
    bre = jnp.swapaxes(b_re, -1, -2).reshape(rows, SSM_GROUP, STATE_P)
    bim = jnp.swapaxes(b_im, -1, -2).reshape(rows, SSM_GROUP, STATE_P)
    full = lambda a: pl.BlockSpec(a.shape, lambda: (0,) * a.ndim)
    outs = [jax.ShapeDtypeStruct((rows, STATE_P), F32)] * 2 + [jax.ShapeDtypeStruct((rows, SSM_GROUP, STATE_P), F32)] * 2
    return pl.pallas_call(
        _zoh_kernel,
        in_specs=[full(are), full(aim), full(ldt), full(bre), full(bim)],
        out_specs=[full(o) for o in outs],
        out_shape=outs,
        name="s5_zoh",
    )(are, aim, ldt, bre, bim)


def _s5_weights(lb_re, lb_im, bb_re, bb_im, c_re, c_im):
    lead = (DEPTH, 2, SSM_BUNDLES)
    lam_re = lb_re.reshape(DEPTH, 2, 1, N_STATES)
    lam_im = lb_im.reshape(DEPTH, 2, 1, N_STATES)
    rows_in = np.arange(LANES)[:, None] // SSM_GROUP
    cols_in = np.arange(BUNDLE_STATES)[None, :] // STATE_P
    mask_in = jnp.asarray(rows_in == cols_in, F32)
    mask_out = jnp.asarray((rows_in == cols_in).T, F32)

    def in_blocks(bb):
        b5 = bb.reshape(*lead, LANES, STATE_P)
        return jnp.concatenate([b5] * SSM_BUNDLE_GROUPS, axis=-1) * mask_in

    def out_blocks(c):
        ct = jnp.swapaxes(c, -1, -2).reshape(*lead, BUNDLE_STATES, SSM_GROUP)
        return jnp.concatenate([ct] * SSM_BUNDLE_GROUPS, axis=-1) * mask_out

    bw = jnp.concatenate([in_blocks(bb_re), in_blocks(bb_im)], axis=-1).astype(BF16)
    cw = jnp.concatenate([out_blocks(c_re), -out_blocks(c_im)], axis=-2).astype(BF16)
    return lam_re, lam_im, bw, cw


def _scan_chunk(u_ref, bw_ref, cw_ref, lre_ref, lim_ref, h0_ref, hs, hc, *, reverse):
    batch, steps, _ = u_ref.shape

    @pl.when(pl.program_id(0) == 0)
    def _():
        hc[...] = h0_ref[...]

    u = jnp.swapaxes(u_ref[...], 0, 1).reshape(steps * batch, SSM_W)
    for k in range(SSM_BUNDLES):
        r = _dot(u[:, k * LANES:(k + 1) * LANES], bw_ref[k])
        hs[:, k * BUNDLE_STATES:(k + 1) * BUNDLE_STATES] = r[:, 0:BUNDLE_STATES]
        hs[:, N_STATES + k * BUNDLE_STATES:N_STATES + (k + 1) * BUNDLE_STATES] = r[:, BUNDLE_STATES:]

    for blk in range(N_STATES // SCAN_LANES):
        re = slice(blk * SCAN_LANES, (blk + 1) * SCAN_LANES)
        im = slice(N_STATES + blk * SCAN_LANES, N_STATES + (blk + 1) * SCAN_LANES)
        lr = jnp.broadcast_to(lre_ref[:, re], (batch, SCAN_LANES))
        li = jnp.broadcast_to(lim_ref[:, re], (batch, SCAN_LANES))

        def step(t, carry, re=re, im=im, lr=lr, li=li):
            hr, hi = carry
            tt = steps - 1 - t if reverse else t
            row = pl.multiple_of(tt * batch, batch)
            nr = lr * hr - li * hi + hs[pl.ds(row, batch), re]
            ni = lr * hi + li * hr + hs[pl.ds(row, batch), im]
            hs[pl.ds(row, batch), re] = nr
            hs[pl.ds(row, batch), im] = ni
            return nr, ni

        hr, hi = lax.fori_loop(0, steps, step, (hc[:, re], hc[:, im]), unroll=8)
        hc[:, re] = hr
        hc[:, im] = hi

    ys = []
    for k in range(SSM_BUNDLES):
        h_re = hs[:, k * BUNDLE_STATES:(k + 1) * BUNDLE_STATES].astype(BF16)
        h_im = hs[:, N_STATES + k * BUNDLE_STATES:N_STATES + (k + 1) * BUNDLE_STATES].astype(BF16)
        ys.append(_dot(h_re, cw_ref[k, 0:BUNDLE_STATES, :]) + _dot(h_im, cw_ref[k, BUNDLE_STATES:, :]))
    return u, ys


def _s5_fwd_kernel(u_ref, bw_ref, cw_ref, lre_ref, lim_ref, h0_ref, y_ref, hfin_ref, hs, hc):
    _, ys = _scan_chunk(u_ref, bw_ref, cw_ref, lre_ref, lim_ref, h0_ref, hs, hc, reverse=False)
    for k, y in enumerate(ys):
        y_ref[:, k * LANES:(k + 1) * LANES] = y.astype(y_ref.dtype)
    hfin_ref[...] = hc[...]


def _gelu_tanh(x):
    return 0.5 * x * (1.0 + jnp.tanh(math.sqrt(2.0 / math.pi) * (x + 0.044715 * (x * x * x))))


def _s5_bwd_kernel(u_ref, yf_ref, bw_ref, cw_ref, lre_ref, lim_ref, h0_ref, d_ref, wg_ref, bg_ref,
                   o_ref, hfin_ref, hs, hc):
    batch, steps, _ = u_ref.shape
    u, ys = _scan_chunk(u_ref, bw_ref, cw_ref, lre_ref, lim_ref, h0_ref, hs, hc, reverse=True)
    y = d_ref[...] * u.astype(F32) + yf_ref[...].astype(F32) + jnp.concatenate(ys, axis=1)
    y = _gelu_tanh(y)
    gate = _sigmoid(_dot(y.astype(BF16), wg_ref[...]) + bg_ref[...])
    out = jnp.swapaxes((y * gate).reshape(steps, batch, SSM_W), 0, 1)
    o_ref[...] = out.astype(o_ref.dtype)
    hfin_ref[...] = hc[...]


def _s5_calls(s_in, batch, seq_len, layer, lam_re, lam_im, bw, cw, h0, d_skip, w_glu, b_glu):
    rows = s_in.shape[0]
    steps = SCAN_ROWS // batch
    n_chunks = seq_len // steps
    u3 = s_in.reshape(batch, seq_len, SSM_W)
    pick = lambda i, rev: (n_chunks - 1 - i) if rev else i
    seq_chunk = lambda rev: pl.BlockSpec((batch, steps, SSM_W), lambda i: (0, pick(i, rev), 0))
    tm_chunk = lambda rev: pl.BlockSpec((SCAN_ROWS, SSM_W), lambda i: (pick(i, rev), 0))
    dir_spec = lambda a, d: pl.BlockSpec((None, None) + a.shape[2:], lambda i: (layer, d) + (0,) * (a.ndim - 2))
    h0_spec = lambda d: pl.BlockSpec((None,) + h0.shape[1:], lambda i: (d, 0, 0))
    state_spec = pl.BlockSpec((batch, 2 * N_STATES), lambda i: (0, 0))
    state = jax.ShapeDtypeStruct((batch, 2 * N_STATES), F32)
    scratch = [pltpu.VMEM((SCAN_ROWS, 2 * N_STATES), F32), pltpu.VMEM((batch, 2 * N_STATES), F32)]

    yf, hfin_f = pl.pallas_call(
        _s5_fwd_kernel,
        grid=(n_chunks,),
        in_specs=[seq_chunk(False), dir_spec(bw, 0), dir_spec(cw, 0), dir_spec(lam_re, 0), dir_spec(lam_im, 0),
                  h0_spec(0)],
        out_specs=[tm_chunk(False), state_spec],
        out_shape=[jax.ShapeDtypeStruct((rows, SSM_W), BF16), state],
        scratch_shapes=scratch,
        compiler_params=_params("arbitrary"),
        name="s5_fwd",
    )(u3, bw, cw, lam_re, lam_im, h0)

    out, hfin_b = pl.pallas_call(
        _s5_bwd_kernel,
        grid=(n_chunks,),
        in_specs=[seq_chunk(True), tm_chunk(True), dir_spec(bw, 1), dir_spec(cw, 1), dir_spec(lam_re, 1),
                  dir_spec(lam_im, 1), h0_spec(1), _layer_spec(d_skip, layer, 1), _layer_spec(w_glu, layer, 1),
                  _layer_spec(b_glu, layer, 1)],
        out_specs=[seq_chunk(True), state_spec],
        out_shape=[jax.ShapeDtypeStruct((batch, seq_len, SSM_W), BF16), state],
        scratch_shapes=scratch,
        compiler_params=_params("arbitrary"),
        name="s5_bwd",
    )(u3, yf, bw, cw, lam_re, lam_im, h0, d_skip, w_glu, b_glu)
    return out.reshape(rows, SSM_W), hfin_f, hfin_b


def _layer_norm_rows(y, g, b):
    mu = jnp.mean(y, axis=-1, keepdims=True)
    yc = y - mu
    var = jnp.mean(yc * yc, axis=-1, keepdims=True)
    return yc * lax.rsqrt(var + EPS) * g + b


def _merge_kernel(x_ref, a_ref, s_ref, f_ref, sc_ref, sh_ref, g1_ref, wg_ref, wa_ref, ws_ref, wf_ref, wo_ref,
                  lg_ref, lb_ref, o_ref):
    x = x_ref[...]
    u = (x * (1.0 + sc_ref[...]) + sh_ref[...]).astype(BF16)
    merged = (_sigmoid(_dot(u, wg_ref[:, 0:D_MODEL])) * _dot(a_ref[...], wa_ref[...])
              + _sigmoid(_dot(u, wg_ref[:, D_MODEL:2 * D_MODEL])) * _dot(s_ref[...], ws_ref[...])
              + _sigmoid(_dot(u, wg_ref[:, 2 * D_MODEL:3 * D_MODEL])) * _dot(f_ref[...], wf_ref[...]))
    mix = _dot(merged.astype(BF16), wo_ref[...])
    y = ALPHA * x + g1_ref[...] * mix
    o_ref[...] = _layer_norm_rows(y, lg_ref[...], lb_ref[...])


def _merge_call(x, attn, ssm, four, mod3, w_gate, w_ba, w_bs, w_bf, w_out, ln_g, ln_b, layer, seq_len):
    rows = x.shape[0]
    tps = seq_len // ROW_TILE
    per_seq = mod3.shape[0] > 1
    row = lambda w: pl.BlockSpec((ROW_TILE, w), lambda i: (i, 0))
    lay = lambda a: _layer_spec(a, layer, 1)
    return pl.pallas_call(
        _merge_kernel,
        grid=(rows // ROW_TILE,),
        in_specs=[row(D_MODEL), row(ATTN_W), row(SSM_W), row(FOURIER_W),
                  _mod_spec(1, tps, per_seq), _mod_spec(0, tps, per_seq), _mod_spec(2, tps, per_seq),
                  lay(w_gate), lay(w_ba), lay(w_bs), lay(w_bf), lay(w_out), lay(ln_g), lay(ln_b)],
        out_specs=row(D_MODEL),
        out_shape=jax.ShapeDtypeStruct((rows, D_MODEL), F32),
        compiler_params=_params("parallel"),
        name="merge",
    )(x, attn, ssm, four, mod3, mod3, mod3, w_gate, w_ba, w_bs, w_bf, w_out, ln_g, ln_b)


def _shift_rows(h, seq_len):
    rows = h.shape[0]
    prev = pltpu.roll(h, 1, axis=0)
    nxt = pltpu.roll(h, rows - 1, axis=0)
    sub = lax.broadcasted_iota(jnp.int32, (SUBLANES, h.shape[1]), 0)
    p_parts, n_parts = [], []
    for s0 in range(0, rows, seq_len):
        e = s0 + seq_len
        p_parts += [jnp.where(sub == 0, 0.0, prev[s0:s0 + SUBLANES]), prev[s0 + SUBLANES:e]]
        n_parts += [nxt[s0:e - SUBLANES], jnp.where(sub == SUBLANES - 1, 0.0, nxt[e - SUBLANES:e])]
    return jnp.concatenate(p_parts, axis=0), jnp.concatenate(n_parts, axis=0)


def _ffn_kernel(x_ref, sc_ref, sh_ref, g2_ref, wa_ref, wb_ref, cwa_ref, cwb_ref, cba_ref, cbb_ref, wd_ref,
                lg_ref, lb_ref, o_ref, u2, acc, *, seq_len):
    c = pl.program_id(1)
    rb = x_ref.shape[0] // FFN_ROW_BLOCKS
    blocks = [slice(i * rb, (i + 1) * rb) for i in range(FFN_ROW_BLOCKS)]

    @pl.when(c == 0)
    def _():
        u2[...] = (x_ref[...] * (1.0 + sc_ref[...]) + sh_ref[...]).astype(BF16)
        acc[...] = jnp.zeros_like(acc)

    wa = wa_ref[...].astype(BF16)
    wb = wb_ref[...].astype(BF16)
    wd = wd_ref[...].astype(BF16)
    ha = jnp.concatenate([_dot(u2[blk, :], wa) for blk in blocks], axis=0)
    hb = jnp.concatenate([_dot(u2[blk, :], wb) for blk in blocks], axis=0)

    def conv(h, cw_ref, cb_ref):
        prev, nxt = _shift_rows(h, seq_len)
        return prev * cw_ref[0:1, :] + h * cw_ref[1:2, :] + nxt * cw_ref[2:3, :] + cb_ref[...]

    a = conv(ha, cwa_ref, cba_ref)
    b = conv(hb, cwb_ref, cbb_ref)
    gated = ((a * _sigmoid(a)) * b).astype(BF16)
    for blk in blocks:
        acc[blk, :] += _dot(gated[blk, :], wd)

    @pl.when(c == pl.num_programs(1) - 1)
    def _():
        y = ALPHA * x_ref[...] + g2_ref[...] * acc[...]
        o_ref[...] = _layer_norm_rows(y, lg_ref[...], lb_ref[...])


def _ffn_call(x, mod3, w_up, conv_w, conv_b, w_down, ln_g, ln_b, layer, seq_len):
    rows = x.shape[0]
    tile = FFN_ROWS
    n_chunks = D_FF // FFN_CHUNK
    seqs_per_tile = tile // seq_len
    per_seq = mod3.shape[0] > 1
    if per_seq:
        mod = lambda ch: pl.BlockSpec((None, 1, D_MODEL), lambda i, c: (i * seqs_per_tile, 0, ch))
    else:
        mod = lambda ch: pl.BlockSpec((None, 1, D_MODEL), lambda i, c: (0, 0, ch))
    rowspec = pl.BlockSpec((tile, D_MODEL), lambda i, c: (i, 0))
    cols = lambda r, off: pl.BlockSpec((None, r, FFN_CHUNK), lambda i, c: (layer, 0, off + c))
    return pl.pallas_call(
        functools.partial(_ffn_kernel, seq_len=seq_len),
        grid=(rows // tile, n_chunks),
        in_specs=[rowspec, mod(4), mod(3), mod(5),
                  cols(D_MODEL, 0), cols(D_MODEL, n_chunks), cols(3, 0), cols(3, n_chunks), cols(1, 0), cols(1, n_chunks),
                  pl.BlockSpec((None, FFN_CHUNK, D_MODEL), lambda i, c: (layer, c, 0)),
                  _layer_spec(ln_g, layer, 2), _layer_spec(ln_b, layer, 2)],
        out_specs=rowspec,
        out_shape=jax.ShapeDtypeStruct((rows, D_MODEL), F32),
        scratch_shapes=[pltpu.VMEM((tile, D_MODEL), BF16), pltpu.VMEM((tile, D_MODEL), F32)],
        compiler_params=_params("parallel", "arbitrary"),
        name="conv_mlp",
    )(x, mod3, mod3, mod3, w_up, w_up, conv_w, conv_w, conv_b, conv_b, w_down, ln_g, ln_b)


def _rope_tables(n_tokens):
    rows = n_tokens // GRID_W
    row = np.repeat(np.arange(rows, dtype=np.float64), GRID_W)
    col = np.tile(np.arange(GRID_W, dtype=np.float64), rows)
    freqs = ROPE_THETA ** (-np.arange(ROPE_AXIS_PAIRS, dtype=np.float64) / ROPE_AXIS_PAIRS)
    ang = np.concatenate([row[:, None] * freqs, col[:, None] * freqs], axis=-1)
    cos = np.repeat(np.cos(ang), 2, axis=-1)
    sin = np.repeat(np.sin(ang), 2, axis=-1) * np.tile(np.array([-1.0, 1.0]), HEAD_DIM // 2)
    return (jnp.asarray(np.tile(cos, (1, N_Q_HEADS)), F32), jnp.asarray(np.tile(sin, (1, N_Q_HEADS)), F32))


def _trunk_layer(x, mod3, w, layer, batch, seq_len, rope_tabs, cache_k, cache_v, h0, kv_dtype):
    q, k, v, s_in, f_in = _inproj_call(x, mod3, w["w_mix"], w["gq"], w["gk"], w["ones"], rope_tabs, layer,
                                       seq_len, kv_dtype)
    attn = _attn_call(q, k, v, cache_k, cache_v, layer, batch, seq_len)
    four = _fourier_call(f_in, batch, seq_len)
    ssm, hfin_f, hfin_b = _s5_calls(s_in, batch, seq_len, layer, w["lam_re"], w["lam_im"], w["bw"], w["cw"], h0,
                                    w["d"], w["w_glu"], w["b_glu"])
    x1 = _merge_call(x, attn, ssm, four, mod3, w["w_gate"], w["w_ba"], w["w_bs"], w["w_bf"], w["w_out"],
                     w["ln1_g"], w["ln1_b"], layer, seq_len)
    x2 = _ffn_call(x1, mod3, w["w_up"], w["conv_w"], w["conv_b"], w["w_down"], w["ln2_g"], w["ln2_b"], layer, seq_len)
    return x2, k, v, hfin_f, hfin_b


def _split_state(h):
    b = h.shape[0]
    re = h[..., 0].reshape(b, 2, N_STATES)
    im = h[..., 1].reshape(b, 2, N_STATES)
    return jnp.concatenate([re, im], axis=-1).transpose(1, 0, 2)


def _join_state(hf, hb):
    def one(h):
        b = h.shape[0]
        re = h[:, 0:N_STATES].reshape(b, N_SSM_GROUPS, STATE_P)
        im = h[:, N_STATES:].reshape(b, N_SSM_GROUPS, STATE_P)
        return jnp.stack([re, im], axis=-1)
    return jnp.stack([one(hf), one(hb)], axis=1)


def kernel(x_prompt, x_sample, c, cache_k, cache_v, state_ssm, c_ctx, w_ada, b_ada, w_in, q_norm_g, k_norm_g, ssm_a_re, ssm_a_im, ssm_log_dt, ssm_b_re, ssm_b_im, ssm_c_re, ssm_c_im, ssm_d, w_glu, b_glu, w_br_attn, w_br_ssm, w_br_four, w_out, ln1_g, ln1_b, w_up, conv_w, conv_b, w_down, ln2_g, ln2_b):
    bc, lc, _ = x_prompt.shape
    bd, ld, _ = x_sample.shape
    past = cache_k.shape[2]

    cvec = jnp.zeros((MOD_ROWS, D_MODEL), F32).at[0:bd].set(c).at[bd].set(c_ctx)
    mod = _mod_call(cvec, w_ada, b_ada)

    lb_re, lb_im, bb_re, bb_im = _zoh_call(ssm_a_re, ssm_a_im, ssm_log_dt, ssm_b_re, ssm_b_im)
    lam_re, lam_im, bw, cw = _s5_weights(lb_re, lb_im, bb_re, bb_im, ssm_c_re, ssm_c_im)

    head_mean = np.kron(np.eye(N_Q_HEADS), np.full((HEAD_DIM, HEAD_DIM), 1.0 / HEAD_DIM))
    rope_tabs = _rope_tables(ld)
    vec = lambda a: a.reshape(DEPTH, 1, -1)
    w = dict(
        w_mix=w_in[:, :, 0:OFF_G].astype(BF16), w_gate=w_in[:, :, OFF_G:D_IN].astype(BF16),
        gq=vec(jnp.tile(q_norm_g, (1, N_Q_HEADS))), gk=vec(jnp.tile(k_norm_g, (1, N_KV_HEADS))),
        ones=jnp.asarray(head_mean, F32).astype(BF16), lam_re=lam_re, lam_im=lam_im, bw=bw, cw=cw, d=vec(ssm_d),
        w_glu=w_glu.astype(BF16), b_glu=vec(b_glu), w_ba=w_br_attn.astype(BF16), w_bs=w_br_ssm.astype(BF16),
        w_bf=w_br_four.astype(BF16), w_out=w_out.astype(BF16), ln1_g=vec(ln1_g), ln1_b=vec(ln1_b),
        w_up=w_up, conv_w=conv_w, conv_b=vec(conv_b), w_down=w_down, ln2_g=vec(ln2_g), ln2_b=vec(ln2_b))

    h = x_prompt.reshape(bc * lc, D_MODEL)
    h0_ctx = jnp.zeros((2, bc, 2 * N_STATES), F32)
    ks, vs, ss = [], [], []
    for l in range(DEPTH):
        mod_ctx = mod[l, bd:bd + 1].reshape(1, 1, 6 * D_MODEL)
        h, k_l, v_l, hf, hb = _trunk_layer(h, mod_ctx, w, l, bc, lc, None, None, None, h0_ctx, F32)
        ks.append(k_l.reshape(bc, lc, N_KV_HEADS, HEAD_DIM))
        vs.append(v_l.reshape(bc, lc, N_KV_HEADS, HEAD_DIM))
        ss.append(_join_state(hf, hb))
    new_cache_k = jnp.stack(ks, axis=1)
    new_cache_v = jnp.stack(vs, axis=1)
    new_state = jnp.stack(ss, axis=1)

    zx = x_sample.reshape(bd * ld, D_MODEL)
    ck = cache_k.reshape(bd, DEPTH, past, KV_W).astype(BF16)
    cv = cache_v.reshape(bd, DEPTH, past, KV_W).astype(BF16)
    for l in range(DEPTH):
        mod_dec = mod[l, 0:bd].reshape(bd, 1, 6 * D_MODEL)
        zx, _, _, _, _ = _trunk_layer(zx, mod_dec, w, l, bd, ld, rope_tabs, ck, cv,
                                      _split_state(state_ssm[:, l]), BF16)

    return (h.reshape(bc, lc, D_MODEL), zx.reshape(bd, ld, D_MODEL), new_cache_k, new_cache_v, new_state)
```

```python
import functools
import math

import jax
import jax.numpy as jnp
import numpy as np
from jax import lax
from jax.experimental import pallas as pl
from jax.experimental.pallas import tpu as pltpu

F32 = jnp.float32
BF16 = jnp.bfloat16

D_MODEL = 1024
DEPTH = 2
GRID_W = 64
HEAD_DIM = 64
N_Q_HEADS = 8
N_KV_HEADS = 2
GQA_GROUP = N_Q_HEADS // N_KV_HEADS
ATTN_W = N_Q_HEADS * HEAD_DIM
KV_W = N_KV_HEADS * HEAD_DIM
ROPE_AXIS_PAIRS = HEAD_DIM // 4
ROPE_THETA = 10000.0
SSM_W = D_MODEL // 2
SSM_GROUP = 16
N_SSM_GROUPS = SSM_W // SSM_GROUP
STATE_P = 64
N_STATES = N_SSM_GROUPS * STATE_P
FOURIER_W = D_MODEL // 2
FOURIER_GROUPS = 4
FOURIER_GW = FOURIER_W // FOURIER_GROUPS
N_BRANCH = 3
GATE_W = N_BRANCH * D_MODEL
D_IN = ATTN_W + 2 * KV_W + SSM_W + FOURIER_W + GATE_W
OFF_K = ATTN_W
OFF_V = ATTN_W + KV_W
OFF_S = ATTN_W + 2 * KV_W
OFF_F = OFF_S + SSM_W
OFF_G = OFF_F + FOURIER_W
D_FF = ((8 * D_MODEL // 3 + 127) // 128) * 128
EPS = 1e-6
ALPHA = (2 * DEPTH) ** 0.25

V7X_VMEM_BYTES = 64 * 1024 * 1024
VMEM_LIMIT_BYTES = V7X_VMEM_BYTES - 8 * 1024 * 1024
LANES = 128
SUBLANES = 8
SSM_BUNDLE_GROUPS = LANES // SSM_GROUP
SSM_BUNDLES = N_SSM_GROUPS // SSM_BUNDLE_GROUPS
BUNDLE_STATES = SSM_BUNDLE_GROUPS * STATE_P

ROW_TILE = 256
ATTN_Q_ROWS = 512
FFN_ROWS = 1024
FFN_CHUNK = 256
FFN_ROW_BLOCKS = 4
SCAN_ROWS = 512
SCAN_LANES = 512
MOD_COLS = 1536
MOD_ROWS = 16


def _params(*sem):
    return pltpu.CompilerParams(dimension_semantics=sem, vmem_limit_bytes=VMEM_LIMIT_BYTES)


def _dot(a, b):
    return jnp.dot(a, b, preferred_element_type=F32)


def _sigmoid(x):
    return 1.0 / (1.0 + jnp.exp(-x))


def _layer_spec(arr, layer, n_grid):
    zeros = (0,) * (arr.ndim - 1)
    if n_grid == 1:
        return pl.BlockSpec((None,) + arr.shape[1:], lambda i: (layer,) + zeros)
    return pl.BlockSpec((None,) + arr.shape[1:], lambda i, j: (layer,) + zeros)


def _mod_spec(chunk, tiles_per_seq, per_seq):
    if per_seq:
        return pl.BlockSpec((None, 1, D_MODEL), lambda i: (i // tiles_per_seq, 0, chunk))
    return pl.BlockSpec((None, 1, D_MODEL), lambda i: (0, 0, chunk))


def _mod_kernel(c_ref, w_ref, b_ref, o_ref):
    c = c_ref[...]
    s = (c * _sigmoid(c)).astype(BF16)
    o_ref[...] = _dot(s, w_ref[...].astype(BF16)) + b_ref[...]


def _mod_call(cvec, w_ada, b_ada):
    n = 6 * D_MODEL
    return pl.pallas_call(
        _mod_kernel,
        grid=(DEPTH, n // MOD_COLS),
        in_specs=[
            pl.BlockSpec((MOD_ROWS, D_MODEL), lambda l, j: (0, 0)),
            pl.BlockSpec((None, D_MODEL, MOD_COLS), lambda l, j: (l, 0, j)),
            pl.BlockSpec((None, 1, MOD_COLS), lambda l, j: (l, 0, j)),
        ],
        out_specs=pl.BlockSpec((None, MOD_ROWS, MOD_COLS), lambda l, j: (l, 0, j)),
        out_shape=jax.ShapeDtypeStruct((DEPTH, MOD_ROWS, n), F32),
        compiler_params=_params("parallel", "parallel"),
        name="mod",
    )(cvec, w_ada, b_ada.reshape(DEPTH, 1, n))


def _swap_pairs(x):
    w = x.shape[-1]
    nxt = pltpu.roll(x, w - 1, axis=1)
    prv = pltpu.roll(x, 1, axis=1)
    lane = lax.broadcasted_iota(jnp.int32, x.shape, 1)
    return jnp.where((lane & 1) == 0, nxt, prv)


def _weight_slab_spec(layer, col0, width):
    return pl.BlockSpec((pl.Element(1), pl.Element(D_MODEL), pl.Element(width)), lambda i: (layer, 0, col0),
                        pipeline_mode=pl.Buffered(1))


def _inproj_kernel(*refs, rope):
    if rope:
        (x_ref, sc_ref, sh_ref, w_ref, gq_ref, gk_ref, ones_ref, cos_ref, sin_ref,
         q_ref, k_ref, v_ref, s_ref, f_ref, wbf) = refs
    else:
        (x_ref, sc_ref, sh_ref, w_ref, gq_ref, gk_ref, ones_ref,
         q_ref, k_ref, v_ref, s_ref, f_ref, wbf) = refs
    batch, steps, _ = x_ref.shape
    rows = batch * steps

    @pl.when(pl.program_id(0) == 0)
    def _():
        wbf[...] = w_ref[0].astype(BF16)

    u = (x_ref[...] * (1.0 + sc_ref[...]) + sh_ref[...]).reshape(rows, D_MODEL).astype(BF16)

    zq = _dot(u, wbf[:, 0:OFF_K])
    zk = _dot(u, wbf[:, OFF_K:OFF_V])
    zv = _dot(u, wbf[:, OFF_V:OFF_S])
    msq_q = _dot((zq * zq).astype(BF16), ones_ref[...])
    msq_k = _dot((zk * zk).astype(BF16), ones_ref[0:KV_W, 0:KV_W])
    q = zq * lax.rsqrt(msq_q + EPS) * gq_ref[...]
    k = zk * lax.rsqrt(msq_k + EPS) * gk_ref[...]
    if rope:
        cos = jnp.concatenate([cos_ref[...]] * batch, axis=0)
        sin = jnp.concatenate([sin_ref[...]] * batch, axis=0)
        q = q * cos + _swap_pairs(q) * sin
        k = k * cos[:, 0:KV_W] + _swap_pairs(k) * sin[:, 0:KV_W]
    q_ref[...] = (q * (HEAD_DIM ** -0.5)).reshape(batch, steps, ATTN_W).astype(q_ref.dtype)
    k_ref[...] = k.reshape(batch, steps, KV_W).astype(k_ref.dtype)
    v_ref[...] = zv.reshape(batch, steps, KV_W).astype(v_ref.dtype)
    f_ref[...] = _dot(u, wbf[:, OFF_F:OFF_G]).reshape(batch, steps, FOURIER_W).astype(f_ref.dtype)
    s = _dot(u, wbf[:, OFF_S:OFF_F]).reshape(batch, steps, SSM_W)
    s_ref[...] = jnp.swapaxes(s, 0, 1).reshape(rows, SSM_W).astype(s_ref.dtype)


def _inproj_call(x, mod3, w_in, gq, gk, ones_blk, rope_tabs, layer, batch, seq_len, kv_dtype):
    steps = ROW_TILE // batch
    rope = rope_tabs is not None
    x3 = x.reshape(batch, seq_len, D_MODEL)
    seq = lambda w: pl.BlockSpec((batch, steps, w), lambda i: (0, i, 0))
    mod = lambda ch: pl.BlockSpec((mod3.shape[0], 1, D_MODEL), lambda i: (0, 0, ch))
    in_specs = [seq(D_MODEL), mod(1), mod(0), _weight_slab_spec(layer, 0, OFF_G),
                _layer_spec(gq, layer, 1), _layer_spec(gk, layer, 1), pl.BlockSpec(ones_blk.shape, lambda i: (0, 0))]
    args = [x3, mod3, mod3, w_in, gq, gk, ones_blk]
    if rope:
        tab = pl.BlockSpec((steps, ATTN_W), lambda i: (i, 0))
        in_specs += [tab, tab]
        args += list(rope_tabs)
    widths = (ATTN_W, KV_W, KV_W, FOURIER_W)
    dtypes = (BF16, kv_dtype, kv_dtype, BF16)
    outs = pl.pallas_call(
        functools.partial(_inproj_kernel, rope=rope),
        grid=(seq_len // steps,),
        in_specs=in_specs,
        out_specs=[seq(w) for w in widths[:3]] + [pl.BlockSpec((ROW_TILE, SSM_W), lambda i: (i, 0)), seq(FOURIER_W)],
        out_shape=([jax.ShapeDtypeStruct((batch, seq_len, w), d) for w, d in zip(widths[:3], dtypes[:3])]
                   + [jax.ShapeDtypeStruct((seq_len * batch, SSM_W), BF16),
                      jax.ShapeDtypeStruct((batch, seq_len, FOURIER_W), BF16)]),
        scratch_shapes=[pltpu.VMEM((D_MODEL, OFF_G), BF16)],
        compiler_params=_params("arbitrary"),
        name="in_proj",
    )(*args)
    q, k, v, s_tm, f = outs
    flat = lambda a: a.reshape(batch * seq_len, a.shape[-1])
    return flat(q), flat(k), flat(v), s_tm, flat(f)


def _attn_kernel(*refs, has_cache):
    if has_cache:
        q_ref, k_ref, v_ref, ck_ref, cv_ref, o_ref = refs
    else:
        q_ref, k_ref, v_ref, o_ref = refs
    q = q_ref[...].astype(F32)
    contract_last = (((1,), (1,)), ((), ()))
    outs = []
    for h in range(N_Q_HEADS):
        j = h // GQA_GROUP
        lo, hi = j * HEAD_DIM, (j + 1) * HEAD_DIM
        qh = q[:, h * HEAD_DIM:(h + 1) * HEAD_DIM].astype(BF16)
        s1 = lax.dot_general(qh, k_ref[:, lo:hi].astype(BF16), contract_last, preferred_element_type=F32)
        m = jnp.max(s1, axis=-1, keepdims=True)
        if has_cache:
            s2 = lax.dot_general(qh, ck_ref[:, lo:hi], contract_last, preferred_element_type=F32)
            m = jnp.maximum(m, jnp.max(s2, axis=-1, keepdims=True))
        p1 = jnp.exp(s1 - m)
        den = jnp.sum(p1, axis=-1, keepdims=True)
        o = _dot(p1.astype(BF16), v_ref[:, lo:hi].astype(BF16))
        if has_cache:
            p2 = jnp.exp(s2 - m)
            den = den + jnp.sum(p2, axis=-1, keepdims=True)
            o = o + _dot(p2.astype(BF16), cv_ref[:, lo:hi])
        outs.append(o * (1.0 / den))
    o_ref[...] = jnp.concatenate(outs, axis=1).astype(o_ref.dtype)


def _attn_call(q, k, v, cache_k, cache_v, layer, batch, seq_len):
    rows = q.shape[0]
    qb = min(ATTN_Q_ROWS, seq_len)
    nq = seq_len // qb
    has_cache = cache_k is not None
    in_specs = [pl.BlockSpec((qb, ATTN_W), lambda b, i: (b * nq + i, 0)),
                pl.BlockSpec((seq_len, KV_W), lambda b, i: (b, 0)),
                pl.BlockSpec((seq_len, KV_W), lambda b, i: (b, 0))]
    args = [q, k, v]
    if has_cache:
        past = cache_k.shape[2]
        in_specs += [pl.BlockSpec((None, None, past, KV_W), lambda b, i: (b, layer, 0, 0))] * 2
        args += [cache_k, cache_v]
    return pl.pallas_call(
        functools.partial(_attn_kernel, has_cache=has_cache),
        grid=(batch, nq),
        in_specs=in_specs,
        out_specs=pl.BlockSpec((qb, ATTN_W), lambda b, i: (b * nq + i, 0)),
        out_shape=jax.ShapeDtypeStruct((rows, ATTN_W), BF16),
        compiler_params=_params("parallel", "parallel"),
        name="attention",
    )(*args)


def _fourier_kernel(f_ref, wc_ref, wl_ref, o_ref):
    f = f_ref[...]
    parts = [_dot(f[:, g * FOURIER_GW:(g + 1) * FOURIER_GW], wc_ref[...]) for g in range(FOURIER_GROUPS)]
    xc = jnp.concatenate([p[:, 0:FOURIER_GW] for p in parts], axis=1)
    xs = jnp.concatenate([p[:, FOURIER_GW:2 * FOURIER_GW] for p in parts], axis=1)
    stacked = jnp.concatenate([xc, xs], axis=0).astype(BF16)
    o_ref[...] = _dot(wl_ref[...], stacked).astype(o_ref.dtype)


def _dft_tables(seq_len):
    c = np.arange(FOURIER_GW)
    ang_c = 2.0 * np.pi * np.outer(c, c) / FOURIER_GW
    wc = np.concatenate([np.cos(ang_c), np.sin(ang_c)], axis=1) / math.sqrt(FOURIER_GW)
    t = np.arange(seq_len)
    ang_l = 2.0 * np.pi * np.outer(t, t) / seq_len
    wl = np.concatenate([np.cos(ang_l), -np.sin(ang_l)], axis=1) / math.sqrt(seq_len)
    return (jnp.asarray(wc, F32).astype(BF16), jnp.asarray(wl, F32).astype(BF16))


def _fourier_call(f, batch, seq_len):
    wc, wl = _dft_tables(seq_len)
    return pl.pallas_call(
        _fourier_kernel,
        grid=(batch,),
        in_specs=[pl.BlockSpec((seq_len, FOURIER_W), lambda b: (b, 0)),
                  pl.BlockSpec(wc.shape, lambda b: (0, 0)),
                  pl.BlockSpec(wl.shape, lambda b: (0, 0))],
        out_specs=pl.BlockSpec((seq_len, FOURIER_W), lambda b: (b, 0)),
        out_shape=jax.ShapeDtypeStruct(f.shape, BF16),
        compiler_params=_params("parallel"),
        name="fourier",
    )(f, wc, wl)


def _block_diagonal(compact, group_rows, group_cols):
    rows, w = compact.shape
    log2 = lambda n: n.bit_length() - 1
    sel = (lax.broadcasted_iota(jnp.int32, (w, SSM_BUNDLE_GROUPS * w), 0)
           == (lax.broadcasted_iota(jnp.int32, (w, SSM_BUNDLE_GROUPS * w), 1) & (w - 1)))
    wide = _dot(compact.astype(BF16), jnp.where(sel, 1.0, 0.0).astype(BF16))
    row_group = (lax.broadcasted_iota(jnp.int32, wide.shape, 0) >> log2(group_rows)) & (SSM_BUNDLE_GROUPS - 1)
    col_group = lax.broadcasted_iota(jnp.int32, wide.shape, 1) >> log2(group_cols)
    return jnp.where(row_group == col_group, wide, 0.0)


def _zoh_kernel(are_ref, aim_ref, ldt_ref, bre_ref, bim_ref, cre_ref, cim_ref, lre_ref, lim_ref, bw_ref, cw_ref):
    a_re = are_ref[...]
    a_im = aim_ref[...]
    dt = jnp.exp(ldt_ref[...])
    mag = jnp.exp(a_re * dt)
    ang = a_im * dt
    lb_re = mag * jnp.cos(ang)
    lb_im = mag * jnp.sin(ang)
    den = a_re * a_re + a_im * a_im
    n_re = lb_re - 1.0
    n_im = lb_im
    f_re = (n_re * a_re + n_im * a_im) / den
    f_im = (n_im * a_re - n_re * a_im) / den
    lre_ref[...] = lb_re
    lim_ref[...] = lb_im
    b_re = bre_ref[...]
    b_im = bim_ref[...]
    bb_re = f_re[:, None, :] * b_re - f_im[:, None, :] * b_im
    bb_im = f_re[:, None, :] * b_im + f_im[:, None, :] * b_re
    flat = lambda a: a.reshape(a.shape[0] * a.shape[1], a.shape[2])
    bw_ref[:, 0:BUNDLE_STATES] = _block_diagonal(flat(bb_re), SSM_GROUP, STATE_P).astype(BF16)
    bw_ref[:, BUNDLE_STATES:] = _block_diagonal(flat(bb_im), SSM_GROUP, STATE_P).astype(BF16)
    n_slabs = cw_ref.shape[0]
    c_re = _block_diagonal(flat(cre_ref[...]), STATE_P, SSM_GROUP)
    c_im = _block_diagonal(flat(cim_ref[...]), STATE_P, SSM_GROUP)
    cw_ref[:, 0] = c_re.reshape(n_slabs, BUNDLE_STATES, LANES).astype(BF16)
    cw_ref[:, 1] = (-c_im).reshape(n_slabs, BUNDLE_STATES, LANES).astype(BF16)


def _zoh_call(a_re, a_im, log_dt, b_re, b_im, c_re, c_im):
    rows = DEPTH * 2 * N_SSM_GROUPS
    n_slabs = DEPTH * 2 * SSM_BUNDLES
    are = a_re.reshape(rows, STATE_P)
    aim = a_im.reshape(rows, STATE_P)
    ldt = log_dt.reshape(rows, 1)
    bre = jnp.swapaxes(b_re, -1, -2).reshape(rows, SSM_GROUP, STATE_P)
    bim = jnp.swapaxes(b_im, -1, -2).reshape(rows, SSM_GROUP, STATE_P)
    cre = jnp.swapaxes(c_re, -1, -2).reshape(rows, STATE_P, SSM_GROUP)
    cim = jnp.swapaxes(c_im, -1, -2).reshape(rows, STATE_P, SSM_GROUP)
    full = lambda a: pl.BlockSpec(a.shape, lambda: (0,) * a.ndim)
    outs = [jax.ShapeDtypeStruct((rows, STATE_P), F32)] * 2 + [
        jax.ShapeDtypeStruct((n_slabs * LANES, 2 * BUNDLE_STATES), BF16),
        jax.ShapeDtypeStruct((n_slabs, 2, BUNDLE_STATES, LANES), BF16)]
    lb_re, lb_im, bw, cw = pl.pallas_call(
        _zoh_kernel,
        in_specs=[full(a) for a in (are, aim, ldt, bre, bim, cre, cim)],
        out_specs=[full(o) for o in outs],
        out_shape=outs,
        compiler_params=pltpu.CompilerParams(vmem_limit_bytes=VMEM_LIMIT_BYTES),
        name="s5_zoh",
    )(are, aim, ldt, bre, bim, cre, cim)
    lead = (DEPTH, 2, SSM_BUNDLES)
    return (lb_re.reshape(DEPTH, 2, 1, N_STATES), lb_im.reshape(DEPTH, 2, 1, N_STATES),
            bw.reshape(*lead, LANES, 2 * BUNDLE_STATES), cw.reshape(*lead, 2 * BUNDLE_STATES, LANES))


def _scan_project_in(u_ref, bw_ref, hs):
    u = u_ref[...]
    for k in range(SSM_BUNDLES):
        r = _dot(u[:, k * LANES:(k + 1) * LANES], bw_ref[k])
        hs[:, k * BUNDLE_STATES:(k + 1) * BUNDLE_STATES] = r[:, 0:BUNDLE_STATES]
        hs[:, N_STATES + k * BUNDLE_STATES:N_STATES + (k + 1) * BUNDLE_STATES] = r[:, BUNDLE_STATES:]


def _scan_recur(hs, hc, lre_ref, lim_ref, batch, reverse):
    steps = hs.shape[0] // batch
    for blk in range(N_STATES // SCAN_LANES):
        re = slice(blk * SCAN_LANES, (blk + 1) * SCAN_LANES)
        im = slice(N_STATES + blk * SCAN_LANES, N_STATES + (blk + 1) * SCAN_LANES)
        lr = jnp.broadcast_to(lre_ref[:, re], (batch, SCAN_LANES))
        li = jnp.broadcast_to(lim_ref[:, re], (batch, SCAN_LANES))
        hr, hi = hc[:, re], hc[:, im]
        for t in range(steps):
            tt = steps - 1 - t if reverse else t
            rows = slice(tt * batch, (tt + 1) * batch)
            hr, hi = (lr * hr - li * hi + hs[rows, re], lr * hi + li * hr + hs[rows, im])
            hs[rows, re] = hr
            hs[rows, im] = hi
        hc[:, re] = hr
        hc[:, im] = hi


def _scan_project_out(hs, cw_ref, y_ref):
    for k in range(SSM_BUNDLES):
        h_re = hs[:, k * BUNDLE_STATES:(k + 1) * BUNDLE_STATES].astype(BF16)
        h_im = hs[:, N_STATES + k * BUNDLE_STATES:N_STATES + (k + 1) * BUNDLE_STATES].astype(BF16)
        y = _dot(h_re, cw_ref[k, 0:BUNDLE_STATES, :]) + _dot(h_im, cw_ref[k, BUNDLE_STATES:, :])
        y_ref[:, k * LANES:(k + 1) * LANES] = y.astype(y_ref.dtype)


def _s5_scan_kernel(uf_ref, ub_ref, bwf_ref, cwf_ref, lrf_ref, lif_ref, bwb_ref, cwb_ref, lrb_ref, lib_ref,
                    h0f_ref, h0b_ref, yf_ref, yb_ref, hff_ref, hfb_ref, hsf, hsb, hcf, hcb, *, batch):
    @pl.when(pl.program_id(0) == 0)
    def _():
        hcf[...] = h0f_ref[...]
        hcb[...] = h0b_ref[...]

    _scan_project_in(uf_ref, bwf_ref, hsf)
    _scan_project_in(ub_ref, bwb_ref, hsb)
    _scan_recur(hsf, hcf, lrf_ref, lif_ref, batch, False)
    _scan_recur(hsb, hcb, lrb_ref, lib_ref, batch, True)
    _scan_project_out(hsf, cwf_ref, yf_ref)
    _scan_project_out(hsb, cwb_ref, yb_ref)
    hff_ref[...] = hcf[...]
    hfb_ref[...] = hcb[...]


def _gelu_tanh(x):
    return 0.5 * x * (1.0 + jnp.tanh(math.sqrt(2.0 / math.pi) * (x + 0.044715 * (x * x * x))))


def _s5_post_kernel(u_ref, yf_ref, yb_ref, d_ref, wg_ref, bg_ref, o_ref):
    batch, steps, _ = o_ref.shape
    y = d_ref[...] * u_ref[...].astype(F32) + yf_ref[...].astype(F32) + yb_ref[...].astype(F32)
    y = _gelu_tanh(y)
    gate = _sigmoid(_dot(y.astype(BF16), wg_ref[...]) + bg_ref[...])
    out = jnp.swapaxes((y * gate).reshape(steps, batch, SSM_W), 0, 1)
    o_ref[...] = out.astype(o_ref.dtype)


def _s5_calls(u_tm, batch, seq_len, layer, lam_re, lam_im, bw, cw, h0, d_skip, w_glu, b_glu):
    rows = u_tm.shape[0]
    steps = SCAN_ROWS // batch
    n_chunks = seq_len // steps
    fwd_chunk = pl.BlockSpec((SCAN_ROWS, SSM_W), lambda i: (i, 0))
    bwd_chunk = pl.BlockSpec((SCAN_ROWS, SSM_W), lambda i: (n_chunks - 1 - i, 0))
    dir_spec = lambda a, d: pl.BlockSpec((None, None) + a.shape[2:], lambda i: (layer, d) + (0,) * (a.ndim - 2))
    weights = lambda d: [dir_spec(bw, d), dir_spec(cw, d), dir_spec(lam_re, d), dir_spec(lam_im, d)]
    h0_spec = lambda d: pl.BlockSpec((None,) + h0.shape[1:], lambda i: (d, 0, 0))
    state_spec = pl.BlockSpec((batch, 2 * N_STATES), lambda i: (0, 0))
    state = jax.ShapeDtypeStruct((batch, 2 * N_STATES), F32)
    hs = lambda: pltpu.VMEM((SCAN_ROWS, 2 * N_STATES), F32)
    hc = lambda: pltpu.VMEM((batch, 2 * N_STATES), F32)

    yf, yb, hfin_f, hfin_b = pl.pallas_call(
        functools.partial(_s5_scan_kernel, batch=batch),
        grid=(n_chunks,),
        in_specs=[fwd_chunk, bwd_chunk] + weights(0) + weights(1) + [h0_spec(0), h0_spec(1)],
        out_specs=[fwd_chunk, bwd_chunk, state_spec, state_spec],
        out_shape=[jax.ShapeDtypeStruct((rows, SSM_W), BF16)] * 2 + [state, state],
        scratch_shapes=[hs(), hs(), hc(), hc()],
        compiler_params=_params("arbitrary"),
        name="s5_scan",
    )(u_tm, u_tm, bw, cw, lam_re, lam_im, bw, cw, lam_re, lam_im, h0, h0)

    out = pl.pallas_call(
        _s5_post_kernel,
        grid=(n_chunks,),
        in_specs=[fwd_chunk, fwd_chunk, fwd_chunk, _layer_spec(d_skip, layer, 1), _layer_spec(w_glu, layer, 1),
                  _layer_spec(b_glu, layer, 1)],
        out_specs=pl.BlockSpec((batch, steps, SSM_W), lambda i: (0, i, 0)),
        out_shape=jax.ShapeDtypeStruct((batch, seq_len, SSM_W), BF16),
        compiler_params=_params("parallel"),
        name="s5_post",
    )(u_tm, yf, yb, d_skip, w_glu, b_glu)
    return out.reshape(rows, SSM_W), hfin_f, hfin_b


def _layer_norm_rows(y, g, b):
    mu = jnp.mean(y, axis=-1, keepdims=True)
    yc = y - mu
    var = jnp.mean(yc * yc, axis=-1, keepdims=True)
    return yc * lax.rsqrt(var + EPS) * g + b


def _merge_kernel(x_ref, a_ref, s_ref, f_ref, sc_ref, sh_ref, g1_ref, wg_ref, wa_ref, ws_ref, wf_ref, wo_ref,
                  lg_ref, lb_ref, o_ref, wgb):
    @pl.when(pl.program_id(0) == 0)
    def _():
        wgb[...] = wg_ref[0].astype(BF16)

    x = x_ref[...]
    u = (x * (1.0 + sc_ref[...]) + sh_ref[...]).astype(BF16)
    merged = (_sigmoid(_dot(u, wgb[:, 0:D_MODEL])) * _dot(a_ref[...], wa_ref[...])
              + _sigmoid(_dot(u, wgb[:, D_MODEL:2 * D_MODEL])) * _dot(s_ref[...], ws_ref[...])
              + _sigmoid(_dot(u, wgb[:, 2 * D_MODEL:3 * D_MODEL])) * _dot(f_ref[...], wf_ref[...]))
    mix = _dot(merged.astype(BF16), wo_ref[...])
    y = ALPHA * x + g1_ref[...] * mix
    o_ref[...] = _layer_norm_rows(y, lg_ref[...], lb_ref[...])


def _merge_call(x, attn, ssm, four, mod3, w_in, w_ba, w_bs, w_bf, w_out, ln_g, ln_b, layer, seq_len):
    rows = x.shape[0]
    tps = seq_len // ROW_TILE
    per_seq = mod3.shape[0] > 1
    row = lambda w: pl.BlockSpec((ROW_TILE, w), lambda i: (i, 0))
    lay = lambda a: _layer_spec(a, layer, 1)
    return pl.pallas_call(
        _merge_kernel,
        grid=(rows // ROW_TILE,),
        in_specs=[row(D_MODEL), row(ATTN_W), row(SSM_W), row(FOURIER_W),
                  _mod_spec(1, tps, per_seq), _mod_spec(0, tps, per_seq), _mod_spec(2, tps, per_seq),
                  _weight_slab_spec(layer, OFF_G, GATE_W), lay(w_ba), lay(w_bs), lay(w_bf), lay(w_out), lay(ln_g),
                  lay(ln_b)],
        out_specs=row(D_MODEL),
        out_shape=jax.ShapeDtypeStruct((rows, D_MODEL), F32),
        scratch_shapes=[pltpu.VMEM((D_MODEL, GATE_W), BF16)],
        compiler_params=_params("arbitrary"),
        name="merge",
    )(x, attn, ssm, four, mod3, mod3, mod3, w_in, w_ba, w_bs, w_bf, w_out, ln_g, ln_b)


def _shift_rows(h, seq_len):
    rows = h.shape[0]
    prev = pltpu.roll(h, 1, axis=0)
    nxt = pltpu.roll(h, rows - 1, axis=0)
    sub = lax.broadcasted_iota(jnp.int32, (SUBLANES, h.shape[1]), 0)
    p_parts, n_parts = [], []
    for s0 in range(0, rows, seq_len):
        e = s0 + seq_len
        p_parts += [jnp.where(sub == 0, 0.0, prev[s0:s0 + SUBLANES]), prev[s0 + SUBLANES:e]]
        n_parts += [nxt[s0:e - SUBLANES], jnp.where(sub == SUBLANES - 1, 0.0, nxt[e - SUBLANES:e])]
    return jnp.concatenate(p_parts, axis=0), jnp.concatenate(n_parts, axis=0)


def _ffn_kernel(x_ref, sc_ref, sh_ref, g2_ref, wa_ref, wb_ref, cwa_ref, cwb_ref, cba_ref, cbb_ref, wd_ref,
                lg_ref, lb_ref, o_ref, u2, acc, *, seq_len):
    c = pl.program_id(1)
    rb = x_ref.shape[0] // FFN_ROW_BLOCKS
    blocks = [slice(i * rb, (i + 1) * rb) for i in range(FFN_ROW_BLOCKS)]

    @pl.when(c == 0)
    def _():
        u2[...] = (x_ref[...] * (1.0 + sc_ref[...]) + sh_ref[...]).astype(BF16)
        acc[...] = jnp.zeros_like(acc)

    wa = wa_ref[...].astype(BF16)
    wb = wb_ref[...].astype(BF16)
    wd = wd_ref[...].astype(BF16)
    ha = jnp.concatenate([_dot(u2[blk, :], wa) for blk in blocks], axis=0)
    hb = jnp.concatenate([_dot(u2[blk, :], wb) for blk in blocks], axis=0)

    def conv(h, cw_ref, cb_ref):
        prev, nxt = _shift_rows(h, seq_len)
        return prev * cw_ref[0:1, :] + h * cw_ref[1:2, :] + nxt * cw_ref[2:3, :] + cb_ref[...]

    a = conv(ha, cwa_ref, cba_ref)
    b = conv(hb, cwb_ref, cbb_ref)
    gated = ((a * _sigmoid(a)) * b).astype(BF16)
    for blk in blocks:
        acc[blk, :] += _dot(gated[blk, :], wd)

    @pl.when(c == pl.num_programs(1) - 1)
    def _():
        y = ALPHA * x_ref[...] + g2_ref[...] * acc[...]
        o_ref[...] = _layer_norm_rows(y, lg_ref[...], lb_ref[...])


def _ffn_call(x, mod3, w_up, conv_w, conv_b, w_down, ln_g, ln_b, layer, seq_len):
    rows = x.shape[0]
    tile = FFN_ROWS
    n_chunks = D_FF // FFN_CHUNK
    seqs_per_tile = tile // seq_len
    per_seq = mod3.shape[0] > 1
    if per_seq:
        mod = lambda ch: pl.BlockSpec((None, 1, D_MODEL), lambda i, c: (i * seqs_per_tile, 0, ch))
    else:
        mod = lambda ch: pl.BlockSpec((None, 1, D_MODEL), lambda i, c: (0, 0, ch))
    rowspec = pl.BlockSpec((tile, D_MODEL), lambda i, c: (i, 0))
    cols = lambda r, off: pl.BlockSpec((None, r, FFN_CHUNK), lambda i, c: (layer, 0, off + c))
    return pl.pallas_call(
        functools.partial(_ffn_kernel, seq_len=seq_len),
        grid=(rows // tile, n_chunks),
        in_specs=[rowspec, mod(4), mod(3), mod(5),
                  cols(D_MODEL, 0), cols(D_MODEL, n_chunks), cols(3, 0), cols(3, n_chunks), cols(1, 0), cols(1, n_chunks),
                  pl.BlockSpec((None, FFN_CHUNK, D_MODEL), lambda i, c: (layer, c, 0)),
                  _layer_spec(ln_g, layer, 2), _layer_spec(ln_b, layer, 2)],
        out_specs=rowspec,
        out_shape=jax.ShapeDtypeStruct((rows, D_MODEL), F32),
        scratch_shapes=[pltpu.VMEM((tile, D_MODEL), BF16), pltpu.VMEM((tile, D_MODEL), F32)],
        compiler_params=_params("parallel", "arbitrary"),
        name="conv_mlp",
    )(x, mod3, mod3, mod3, w_up, w_up, conv_w, conv_w, conv_b, conv_b, w_down, ln_g, ln_b)


def _rope_tables(n_tokens):
    rows = n_tokens // GRID_W
    row = np.repeat(np.arange(rows, dtype=np.float64), GRID_W)
    col = np.tile(np.arange(GRID_W, dtype=np.float64), rows)
    freqs = ROPE_THETA ** (-np.arange(ROPE_AXIS_PAIRS, dtype=np.float64) / ROPE_AXIS_PAIRS)
    ang = np.concatenate([row[:, None] * freqs, col[:, None] * freqs], axis=-1)
    cos = np.repeat(np.cos(ang), 2, axis=-1)
    sin = np.repeat(np.sin(ang), 2, axis=-1) * np.tile(np.array([-1.0, 1.0]), HEAD_DIM // 2)
    return (jnp.asarray(np.tile(cos, (1, N_Q_HEADS)), F32), jnp.asarray(np.tile(sin, (1, N_Q_HEADS)), F32))


def _trunk_layer(x, mod3, w, layer, batch, seq_len, rope_tabs, cache_k, cache_v, h0, kv_dtype):
    q, k, v, s_tm, f_in = _inproj_call(x, mod3, w["w_in"], w["gq"], w["gk"], w["ones"], rope_tabs, layer,
                                       batch, seq_len, kv_dtype)
    attn = _attn_call(q, k, v, cache_k, cache_v, layer, batch, seq_len)
    four = _fourier_call(f_in, batch, seq_len)
    ssm, hfin_f, hfin_b = _s5_calls(s_tm, batch, seq_len, layer, w["lam_re"], w["lam_im"], w["bw"], w["cw"], h0,
                                    w["d"], w["w_glu"], w["b_glu"])
    x1 = _merge_call(x, attn, ssm, four, mod3, w["w_in"], w["w_ba"], w["w_bs"], w["w_bf"], w["w_out"],
                     w["ln1_g"], w["ln1_b"], layer, seq_len)
    x2 = _ffn_call(x1, mod3, w["w_up"], w["conv_w"], w["conv_b"], w["w_down"], w["ln2_g"], w["ln2_b"], layer, seq_len)
    return x2, k, v, hfin_f, hfin_b


def _split_state(h):
    b = h.shape[0]
    re = h[..., 0].reshape(b, 2, N_STATES)
    im = h[..., 1].reshape(b, 2, N_STATES)
    return jnp.concatenate([re, im], axis=-1).transpose(1, 0, 2)


def _join_state(hf, hb):
    def one(h):
        b = h.shape[0]
        re = h[:, 0:N_STATES].reshape(b, N_SSM_GROUPS, STATE_P)
        im = h[:, N_STATES:].reshape(b, N_SSM_GROUPS, STATE_P)
        return jnp.stack([re, im], axis=-1)
    return jnp.stack([one(hf), one(hb)], axis=1)


def kernel(x_prompt, x_sample, c, cache_k, cache_v, state_ssm, c_ctx, w_ada, b_ada, w_in, q_norm_g, k_norm_g, ssm_a_re, ssm_a_im, ssm_log_dt, ssm_b_re, ssm_b_im, ssm_c_re, ssm_c_im, ssm_d, w_glu, b_glu, w_br_attn, w_br_ssm, w_br_four, w_out, ln1_g, ln1_b, w_up, conv_w, conv_b, w_down, ln2_g, ln2_b):
    bc, lc, _ = x_prompt.shape
    bd, ld, _ = x_sample.shape
    past = cache_k.shape[2]

    cvec = jnp.zeros((MOD_ROWS, D_MODEL), F32).at[0:bd].set(c).at[bd].set(c_ctx)
    mod = _mod_call(cvec, w_ada, b_ada)

    lam_re, lam_im, bw, cw = _zoh_call(ssm_a_re, ssm_a_im, ssm_log_dt, ssm_b_re, ssm_b_im, ssm_c_re, ssm_c_im)

    head_mean = np.kron(np.eye(N_Q_HEADS), np.full((HEAD_DIM, HEAD_DIM), 1.0 / HEAD_DIM))
    rope_tabs = _rope_tables(ld)
    vec = lambda a: a.reshape(DEPTH, 1, -1)
    w = dict(
        w_in=w_in, gq=vec(jnp.tile(q_norm_g, (1, N_Q_HEADS))), gk=vec(jnp.tile(k_norm_g, (1, N_KV_HEADS))),
        ones=jnp.asarray(head_mean, F32).astype(BF16), lam_re=lam_re, lam_im=lam_im, bw=bw, cw=cw, d=vec(ssm_d),
        w_glu=w_glu.astype(BF16), b_glu=vec(b_glu), w_ba=w_br_attn.astype(BF16), w_bs=w_br_ssm.astype(BF16),
        w_bf=w_br_four.astype(BF16), w_out=w_out.astype(BF16), ln1_g=vec(ln1_g), ln1_b=vec(ln1_b),
        w_up=w_up, conv_w=conv_w, conv_b=vec(conv_b), w_down=w_down, ln2_g=vec(ln2_g), ln2_b=vec(ln2_b))

    h = x_prompt.reshape(bc * lc, D_MODEL)
    h0_ctx = jnp.zeros((2, bc, 2 * N_STATES), F32)
    ks, vs, ss = [], [], []
    for l in range(DEPTH):
        mod_ctx = mod[l, bd:bd + 1].reshape(1, 1, 6 * D_MODEL)
        h, k_l, v_l, hf, hb = _trunk_layer(h, mod_ctx, w, l, bc, lc, None, None, None, h0_ctx, F32)
        ks.append(k_l.reshape(bc, lc, N_KV_HEADS, HEAD_DIM))
        vs.append(v_l.reshape(bc, lc, N_KV_HEADS, HEAD_DIM))
        ss.append(_join_state(hf, hb))
    new_cache_k = jnp.stack(ks, axis=1)
    new_cache_v = jnp.stack(vs, axis=1)
    new_state = jnp.stack(ss, axis=1)

    zx = x_sample.reshape(bd * ld, D_MODEL)
    ck = cache_k.reshape(bd, DEPTH, past, KV_W).astype(BF16)
    cv = cache_v.reshape(bd, DEPTH, past, KV_W).astype(BF16)
    for l in range(DEPTH):
        mod_dec = mod[l, 0:bd].reshape(bd, 1, 6 * D_MODEL)
        zx, _, _, _, _ = _trunk_layer(zx, mod_dec, w, l, bd, ld, rope_tabs, ck, cv,
                                      _split_state(state_ssm[:, l]), BF16)

    return (h.reshape(bc, lc, D_MODEL), zx.reshape(bd, ld, D_MODEL), new_cache_k, new_cache_v, new_state)
```

```python
import functools
import math

import jax
import jax.numpy as jnp
import numpy as np
from jax import lax
from jax.experimental import pallas as pl
from jax.experimental.pallas import tpu as pltpu

F32 = jnp.float32
BF16 = jnp.bfloat16

D_MODEL = 1024
DEPTH = 2
GRID_W = 64
HEAD_DIM = 64
N_Q_HEADS = 8
N_KV_HEADS = 2
GQA_GROUP = N_Q_HEADS // N_KV_HEADS
ATTN_W = N_Q_HEADS * HEAD_DIM
KV_W = N_KV_HEADS * HEAD_DIM
ROPE_AXIS_PAIRS = HEAD_DIM // 4
ROPE_THETA = 10000.0
SSM_W = D_MODEL // 2
SSM_GROUP = 16
N_SSM_GROUPS = SSM_W // SSM_GROUP
STATE_P = 64
N_STATES = N_SSM_GROUPS * STATE_P
FOURIER_W = D_MODEL // 2
FOURIER_GROUPS = 4
FOURIER_GW = FOURIER_W // FOURIER_GROUPS
N_BRANCH = 3
GATE_W = N_BRANCH * D_MODEL
D_IN = ATTN_W + 2 * KV_W + SSM_W + FOURIER_W + GATE_W
OFF_K = ATTN_W
OFF_V = ATTN_W + KV_W
OFF_S = ATTN_W + 2 * KV_W
OFF_F = OFF_S + SSM_W
OFF_G = OFF_F + FOURIER_W
D_FF = ((8 * D_MODEL // 3 + 127) // 128) * 128
EPS = 1e-6
ALPHA = (2 * DEPTH) ** 0.25

V7X_VMEM_BYTES = 64 * 1024 * 1024
VMEM_LIMIT_BYTES = V7X_VMEM_BYTES - 8 * 1024 * 1024
LANES = 128
SUBLANES = 8
SSM_BUNDLE_GROUPS = LANES // SSM_GROUP
SSM_BUNDLES = N_SSM_GROUPS // SSM_BUNDLE_GROUPS
BUNDLE_STATES = SSM_BUNDLE_GROUPS * STATE_P

ROW_TILE = 256
ATTN_Q_ROWS = 512
FFN_ROWS = 1024
FFN_CHUNK = 256
FFN_ROW_BLOCKS = 8
SCAN_ROWS = 512
SCAN_LANES = 512
MOD_COLS = 1536
MOD_ROWS = 16


def _params(*sem):
    return pltpu.CompilerParams(dimension_semantics=sem, vmem_limit_bytes=VMEM_LIMIT_BYTES)


def _dot(a, b):
    return jnp.dot(a, b, preferred_element_type=F32)


def _sigmoid(x):
    return 1.0 / (1.0 + jnp.exp(-x))


def _layer_spec(arr, layer, n_grid):
    zeros = (0,) * (arr.ndim - 1)
    if n_grid == 1:
        return pl.BlockSpec((None,) + arr.shape[1:], lambda i: (layer,) + zeros)
    return pl.BlockSpec((None,) + arr.shape[1:], lambda i, j: (layer,) + zeros)


def _mod_spec(chunk, tiles_per_seq, per_seq):
    if per_seq:
        return pl.BlockSpec((None, 1, D_MODEL), lambda i: (i // tiles_per_seq, 0, chunk))
    return pl.BlockSpec((None, 1, D_MODEL), lambda i: (0, 0, chunk))


def _mod_kernel(c_ref, w_ref, b_ref, o_ref):
    c = c_ref[...]
    s = (c * _sigmoid(c)).astype(BF16)
    o_ref[...] = _dot(s, w_ref[...].astype(BF16)) + b_ref[...]


def _mod_call(cvec, w_ada, b_ada):
    n = 6 * D_MODEL
    return pl.pallas_call(
        _mod_kernel,
        grid=(DEPTH, n // MOD_COLS),
        in_specs=[
            pl.BlockSpec((MOD_ROWS, D_MODEL), lambda l, j: (0, 0)),
            pl.BlockSpec((None, D_MODEL, MOD_COLS), lambda l, j: (l, 0, j)),
            pl.BlockSpec((None, 1, MOD_COLS), lambda l, j: (l, 0, j)),
        ],
        out_specs=pl.BlockSpec((None, MOD_ROWS, MOD_COLS), lambda l, j: (l, 0, j)),
        out_shape=jax.ShapeDtypeStruct((DEPTH, MOD_ROWS, n), F32),
        compiler_params=_params("parallel", "parallel"),
        name="mod",
    )(cvec, w_ada, b_ada.reshape(DEPTH, 1, n))


def _swap_pairs(x):
    w = x.shape[-1]
    nxt = pltpu.roll(x, w - 1, axis=1)
    prv = pltpu.roll(x, 1, axis=1)
    lane = lax.broadcasted_iota(jnp.int32, x.shape, 1)
    return jnp.where((lane & 1) == 0, nxt, prv)


def _weight_slab_spec(layer, col0, width):
    return pl.BlockSpec((pl.Element(1), pl.Element(D_MODEL), pl.Element(width)), lambda i: (layer, 0, col0),
                        pipeline_mode=pl.Buffered(1))


def _inproj_kernel(*refs, rope):
    if rope:
        (x_ref, sc_ref, sh_ref, w_ref, gq_ref, gk_ref, ones_ref, cos_ref, sin_ref,
         q_ref, k_ref, v_ref, s_ref, f_ref, wbf) = refs
    else:
        (x_ref, sc_ref, sh_ref, w_ref, gq_ref, gk_ref, ones_ref,
         q_ref, k_ref, v_ref, s_ref, f_ref, wbf) = refs
    batch, steps, _ = x_ref.shape
    rows = batch * steps

    @pl.when(pl.program_id(0) == 0)
    def _():
        wbf[...] = w_ref[0].astype(BF16)

    u = (x_ref[...] * (1.0 + sc_ref[...]) + sh_ref[...]).reshape(rows, D_MODEL).astype(BF16)

    zq = _dot(u, wbf[:, 0:OFF_K])
    zk = _dot(u, wbf[:, OFF_K:OFF_V])
    zv = _dot(u, wbf[:, OFF_V:OFF_S])
    msq_q = _dot((zq * zq).astype(BF16), ones_ref[...])
    msq_k = _dot((zk * zk).astype(BF16), ones_ref[0:KV_W, 0:KV_W])
    q = zq * lax.rsqrt(msq_q + EPS) * gq_ref[...]
    k = zk * lax.rsqrt(msq_k + EPS) * gk_ref[...]
    if rope:
        cos = jnp.concatenate([cos_ref[...]] * batch, axis=0)
        sin = jnp.concatenate([sin_ref[...]] * batch, axis=0)
        q = q * cos + _swap_pairs(q) * sin
        k = k * cos[:, 0:KV_W] + _swap_pairs(k) * sin[:, 0:KV_W]
    q_ref[...] = (q * (HEAD_DIM ** -0.5)).reshape(batch, steps, ATTN_W).astype(q_ref.dtype)
    k_ref[...] = k.reshape(batch, steps, KV_W).astype(k_ref.dtype)
    v_ref[...] = zv.reshape(batch, steps, KV_W).astype(v_ref.dtype)
    f_ref[...] = _dot(u, wbf[:, OFF_F:OFF_G]).reshape(batch, steps, FOURIER_W).astype(f_ref.dtype)
    s = _dot(u, wbf[:, OFF_S:OFF_F]).reshape(batch, steps, SSM_W)
    s_ref[...] = jnp.swapaxes(s, 0, 1).reshape(rows, SSM_W).astype(s_ref.dtype)


def _inproj_call(x, mod3, w_in, gq, gk, ones_blk, rope_tabs, layer, batch, seq_len, kv_dtype):
    steps = ROW_TILE // batch
    rope = rope_tabs is not None
    x3 = x.reshape(batch, seq_len, D_MODEL)
    seq = lambda w: pl.BlockSpec((batch, steps, w), lambda i: (0, i, 0))
    mod = lambda ch: pl.BlockSpec((mod3.shape[0], 1, D_MODEL), lambda i: (0, 0, ch))
    in_specs = [seq(D_MODEL), mod(1), mod(0), _weight_slab_spec(layer, 0, OFF_G),
                _layer_spec(gq, layer, 1), _layer_spec(gk, layer, 1), pl.BlockSpec(ones_blk.shape, lambda i: (0, 0))]
    args = [x3, mod3, mod3, w_in, gq, gk, ones_blk]
    if rope:
        tab = pl.BlockSpec((steps, ATTN_W), lambda i: (i, 0))
        in_specs += [tab, tab]
        args += list(rope_tabs)
    widths = (ATTN_W, KV_W, KV_W, FOURIER_W)
    dtypes = (BF16, kv_dtype, kv_dtype, BF16)
    outs = pl.pallas_call(
        functools.partial(_inproj_kernel, rope=rope),
        grid=(seq_len // steps,),
        in_specs=in_specs,
        out_specs=[seq(w) for w in widths[:3]] + [pl.BlockSpec((ROW_TILE, SSM_W), lambda i: (i, 0)), seq(FOURIER_W)],
        out_shape=([jax.ShapeDtypeStruct((batch, seq_len, w), d) for w, d in zip(widths[:3], dtypes[:3])]
                   + [jax.ShapeDtypeStruct((seq_len * batch, SSM_W), F32),
                      jax.ShapeDtypeStruct((batch, seq_len, FOURIER_W), BF16)]),
        scratch_shapes=[pltpu.VMEM((D_MODEL, OFF_G), BF16)],
        compiler_params=_params("arbitrary"),
        name="in_proj",
    )(*args)
    q, k, v, s_tm, f = outs
    flat = lambda a: a.reshape(batch * seq_len, a.shape[-1])
    return flat(q), flat(k), flat(v), s_tm, flat(f)


def _attn_kernel(*refs, has_cache):
    if has_cache:
        q_ref, k_ref, v_ref, ck_ref, cv_ref, o_ref = refs
    else:
        q_ref, k_ref, v_ref, o_ref = refs
    q = q_ref[...].astype(F32)
    contract_last = (((1,), (1,)), ((), ()))
    outs = []
    for h in range(N_Q_HEADS):
        j = h // GQA_GROUP
        lo, hi = j * HEAD_DIM, (j + 1) * HEAD_DIM
        qh = q[:, h * HEAD_DIM:(h + 1) * HEAD_DIM].astype(BF16)
        s1 = lax.dot_general(qh, k_ref[:, lo:hi].astype(BF16), contract_last, preferred_element_type=F32)
        m = jnp.max(s1, axis=-1, keepdims=True)
        if has_cache:
            s2 = lax.dot_general(qh, ck_ref[:, lo:hi], contract_last, preferred_element_type=F32)
            m = jnp.maximum(m, jnp.max(s2, axis=-1, keepdims=True))
        p1 = jnp.exp(s1 - m)
        den = jnp.sum(p1, axis=-1, keepdims=True)
        o = _dot(p1.astype(BF16), v_ref[:, lo:hi].astype(BF16))
        if has_cache:
            p2 = jnp.exp(s2 - m)
            den = den + jnp.sum(p2, axis=-1, keepdims=True)
            o = o + _dot(p2.astype(BF16), cv_ref[:, lo:hi])
        outs.append(o * (1.0 / den))
    o_ref[...] = jnp.concatenate(outs, axis=1).astype(o_ref.dtype)


def _attn_call(q, k, v, cache_k, cache_v, layer, batch, seq_len):
    rows = q.shape[0]
    qb = min(ATTN_Q_ROWS, seq_len)
    nq = seq_len // qb
    has_cache = cache_k is not None
    in_specs = [pl.BlockSpec((qb, ATTN_W), lambda b, i: (b * nq + i, 0)),
                pl.BlockSpec((seq_len, KV_W), lambda b, i: (b, 0)),
                pl.BlockSpec((seq_len, KV_W), lambda b, i: (b, 0))]
    args = [q, k, v]
    if has_cache:
        past = cache_k.shape[2]
        in_specs += [pl.BlockSpec((None, None, past, KV_W), lambda b, i: (b, layer, 0, 0))] * 2
        args += [cache_k, cache_v]
    return pl.pallas_call(
        functools.partial(_attn_kernel, has_cache=has_cache),
        grid=(batch, nq),
        in_specs=in_specs,
        out_specs=pl.BlockSpec((qb, ATTN_W), lambda b, i: (b * nq + i, 0)),
        out_shape=jax.ShapeDtypeStruct((rows, ATTN_W), BF16),
        compiler_params=_params("parallel", "parallel"),
        name="attention",
    )(*args)


def _fourier_kernel(f_ref, wc_ref, wl_ref, o_ref):
    f = f_ref[...]
    parts = [_dot(f[:, g * FOURIER_GW:(g + 1) * FOURIER_GW], wc_ref[...]) for g in range(FOURIER_GROUPS)]
    xc = jnp.concatenate([p[:, 0:FOURIER_GW] for p in parts], axis=1)
    xs = jnp.concatenate([p[:, FOURIER_GW:2 * FOURIER_GW] for p in parts], axis=1)
    stacked = jnp.concatenate([xc, xs], axis=0).astype(BF16)
    o_ref[...] = _dot(wl_ref[...], stacked).astype(o_ref.dtype)


def _dft_tables(seq_len):
    c = np.arange(FOURIER_GW)
    ang_c = 2.0 * np.pi * np.outer(c, c) / FOURIER_GW
    wc = np.concatenate([np.cos(ang_c), np.sin(ang_c)], axis=1) / math.sqrt(FOURIER_GW)
    t = np.arange(seq_len)
    ang_l = 2.0 * np.pi * np.outer(t, t) / seq_len
    wl = np.concatenate([np.cos(ang_l), -np.sin(ang_l)], axis=1) / math.sqrt(seq_len)
    return (jnp.asarray(wc, F32).astype(BF16), jnp.asarray(wl, F32).astype(BF16))


def _fourier_call(f, batch, seq_len):
    wc, wl = _dft_tables(seq_len)
    return pl.pallas_call(
        _fourier_kernel,
        grid=(batch,),
        in_specs=[pl.BlockSpec((seq_len, FOURIER_W), lambda b: (b, 0)),
                  pl.BlockSpec(wc.shape, lambda b: (0, 0)),
                  pl.BlockSpec(wl.shape, lambda b: (0, 0))],
        out_specs=pl.BlockSpec((seq_len, FOURIER_W), lambda b: (b, 0)),
        out_shape=jax.ShapeDtypeStruct(f.shape, BF16),
        compiler_params=_params("parallel"),
        name="fourier",
    )(f, wc, wl)


def _block_diagonal(compact, group_rows, group_cols):
    rows, w = compact.shape
    log2 = lambda n: n.bit_length() - 1
    sel = (lax.broadcasted_iota(jnp.int32, (w, SSM_BUNDLE_GROUPS * w), 0)
           == (lax.broadcasted_iota(jnp.int32, (w, SSM_BUNDLE_GROUPS * w), 1) & (w - 1)))
    wide = _dot(compact.astype(BF16), jnp.where(sel, 1.0, 0.0).astype(BF16))
    row_group = (lax.broadcasted_iota(jnp.int32, wide.shape, 0) >> log2(group_rows)) & (SSM_BUNDLE_GROUPS - 1)
    col_group = lax.broadcasted_iota(jnp.int32, wide.shape, 1) >> log2(group_cols)
    return jnp.where(row_group == col_group, wide, 0.0)


def _zoh_lambda(a_re, a_im, log_dt):
    dt = jnp.exp(log_dt)
    mag = jnp.exp(a_re * dt)
    ang = a_im * dt
    return mag * jnp.cos(ang), mag * jnp.sin(ang)


def _zoh_kernel(are_ref, aim_ref, ldt_ref, bre_ref, bim_ref, arec_ref, aimc_ref, ldtc_ref, cre_ref, cim_ref,
                l2re_ref, l2im_ref, bw_ref, cw_ref, k0_ref):
    a_re = are_ref[...]
    a_im = aim_ref[...]
    lb_re, lb_im = _zoh_lambda(a_re, a_im, ldt_ref[...])
    den = a_re * a_re + a_im * a_im
    n_re = lb_re - 1.0
    n_im = lb_im
    f_re = (n_re * a_re + n_im * a_im) / den
    f_im = (n_im * a_re - n_re * a_im) / den
    l2re_ref[...] = lb_re * lb_re - lb_im * lb_im
    l2im_ref[...] = 2.0 * lb_re * lb_im
    b_re = bre_ref[...]
    b_im = bim_ref[...]
    bb_re = f_re[:, None, :] * b_re - f_im[:, None, :] * b_im
    bb_im = f_re[:, None, :] * b_im + f_im[:, None, :] * b_re
    lbb_re = lb_re[:, None, :] * bb_re - lb_im[:, None, :] * bb_im
    lbb_im = lb_re[:, None, :] * bb_im + lb_im[:, None, :] * bb_re
    flat = lambda a: a.reshape(a.shape[0] * a.shape[1], a.shape[2])
    n_slabs = bw_ref.shape[0]
    in_slab = lambda a: _block_diagonal(flat(a), SSM_GROUP, STATE_P).reshape(n_slabs, LANES, BUNDLE_STATES)
    bd_re, bd_im = in_slab(bb_re), in_slab(bb_im)
    bw_ref[:, 0, :, 0:BUNDLE_STATES] = in_slab(lbb_re).astype(BF16)
    bw_ref[:, 0, :, BUNDLE_STATES:] = in_slab(lbb_im).astype(BF16)
    bw_ref[:, 1, :, 0:BUNDLE_STATES] = bd_re.astype(BF16)
    bw_ref[:, 1, :, BUNDLE_STATES:] = bd_im.astype(BF16)
    lc_re, lc_im = _zoh_lambda(arec_ref[...], aimc_ref[...], ldtc_ref[...])
    c_re = cre_ref[...]
    c_im = cim_ref[...]
    cl_re = c_re * lc_re - c_im * lc_im
    cl_im = c_re * lc_im + c_im * lc_re
    out_slab = lambda a: _block_diagonal(flat(a), STATE_P, SSM_GROUP).reshape(n_slabs, BUNDLE_STATES, LANES)
    cd_re, cd_im = out_slab(c_re), out_slab(c_im)
    cw_ref[:, 0, :, 0:LANES] = cd_re.astype(BF16)
    cw_ref[:, 0, :, LANES:] = out_slab(cl_re).astype(BF16)
    cw_ref[:, 1, :, 0:LANES] = (-cd_im).astype(BF16)
    cw_ref[:, 1, :, LANES:] = (-out_slab(cl_im)).astype(BF16)
    for s in range(n_slabs):
        k0 = _dot(bd_re[s].astype(BF16), cd_re[s].astype(BF16)) - _dot(bd_im[s].astype(BF16), cd_im[s].astype(BF16))
        k0_ref[s] = k0.astype(BF16)


def _zoh_call(a_re, a_im, log_dt, b_re, b_im, c_re, c_im):
    rows = DEPTH * 2 * N_SSM_GROUPS
    n_dirs = DEPTH * 2
    are = a_re.reshape(rows, STATE_P)
    aim = a_im.reshape(rows, STATE_P)
    ldt = log_dt.reshape(rows, 1)
    bre = jnp.swapaxes(b_re, -1, -2).reshape(rows, SSM_GROUP, STATE_P)
    bim = jnp.swapaxes(b_im, -1, -2).reshape(rows, SSM_GROUP, STATE_P)
    cre = jnp.swapaxes(c_re, -1, -2).reshape(rows, STATE_P, SSM_GROUP)
    cim = jnp.swapaxes(c_im, -1, -2).reshape(rows, STATE_P, SSM_GROUP)
    args = (are, aim, ldt, bre, bim, are.reshape(rows, STATE_P, 1), aim.reshape(rows, STATE_P, 1),
            ldt.reshape(rows, 1, 1), cre, cim)
    per_dir = lambda shape: pl.BlockSpec((shape[0] // n_dirs,) + tuple(shape[1:]),
                                         lambda i: (i,) + (0,) * (len(shape) - 1))
    n_slabs = n_dirs * SSM_BUNDLES
    outs = [jax.ShapeDtypeStruct((rows, STATE_P), F32)] * 2 + [
        jax.ShapeDtypeStruct((n_slabs, 2, LANES, 2 * BUNDLE_STATES), BF16),
        jax.ShapeDtypeStruct((n_slabs, 2, BUNDLE_STATES, 2 * LANES), BF16),
        jax.ShapeDtypeStruct((n_slabs, LANES, LANES), BF16)]
    l2_re, l2_im, bw, cw, k0 = pl.pallas_call(
        _zoh_kernel,
        grid=(n_dirs,),
        in_specs=[per_dir(a.shape) for a in args],
        out_specs=[per_dir(o.shape) for o in outs],
        out_shape=outs,
        compiler_params=_params("parallel"),
        name="s5_zoh",
    )(*args)
    lead = (DEPTH, 2, SSM_BUNDLES)
    return (l2_re.reshape(DEPTH, 2, 1, N_STATES), l2_im.reshape(DEPTH, 2, 1, N_STATES),
            bw.reshape(*lead, 2 * LANES, 2 * BUNDLE_STATES), cw.reshape(*lead, 2 * BUNDLE_STATES, 2 * LANES),
            k0.reshape(*lead, LANES, LANES))


def _state_slabs(h, k):
    return (h[:, k * BUNDLE_STATES:(k + 1) * BUNDLE_STATES],
            h[:, N_STATES + k * BUNDLE_STATES:N_STATES + (k + 1) * BUNDLE_STATES])


def _scan_project_state(h_re, h_im, cw_ref, k):
    return (_dot(h_re.astype(BF16), cw_ref[k, 0:BUNDLE_STATES, :])
            + _dot(h_im.astype(BF16), cw_ref[k, BUNDLE_STATES:, :]))


def _scan_init(h0_ref, cw_ref, hc, wp):
    hc[...] = h0_ref[...]
    for k in range(SSM_BUNDLES):
        h_re, h_im = _state_slabs(h0_ref, k)
        wp[:, k * LANES:(k + 1) * LANES] = _scan_project_state(h_re, h_im, cw_ref, k)[:, LANES:]


def _scan_direction(u_ref, bw_ref, cw_ref, k0_ref, l2re_ref, l2im_ref, y_ref, hs, hc, wp, *, batch, reverse):
    rows = u_ref.shape[0]
    pairs = rows // (2 * batch)
    half = pairs * batch
    u4 = u_ref[...].reshape(pairs, 2, batch, SSM_W)
    early, late = u4[:, 0].reshape(half, SSM_W), u4[:, 1].reshape(half, SSM_W)
    first, second = (late, early) if reverse else (early, late)
    first, second = first.astype(BF16), second.astype(BF16)

    for k in range(SSM_BUNDLES):
        slab = slice(k * LANES, (k + 1) * LANES)
        r = _dot(jnp.concatenate([first[:, slab], second[:, slab]], axis=1), bw_ref[k])
        hs[:, k * BUNDLE_STATES:(k + 1) * BUNDLE_STATES] = r[:, 0:BUNDLE_STATES]
        hs[:, N_STATES + k * BUNDLE_STATES:N_STATES + (k + 1) * BUNDLE_STATES] = r[:, BUNDLE_STATES:]

    for blk in range(N_STATES // SCAN_LANES):
        re = slice(blk * SCAN_LANES, (blk + 1) * SCAN_LANES)
        im = slice(N_STATES + blk * SCAN_LANES, N_STATES + (blk + 1) * SCAN_LANES)
        lr = jnp.broadcast_to(l2re_ref[:, re], (batch, SCAN_LANES))
        li = jnp.broadcast_to(l2im_ref[:, re], (batch, SCAN_LANES))
        hr, hi = hc[:, re], hc[:, im]
        for j in range(pairs):
            jj = pairs - 1 - j if reverse else j
            prow = slice(jj * batch, (jj + 1) * batch)
            hr, hi = (lr * hr - li * hi + hs[prow, re], lr * hi + li * hr + hs[prow, im])
            hs[prow, re] = hr
            hs[prow, im] = hi
        hc[:, re] = hr
        hc[:, im] = hi

    y_second, w, local = [], [], []
    for k in range(SSM_BUNDLES):
        h_re, h_im = _state_slabs(hs, k)
        z = _scan_project_state(h_re, h_im, cw_ref, k)
        y_second.append(z[:, 0:LANES])
        w.append(z[:, LANES:])
        local.append(_dot(first[:, k * LANES:(k + 1) * LANES], k0_ref[k]))
    y_second = jnp.concatenate(y_second, axis=1)
    w = jnp.concatenate(w, axis=1)
    if reverse:
        w_prev = jnp.concatenate([w[batch:], wp[...]], axis=0)
        wp[...] = w[0:batch]
    else:
        w_prev = jnp.concatenate([wp[...], w[0:half - batch]], axis=0)
        wp[...] = w[half - batch:]
    y_first = w_prev + jnp.concatenate(local, axis=1)
    y_early, y_late = (y_second, y_first) if reverse else (y_first, y_second)
    y = jnp.concatenate([y_early.reshape(pairs, 1, batch, SSM_W), y_late.reshape(pairs, 1, batch, SSM_W)], axis=1)
    y_ref[...] = y.reshape(rows, SSM_W).astype(y_ref.dtype)


def _s5_scan_kernel(uf_ref, ub_ref, bwf_ref, cwf_ref, k0f_ref, l2rf_ref, l2if_ref, bwb_ref, cwb_ref, k0b_ref,
                    l2rb_ref, l2ib_ref, h0f_ref, h0b_ref, yf_ref, yb_ref, hff_ref, hfb_ref,
                    hsf, hsb, hcf, hcb, wpf, wpb, *, batch):
    @pl.when(pl.program_id(0) == 0)
    def _():
        _scan_init(h0f_ref, cwf_ref, hcf, wpf)
        _scan_init(h0b_ref, cwb_ref, hcb, wpb)

    _scan_direction(uf_ref, bwf_ref, cwf_ref, k0f_ref, l2rf_ref, l2if_ref, yf_ref, hsf, hcf, wpf,
                    batch=batch, reverse=False)
    _scan_direction(ub_ref, bwb_ref, cwb_ref, k0b_ref, l2rb_ref, l2ib_ref, yb_ref, hsb, hcb, wpb,
                    batch=batch, reverse=True)
    hff_ref[...] = hcf[...]
    hfb_ref[...] = hcb[...]


def _gelu_tanh(x):
    return 0.5 * x * (1.0 + jnp.tanh(math.sqrt(2.0 / math.pi) * (x + 0.044715 * (x * x * x))))


def _s5_post_kernel(u_ref, yf_ref, yb_ref, d_ref, wg_ref, bg_ref, o_ref):
    batch, steps, _ = o_ref.shape
    y = d_ref[...] * u_ref[...].astype(F32) + yf_ref[...].astype(F32) + yb_ref[...].astype(F32)
    y = _gelu_tanh(y)
    gate = _sigmoid(_dot(y.astype(BF16), wg_ref[...]) + bg_ref[...])
    out = jnp.swapaxes((y * gate).reshape(steps, batch, SSM_W), 0, 1)
    o_ref[...] = out.astype(o_ref.dtype)


def _s5_calls(u_tm, batch, seq_len, layer, s5w, h0, d_skip, w_glu, b_glu):
    lam2_re, lam2_im, bw, cw, k0 = s5w
    rows = u_tm.shape[0]
    steps = SCAN_ROWS // batch
    n_chunks = seq_len // steps
    fwd_chunk = pl.BlockSpec((SCAN_ROWS, SSM_W), lambda i: (i, 0))
    bwd_chunk = pl.BlockSpec((SCAN_ROWS, SSM_W), lambda i: (n_chunks - 1 - i, 0))
    dir_spec = lambda a, d: pl.BlockSpec((None, None) + a.shape[2:], lambda i: (layer, d) + (0,) * (a.ndim - 2))
    weights = lambda d: [dir_spec(a, d) for a in (bw, cw, k0, lam2_re, lam2_im)]
    h0_spec = lambda d: pl.BlockSpec((None,) + h0.shape[1:], lambda i: (d, 0, 0))
    state_spec = pl.BlockSpec((batch, 2 * N_STATES), lambda i: (0, 0))
    state = jax.ShapeDtypeStruct((batch, 2 * N_STATES), F32)
    hs = lambda: pltpu.VMEM((SCAN_ROWS // 2, 2 * N_STATES), F32)
    hc = lambda: pltpu.VMEM((batch, 2 * N_STATES), F32)
    wp = lambda: pltpu.VMEM((batch, SSM_W), F32)

    yf, yb, hfin_f, hfin_b = pl.pallas_call(
        functools.partial(_s5_scan_kernel, batch=batch),
        grid=(n_chunks,),
        in_specs=[fwd_chunk, bwd_chunk] + weights(0) + weights(1) + [h0_spec(0), h0_spec(1)],
        out_specs=[fwd_chunk, bwd_chunk, state_spec, state_spec],
        out_shape=[jax.ShapeDtypeStruct((rows, SSM_W), BF16)] * 2 + [state, state],
        scratch_shapes=[hs(), hs(), hc(), hc(), wp(), wp()],
        compiler_params=_params("arbitrary"),
        name="s5_scan",
    )(u_tm, u_tm, bw, cw, k0, lam2_re, lam2_im, bw, cw, k0, lam2_re, lam2_im, h0, h0)

    out = pl.pallas_call(
        _s5_post_kernel,
        grid=(n_chunks,),
        in_specs=[fwd_chunk, fwd_chunk, fwd_chunk, _layer_spec(d_skip, layer, 1), _layer_spec(w_glu, layer, 1),
                  _layer_spec(b_glu, layer, 1)],
        out_specs=pl.BlockSpec((batch, steps, SSM_W), lambda i: (0, i, 0)),
        out_shape=jax.ShapeDtypeStruct((batch, seq_len, SSM_W), BF16),
        compiler_params=_params("parallel"),
        name="s5_post",
    )(u_tm, yf, yb, d_skip, w_glu, b_glu)
    return out.reshape(rows, SSM_W), hfin_f, hfin_b


def _layer_norm_rows(y, g, b):
    mu = jnp.mean(y, axis=-1, keepdims=True)
    yc = y - mu
    var = jnp.mean(yc * yc, axis=-1, keepdims=True)
    return yc * lax.rsqrt(var + EPS) * g + b


def _merge_kernel(x_ref, a_ref, s_ref, f_ref, sc_ref, sh_ref, g1_ref, wg_ref, wa_ref, ws_ref, wf_ref, wo_ref,
                  lg_ref, lb_ref, o_ref, wgb):
    @pl.when(pl.program_id(0) == 0)
    def _():
        wgb[...] = wg_ref[0].astype(BF16)

    x = x_ref[...]
    u = (x * (1.0 + sc_ref[...]) + sh_ref[...]).astype(BF16)
    merged = (_sigmoid(_dot(u, wgb[:, 0:D_MODEL])) * _dot(a_ref[...], wa_ref[...])
              + _sigmoid(_dot(u, wgb[:, D_MODEL:2 * D_MODEL])) * _dot(s_ref[...], ws_ref[...])
              + _sigmoid(_dot(u, wgb[:, 2 * D_MODEL:3 * D_MODEL])) * _dot(f_ref[...], wf_ref[...]))
    mix = _dot(merged.astype(BF16), wo_ref[...])
    y = ALPHA * x + g1_ref[...] * mix
    o_ref[...] = _layer_norm_rows(y, lg_ref[...], lb_ref[...])


def _merge_call(x, attn, ssm, four, mod3, w_in, w_ba, w_bs, w_bf, w_out, ln_g, ln_b, layer, seq_len):
    rows = x.shape[0]
    tps = seq_len // ROW_TILE
    per_seq = mod3.shape[0] > 1
    row = lambda w: pl.BlockSpec((ROW_TILE, w), lambda i: (i, 0))
    lay = lambda a: _layer_spec(a, layer, 1)
    return pl.pallas_call(
        _merge_kernel,
        grid=(rows // ROW_TILE,),
        in_specs=[row(D_MODEL), row(ATTN_W), row(SSM_W), row(FOURIER_W),
                  _mod_spec(1, tps, per_seq), _mod_spec(0, tps, per_seq), _mod_spec(2, tps, per_seq),
                  _weight_slab_spec(layer, OFF_G, GATE_W), lay(w_ba), lay(w_bs), lay(w_bf), lay(w_out), lay(ln_g),
                  lay(ln_b)],
        out_specs=row(D_MODEL),
        out_shape=jax.ShapeDtypeStruct((rows, D_MODEL), F32),
        scratch_shapes=[pltpu.VMEM((D_MODEL, GATE_W), BF16)],
        compiler_params=_params("arbitrary"),
        name="merge",
    )(x, attn, ssm, four, mod3, mod3, mod3, w_in, w_ba, w_bs, w_bf, w_out, ln_g, ln_b)


def _shift_rows(h, seq_len):
    rows = h.shape[0]
    prev = pltpu.roll(h, 1, axis=0)
    nxt = pltpu.roll(h, rows - 1, axis=0)
    sub = lax.broadcasted_iota(jnp.int32, (SUBLANES, h.shape[1]), 0)
    p_parts, n_parts = [], []
    for s0 in range(0, rows, seq_len):
        e = s0 + seq_len
        p_parts += [jnp.where(sub == 0, 0.0, prev[s0:s0 + SUBLANES]), prev[s0 + SUBLANES:e]]
        n_parts += [nxt[s0:e - SUBLANES], jnp.where(sub == SUBLANES - 1, 0.0, nxt[e - SUBLANES:e])]
    return jnp.concatenate(p_parts, axis=0), jnp.concatenate(n_parts, axis=0)


def _ffn_kernel(x_ref, sc_ref, sh_ref, g2_ref, wa_ref, wb_ref, cwa_ref, cwb_ref, cba_ref, cbb_ref, wd_ref,
                lg_ref, lb_ref, o_ref, u2, acc, *, seq_len):
    c = pl.program_id(1)
    rb = x_ref.shape[0] // FFN_ROW_BLOCKS
    blocks = [slice(i * rb, (i + 1) * rb) for i in range(FFN_ROW_BLOCKS)]

    @pl.when(c == 0)
    def _():
        u2[...] = (x_ref[...] * (1.0 + sc_ref[...]) + sh_ref[...]).astype(BF16)
        acc[...] = jnp.zeros_like(acc)

    wa = wa_ref[...].astype(BF16)
    wb = wb_ref[...].astype(BF16)
    wd = wd_ref[...].astype(BF16)
    ha = jnp.concatenate([_dot(u2[blk, :], wa) for blk in blocks], axis=0)
    hb = jnp.concatenate([_dot(u2[blk, :], wb) for blk in blocks], axis=0)

    def conv(h, cw_ref, cb_ref):
        prev, nxt = _shift_rows(h, seq_len)
        return prev * cw_ref[0:1, :] + h * cw_ref[1:2, :] + nxt * cw_ref[2:3, :] + cb_ref[...]

    a = conv(ha, cwa_ref, cba_ref)
    b = conv(hb, cwb_ref, cbb_ref)
    gated = ((a * _sigmoid(a)) * b).astype(BF16)
    for blk in blocks:
        acc[blk, :] += _dot(gated[blk, :], wd)

    @pl.when(c == pl.num_programs(1) - 1)
    def _():
        y = ALPHA * x_ref[...] + g2_ref[...] * acc[...]
        o_ref[...] = _layer_norm_rows(y, lg_ref[...], lb_ref[...])


def _ffn_call(x, mod3, w_up, conv_w, conv_b, w_down, ln_g, ln_b, layer, seq_len):
    rows = x.shape[0]
    tile = FFN_ROWS
    n_chunks = D_FF // FFN_CHUNK
    seqs_per_tile = tile // seq_len
    per_seq = mod3.shape[0] > 1
    if per_seq:
        mod = lambda ch: pl.BlockSpec((None, 1, D_MODEL), lambda i, c: (i * seqs_per_tile, 0, ch))
    else:
        mod = lambda ch: pl.BlockSpec((None, 1, D_MODEL), lambda i, c: (0, 0, ch))
    rowspec = pl.BlockSpec((tile, D_MODEL), lambda i, c: (i, 0))
    cols = lambda r, off: pl.BlockSpec((None, r, FFN_CHUNK), lambda i, c: (layer, 0, off + c))
    return pl.pallas_call(
        functools.partial(_ffn_kernel, seq_len=seq_len),
        grid=(rows // tile, n_chunks),
        in_specs=[rowspec, mod(4), mod(3), mod(5),
                  cols(D_MODEL, 0), cols(D_MODEL, n_chunks), cols(3, 0), cols(3, n_chunks), cols(1, 0), cols(1, n_chunks),
                  pl.BlockSpec((None, FFN_CHUNK, D_MODEL), lambda i, c: (layer, c, 0)),
                  _layer_spec(ln_g, layer, 2), _layer_spec(ln_b, layer, 2)],
        out_specs=rowspec,
        out_shape=jax.ShapeDtypeStruct((rows, D_MODEL), F32),
        scratch_shapes=[pltpu.VMEM((tile, D_MODEL), BF16), pltpu.VMEM((tile, D_MODEL), F32)],
        compiler_params=_params("parallel", "arbitrary"),
        name="conv_mlp",
    )(x, mod3, mod3, mod3, w_up, w_up, conv_w, conv_w, conv_b, conv_b, w_down, ln_g, ln_b)


def _rope_tables(n_tokens):
    rows = n_tokens // GRID_W
    row = np.repeat(np.arange(rows, dtype=np.float64), GRID_W)
    col = np.tile(np.arange(GRID_W, dtype=np.float64), rows)
    freqs = ROPE_THETA ** (-np.arange(ROPE_AXIS_PAIRS, dtype=np.float64) / ROPE_AXIS_PAIRS)
    ang = np.concatenate([row[:, None] * freqs, col[:, None] * freqs], axis=-1)
    cos = np.repeat(np.cos(ang), 2, axis=-1)
    sin = np.repeat(np.sin(ang), 2, axis=-1) * np.tile(np.array([-1.0, 1.0]), HEAD_DIM // 2)
    return (jnp.asarray(np.tile(cos, (1, N_Q_HEADS)), F32), jnp.asarray(np.tile(sin, (1, N_Q_HEADS)), F32))


def _trunk_layer(x, mod3, w, layer, batch, seq_len, rope_tabs, cache_k, cache_v, h0, kv_dtype):
    q, k, v, s_tm, f_in = _inproj_call(x, mod3, w["w_in"], w["gq"], w["gk"], w["ones"], rope_tabs, layer,
                                       batch, seq_len, kv_dtype)
    attn = _attn_call(q, k, v, cache_k, cache_v, layer, batch, seq_len)
    four = _fourier_call(f_in, batch, seq_len)
    ssm, hfin_f, hfin_b = _s5_calls(s_tm, batch, seq_len, layer, w["s5"], h0, w["d"], w["w_glu"], w["b_glu"])
    x1 = _merge_call(x, attn, ssm, four, mod3, w["w_in"], w["w_ba"], w["w_bs"], w["w_bf"], w["w_out"],
                     w["ln1_g"], w["ln1_b"], layer, seq_len)
    x2 = _ffn_call(x1, mod3, w["w_up"], w["conv_w"], w["conv_b"], w["w_down"], w["ln2_g"], w["ln2_b"], layer, seq_len)
    return x2, k, v, hfin_f, hfin_b


def _split_state(h):
    b = h.shape[0]
    re = h[..., 0].reshape(b, 2, N_STATES)
    im = h[..., 1].reshape(b, 2, N_STATES)
    return jnp.concatenate([re, im], axis=-1).transpose(1, 0, 2)


def _join_state(hf, hb):
    def one(h):
        b = h.shape[0]
        re = h[:, 0:N_STATES].reshape(b, N_SSM_GROUPS, STATE_P)
        im = h[:, N_STATES:].reshape(b, N_SSM_GROUPS, STATE_P)
        return jnp.stack([re, im], axis=-1)
    return jnp.stack([one(hf), one(hb)], axis=1)


def kernel(x_prompt, x_sample, c, cache_k, cache_v, state_ssm, c_ctx, w_ada, b_ada, w_in, q_norm_g, k_norm_g, ssm_a_re, ssm_a_im, ssm_log_dt, ssm_b_re, ssm_b_im, ssm_c_re, ssm_c_im, ssm_d, w_glu, b_glu, w_br_attn, w_br_ssm, w_br_four, w_out, ln1_g, ln1_b, w_up, conv_w, conv_b, w_down, ln2_g, ln2_b):
    bc, lc, _ = x_prompt.shape
    bd, ld, _ = x_sample.shape
    past = cache_k.shape[2]

    cvec = jnp.zeros((MOD_ROWS, D_MODEL), F32).at[0:bd].set(c).at[bd].set(c_ctx)
    mod = _mod_call(cvec, w_ada, b_ada)

    s5w = _zoh_call(ssm_a_re, ssm_a_im, ssm_log_dt, ssm_b_re, ssm_b_im, ssm_c_re, ssm_c_im)

    head_mean = np.kron(np.eye(N_Q_HEADS), np.full((HEAD_DIM, HEAD_DIM), 1.0 / HEAD_DIM))
    rope_tabs = _rope_tables(ld)
    vec = lambda a: a.reshape(DEPTH, 1, -1)
    w = dict(
        w_in=w_in, gq=vec(jnp.tile(q_norm_g, (1, N_Q_HEADS))), gk=vec(jnp.tile(k_norm_g, (1, N_KV_HEADS))),
        ones=jnp.asarray(head_mean, F32).astype(BF16), s5=s5w, d=vec(ssm_d),
        w_glu=w_glu.astype(BF16), b_glu=vec(b_glu), w_ba=w_br_attn.astype(BF16), w_bs=w_br_ssm.astype(BF16),
        w_bf=w_br_four.astype(BF16), w_out=w_out.astype(BF16), ln1_g=vec(ln1_g), ln1_b=vec(ln1_b),
        w_up=w_up, conv_w=conv_w, conv_b=vec(conv_b), w_down=w_down, ln2_g=vec(ln2_g), ln2_b=vec(ln2_b))

    h = x_prompt.reshape(bc * lc, D_MODEL)
    h0_ctx = jnp.zeros((2, bc, 2 * N_STATES), F32)
    ks, vs, ss = [], [], []
    for l in range(DEPTH):
        mod_ctx = mod[l, bd:bd + 1].reshape(1, 1, 6 * D_MODEL)
        h, k_l, v_l, hf, hb = _trunk_layer(h, mod_ctx, w, l, bc, lc, None, None, None, h0_ctx, F32)
        ks.append(k_l.reshape(bc, lc, N_KV_HEADS, HEAD_DIM))
        vs.append(v_l.reshape(bc, lc, N_KV_HEADS, HEAD_DIM))
        ss.append(_join_state(hf, hb))
    new_cache_k = jnp.stack(ks, axis=1)
    new_cache_v = jnp.stack(vs, axis=1)
    new_state = jnp.stack(ss, axis=1)

    zx = x_sample.reshape(bd * ld, D_MODEL)
    ck = cache_k.reshape(bd, DEPTH, past, KV_W).astype(BF16)
    cv = cache_v.reshape(bd, DEPTH, past, KV_W).astype(BF16)
    for l in range(DEPTH):
        mod_dec = mod[l, 0:bd].reshape(bd, 1, 6 * D_MODEL)
        zx, _, _, _, _ = _trunk_layer(zx, mod_dec, w, l, bd, ld, rope_tabs, ck, cv,
                                      _split_state(state_ssm[:, l]), BF16)

    return (h.reshape(bc, lc, D_MODEL), zx.reshape(bd, ld, D_MODEL), new_cache_k, new_cache_v, new_state)
```

```python
import functools
import math

import jax
import jax.numpy as jnp
import numpy as np
from jax import lax
from jax.experimental import pallas as pl
from jax.experimental.pallas import tpu as pltpu

F32 = jnp.float32
BF16 = jnp.bfloat16

D_MODEL = 1024
DEPTH = 2
GRID_W = 64
HEAD_DIM = 64
N_Q_HEADS = 8
N_KV_HEADS = 2
GQA_GROUP = N_Q_HEADS // N_KV_HEADS
ATTN_W = N_Q_HEADS * HEAD_DIM
KV_W = N_KV_HEADS * HEAD_DIM
ROPE_AXIS_PAIRS = HEAD_DIM // 4
ROPE_THETA = 10000.0
SSM_W = D_MODEL // 2
SSM_GROUP = 16
N_SSM_GROUPS = SSM_W // SSM_GROUP
STATE_P = 64
N_STATES = N_SSM_GROUPS * STATE_P
FOURIER_W = D_MODEL // 2
FOURIER_GROUPS = 4
FOURIER_GW = FOURIER_W // FOURIER_GROUPS
N_BRANCH = 3
GATE_W = N_BRANCH * D_MODEL
D_IN = ATTN_W + 2 * KV_W + SSM_W + FOURIER_W + GATE_W
OFF_K = ATTN_W
OFF_V = ATTN_W + KV_W
OFF_S = ATTN_W + 2 * KV_W
OFF_F = OFF_S + SSM_W
OFF_G = OFF_F + FOURIER_W
D_FF = ((8 * D_MODEL // 3 + 127) // 128) * 128
EPS = 1e-6
ALPHA = (2 * DEPTH) ** 0.25

V7X_VMEM_BYTES = 64 * 1024 * 1024
VMEM_LIMIT_BYTES = V7X_VMEM_BYTES - 8 * 1024 * 1024
LANES = 128
SUBLANES = 8
SSM_BUNDLE_GROUPS = LANES // SSM_GROUP
SSM_BUNDLES = N_SSM_GROUPS // SSM_BUNDLE_GROUPS
BUNDLE_STATES = SSM_BUNDLE_GROUPS * STATE_P

ROW_TILE = 512
MERGE_ROW_BLOCKS = 2
ATTN_Q_ROWS = 512
FFN_ROWS = 1024
FFN_CHUNK = 256
FFN_ROW_BLOCKS = 4
SCAN_ROWS = 512
SCAN_LANES = 512
MOD_COLS = 1536
MOD_ROWS = 16


def _params(*sem):
    return pltpu.CompilerParams(dimension_semantics=sem, vmem_limit_bytes=VMEM_LIMIT_BYTES)


def _dot(a, b):
    return jnp.dot(a, b, preferred_element_type=F32)


def _sigmoid(x):
    return 1.0 / (1.0 + jnp.exp(-x))


def _layer_spec(arr, layer, n_grid):
    zeros = (0,) * (arr.ndim - 1)
    if n_grid == 1:
        return pl.BlockSpec((None,) + arr.shape[1:], lambda i: (layer,) + zeros)
    return pl.BlockSpec((None,) + arr.shape[1:], lambda i, j: (layer,) + zeros)


def _mod_spec(chunk, tiles_per_seq, per_seq):
    if per_seq:
        return pl.BlockSpec((None, 1, D_MODEL), lambda i: (i // tiles_per_seq, 0, chunk))
    return pl.BlockSpec((None, 1, D_MODEL), lambda i: (0, 0, chunk))


def _mod_kernel(c_ref, w_ref, b_ref, o_ref):
    c = c_ref[...]
    s = (c * _sigmoid(c)).astype(BF16)
    o_ref[...] = _dot(s, w_ref[...].astype(BF16)) + b_ref[...]


def _mod_call(cvec, w_ada, b_ada):
    n = 6 * D_MODEL
    return pl.pallas_call(
        _mod_kernel,
        grid=(DEPTH, n // MOD_COLS),
        in_specs=[
            pl.BlockSpec((MOD_ROWS, D_MODEL), lambda l, j: (0, 0)),
            pl.BlockSpec((None, D_MODEL, MOD_COLS), lambda l, j: (l, 0, j)),
            pl.BlockSpec((None, 1, MOD_COLS), lambda l, j: (l, 0, j)),
        ],
        out_specs=pl.BlockSpec((None, MOD_ROWS, MOD_COLS), lambda l, j: (l, 0, j)),
        out_shape=jax.ShapeDtypeStruct((DEPTH, MOD_ROWS, n), F32),
        compiler_params=_params("parallel", "parallel"),
        name="mod",
    )(cvec, w_ada, b_ada.reshape(DEPTH, 1, n))


def _swap_pairs(x):
    w = x.shape[-1]
    nxt = pltpu.roll(x, w - 1, axis=1)
    prv = pltpu.roll(x, 1, axis=1)
    lane = lax.broadcasted_iota(jnp.int32, x.shape, 1)
    return jnp.where((lane & 1) == 0, nxt, prv)


def _weight_slab_spec(layer, col0, width):
    return pl.BlockSpec((pl.Element(1), pl.Element(D_MODEL), pl.Element(width)), lambda i: (layer, 0, col0),
                        pipeline_mode=pl.Buffered(1))


def _inproj_kernel(*refs, rope):
    if rope:
        (x_ref, sc_ref, sh_ref, w_ref, gq_ref, gk_ref, ones_ref, cos_ref, sin_ref,
         q_ref, k_ref, v_ref, s_ref, f_ref, wbf) = refs
    else:
        (x_ref, sc_ref, sh_ref, w_ref, gq_ref, gk_ref, ones_ref,
         q_ref, k_ref, v_ref, s_ref, f_ref, wbf) = refs
    batch, steps, _ = x_ref.shape
    rows = batch * steps

    @pl.when(pl.program_id(0) == 0)
    def _():
        wbf[...] = w_ref[0].astype(BF16)

    u = (x_ref[...] * (1.0 + sc_ref[...]) + sh_ref[...]).reshape(rows, D_MODEL).astype(BF16)

    zq = _dot(u, wbf[:, 0:OFF_K])
    zk = _dot(u, wbf[:, OFF_K:OFF_V])
    zv = _dot(u, wbf[:, OFF_V:OFF_S])
    msq_q = _dot((zq * zq).astype(BF16), ones_ref[...])
    msq_k = _dot((zk * zk).astype(BF16), ones_ref[0:KV_W, 0:KV_W])
    q = zq * lax.rsqrt(msq_q + EPS) * gq_ref[...]
    k = zk * lax.rsqrt(msq_k + EPS) * gk_ref[...]
    if rope:
        cos = jnp.concatenate([cos_ref[...]] * batch, axis=0)
        sin = jnp.concatenate([sin_ref[...]] * batch, axis=0)
        q = q * cos + _swap_pairs(q) * sin
        k = k * cos[:, 0:KV_W] + _swap_pairs(k) * sin[:, 0:KV_W]
    q_ref[...] = (q * (HEAD_DIM ** -0.5)).reshape(batch, steps, ATTN_W).astype(q_ref.dtype)
    k_ref[...] = k.reshape(batch, steps, KV_W).astype(k_ref.dtype)
    v_ref[...] = zv.reshape(batch, steps, KV_W).astype(v_ref.dtype)
    f_ref[...] = _dot(u, wbf[:, OFF_F:OFF_G]).reshape(batch, steps, FOURIER_W).astype(f_ref.dtype)
    s = _dot(u, wbf[:, OFF_S:OFF_F]).reshape(batch, steps, SSM_W)
    s_ref[...] = jnp.swapaxes(s, 0, 1).reshape(rows, SSM_W).astype(s_ref.dtype)


def _inproj_call(x, mod3, w_in, gq, gk, ones_blk, rope_tabs, layer, batch, seq_len, kv_dtype):
    steps = ROW_TILE // batch
    rope = rope_tabs is not None
    x3 = x.reshape(batch, seq_len, D_MODEL)
    seq = lambda w: pl.BlockSpec((batch, steps, w), lambda i: (0, i, 0))
    mod = lambda ch: pl.BlockSpec((mod3.shape[0], 1, D_MODEL), lambda i: (0, 0, ch))
    in_specs = [seq(D_MODEL), mod(1), mod(0), _weight_slab_spec(layer, 0, OFF_G),
                _layer_spec(gq, layer, 1), _layer_spec(gk, layer, 1), pl.BlockSpec(ones_blk.shape, lambda i: (0, 0))]
    args = [x3, mod3, mod3, w_in, gq, gk, ones_blk]
    if rope:
        tab = pl.BlockSpec((steps, ATTN_W), lambda i: (i, 0))
        in_specs += [tab, tab]
        args += list(rope_tabs)
    widths = (ATTN_W, KV_W, KV_W, FOURIER_W)
    dtypes = (BF16, kv_dtype, kv_dtype, BF16)
    outs = pl.pallas_call(
        functools.partial(_inproj_kernel, rope=rope),
        grid=(seq_len // steps,),
        in_specs=in_specs,
        out_specs=[seq(w) for w in widths[:3]] + [pl.BlockSpec((ROW_TILE, SSM_W), lambda i: (i, 0)), seq(FOURIER_W)],
        out_shape=([jax.ShapeDtypeStruct((batch, seq_len, w), d) for w, d in zip(widths[:3], dtypes[:3])]
                   + [jax.ShapeDtypeStruct((seq_len * batch, SSM_W), F32),
                      jax.ShapeDtypeStruct((batch, seq_len, FOURIER_W), BF16)]),
        scratch_shapes=[pltpu.VMEM((D_MODEL, OFF_G), BF16)],
        compiler_params=_params("arbitrary"),
        name="in_proj",
    )(*args)
    q, k, v, s_tm, f = outs
    flat = lambda a: a.reshape(batch * seq_len, a.shape[-1])
    return flat(q), flat(k), flat(v), s_tm, flat(f)


def _attn_kernel(*refs, has_cache):
    if has_cache:
        q_ref, k_ref, v_ref, ck_ref, cv_ref, o_ref = refs
    else:
        q_ref, k_ref, v_ref, o_ref = refs
    q = q_ref[...].astype(F32)
    contract_last = (((1,), (1,)), ((), ()))
    kv = lambda h: slice((h // GQA_GROUP) * HEAD_DIM, (h // GQA_GROUP + 1) * HEAD_DIM)
    scores = {}
    outs = []

    def score(h):
        qh = q[:, h * HEAD_DIM:(h + 1) * HEAD_DIM].astype(BF16)
        s1 = lax.dot_general(qh, k_ref[:, kv(h)].astype(BF16), contract_last, preferred_element_type=F32)
        s2 = None
        if has_cache:
            s2 = lax.dot_general(qh, ck_ref[:, kv(h)], contract_last, preferred_element_type=F32)
        scores[h] = (s1, s2)

    def attend(h):
        s1, s2 = scores.pop(h)
        m = jnp.max(s1, axis=-1, keepdims=True)
        if has_cache:
            m = jnp.maximum(m, jnp.max(s2, axis=-1, keepdims=True))
        p1 = jnp.exp(s1 - m)
        den = jnp.sum(p1, axis=-1, keepdims=True)
        o = _dot(p1.astype(BF16), v_ref[:, kv(h)].astype(BF16))
        if has_cache:
            p2 = jnp.exp(s2 - m)
            den = den + jnp.sum(p2, axis=-1, keepdims=True)
            o = o + _dot(p2.astype(BF16), cv_ref[:, kv(h)])
        outs.append(o * (1.0 / den))

    for h in range(N_Q_HEADS + 1):
        if h < N_Q_HEADS:
            score(h)
        if h >= 1:
            attend(h - 1)
    o_ref[...] = jnp.concatenate(outs, axis=1).astype(o_ref.dtype)


def _attn_call(q, k, v, cache_k, cache_v, layer, batch, seq_len):
    rows = q.shape[0]
    qb = min(ATTN_Q_ROWS, seq_len)
    nq = seq_len // qb
    has_cache = cache_k is not None
    in_specs = [pl.BlockSpec((qb, ATTN_W), lambda b, i: (b * nq + i, 0)),
                pl.BlockSpec((seq_len, KV_W), lambda b, i: (b, 0)),
                pl.BlockSpec((seq_len, KV_W), lambda b, i: (b, 0))]
    args = [q, k, v]
    if has_cache:
        past = cache_k.shape[2]
        in_specs += [pl.BlockSpec((None, None, past, KV_W), lambda b, i: (b, layer, 0, 0))] * 2
        args += [cache_k, cache_v]
    return pl.pallas_call(
        functools.partial(_attn_kernel, has_cache=has_cache),
        grid=(batch, nq),
        in_specs=in_specs,
        out_specs=pl.BlockSpec((qb, ATTN_W), lambda b, i: (b * nq + i, 0)),
        out_shape=jax.ShapeDtypeStruct((rows, ATTN_W), BF16),
        compiler_params=_params("parallel", "parallel"),
        name="attention",
    )(*args)


def _fourier_kernel(f_ref, wc_ref, wl_ref, o_ref):
    f = f_ref[...]
    parts = [_dot(f[:, g * FOURIER_GW:(g + 1) * FOURIER_GW], wc_ref[...]) for g in range(FOURIER_GROUPS)]
    xc = jnp.concatenate([p[:, 0:FOURIER_GW] for p in parts], axis=1)
    xs = jnp.concatenate([p[:, FOURIER_GW:2 * FOURIER_GW] for p in parts], axis=1)
    stacked = jnp.concatenate([xc, xs], axis=0).astype(BF16)
    o_ref[...] = _dot(wl_ref[...], stacked).astype(o_ref.dtype)


def _dft_tables(seq_len):
    c = np.arange(FOURIER_GW)
    ang_c = 2.0 * np.pi * np.outer(c, c) / FOURIER_GW
    wc = np.concatenate([np.cos(ang_c), np.sin(ang_c)], axis=1) / math.sqrt(FOURIER_GW)
    t = np.arange(seq_len)
    ang_l = 2.0 * np.pi * np.outer(t, t) / seq_len
    wl = np.concatenate([np.cos(ang_l), -np.sin(ang_l)], axis=1) / math.sqrt(seq_len)
    return (jnp.asarray(wc, F32).astype(BF16), jnp.asarray(wl, F32).astype(BF16))


def _fourier_call(f, batch, seq_len):
    wc, wl = _dft_tables(seq_len)
    return pl.pallas_call(
        _fourier_kernel,
        grid=(batch,),
        in_specs=[pl.BlockSpec((seq_len, FOURIER_W), lambda b: (b, 0)),
                  pl.BlockSpec(wc.shape, lambda b: (0, 0)),
                  pl.BlockSpec(wl.shape, lambda b: (0, 0))],
        out_specs=pl.BlockSpec((seq_len, FOURIER_W), lambda b: (b, 0)),
        out_shape=jax.ShapeDtypeStruct(f.shape, BF16),
        compiler_params=_params("parallel"),
        name="fourier",
    )(f, wc, wl)


def _block_diagonal(compact, group_rows, group_cols):
    rows, w = compact.shape
    log2 = lambda n: n.bit_length() - 1
    sel = (lax.broadcasted_iota(jnp.int32, (w, SSM_BUNDLE_GROUPS * w), 0)
           == (lax.broadcasted_iota(jnp.int32, (w, SSM_BUNDLE_GROUPS * w), 1) & (w - 1)))
    wide = _dot(compact.astype(BF16), jnp.where(sel, 1.0, 0.0).astype(BF16))
    row_group = (lax.broadcasted_iota(jnp.int32, wide.shape, 0) >> log2(group_rows)) & (SSM_BUNDLE_GROUPS - 1)
    col_group = lax.broadcasted_iota(jnp.int32, wide.shape, 1) >> log2(group_cols)
    return jnp.where(row_group == col_group, wide, 0.0)


def _zoh_lambda(a_re, a_im, log_dt):
    dt = jnp.exp(log_dt)
    mag = jnp.exp(a_re * dt)
    ang = a_im * dt
    return mag * jnp.cos(ang), mag * jnp.sin(ang)


def _zoh_kernel(are_ref, aim_ref, ldt_ref, bre_ref, bim_ref, arec_ref, aimc_ref, ldtc_ref, cre_ref, cim_ref,
                l2re_ref, l2im_ref, bw_ref, cw_ref, k0_ref):
    a_re = are_ref[...]
    a_im = aim_ref[...]
    lb_re, lb_im = _zoh_lambda(a_re, a_im, ldt_ref[...])
    den = a_re * a_re + a_im * a_im
    n_re = lb_re - 1.0
    n_im = lb_im
    f_re = (n_re * a_re + n_im * a_im) / den
    f_im = (n_im * a_re - n_re * a_im) / den
    l2re_ref[...] = lb_re * lb_re - lb_im * lb_im
    l2im_ref[...] = 2.0 * lb_re * lb_im
    b_re = bre_ref[...]
    b_im = bim_ref[...]
    bb_re = f_re[:, None, :] * b_re - f_im[:, None, :] * b_im
    bb_im = f_re[:, None, :] * b_im + f_im[:, None, :] * b_re
    lbb_re = lb_re[:, None, :] * bb_re - lb_im[:, None, :] * bb_im
    lbb_im = lb_re[:, None, :] * bb_im + lb_im[:, None, :] * bb_re
    flat = lambda a: a.reshape(a.shape[0] * a.shape[1], a.shape[2])
    n_slabs = bw_ref.shape[0]
    in_slab = lambda a: _block_diagonal(flat(a), SSM_GROUP, STATE_P).reshape(n_slabs, LANES, BUNDLE_STATES)
    bd_re, bd_im = in_slab(bb_re), in_slab(bb_im)
    bw_ref[:, 0, :, 0:BUNDLE_STATES] = in_slab(lbb_re).astype(BF16)
    bw_ref[:, 0, :, BUNDLE_STATES:] = in_slab(lbb_im).astype(BF16)
    bw_ref[:, 1, :, 0:BUNDLE_STATES] = bd_re.astype(BF16)
    bw_ref[:, 1, :, BUNDLE_STATES:] = bd_im.astype(BF16)
    lc_re, lc_im = _zoh_lambda(arec_ref[...], aimc_ref[...], ldtc_ref[...])
    c_re = cre_ref[...]
    c_im = cim_ref[...]
    cl_re = c_re * lc_re - c_im * lc_im
    cl_im = c_re * lc_im + c_im * lc_re
    out_slab = lambda a: _block_diagonal(flat(a), STATE_P, SSM_GROUP).reshape(n_slabs, BUNDLE_STATES, LANES)
    cd_re, cd_im = out_slab(c_re), out_slab(c_im)
    cw_ref[:, 0, :, 0:LANES] = cd_re.astype(BF16)
    cw_ref[:, 0, :, LANES:] = out_slab(cl_re).astype(BF16)
    cw_ref[:, 1, :, 0:LANES] = (-cd_im).astype(BF16)
    cw_ref[:, 1, :, LANES:] = (-out_slab(cl_im)).astype(BF16)
    for s in range(n_slabs):
        k0 = _dot(bd_re[s].astype(BF16), cd_re[s].astype(BF16)) - _dot(bd_im[s].astype(BF16), cd_im[s].astype(BF16))
        k0_ref[s] = k0.astype(BF16)


def _zoh_call(a_re, a_im, log_dt, b_re, b_im, c_re, c_im):
    rows = DEPTH * 2 * N_SSM_GROUPS
    n_dirs = DEPTH * 2
    are = a_re.reshape(rows, STATE_P)
    aim = a_im.reshape(rows, STATE_P)
    ldt = log_dt.reshape(rows, 1)
    bre = jnp.swapaxes(b_re, -1, -2).reshape(rows, SSM_GROUP, STATE_P)
    bim = jnp.swapaxes(b_im, -1, -2).reshape(rows, SSM_GROUP, STATE_P)
    cre = jnp.swapaxes(c_re, -1, -2).reshape(rows, STATE_P, SSM_GROUP)
    cim = jnp.swapaxes(c_im, -1, -2).reshape(rows, STATE_P, SSM_GROUP)
    args = (are, aim, ldt, bre, bim, are.reshape(rows, STATE_P, 1), aim.reshape(rows, STATE_P, 1),
            ldt.reshape(rows, 1, 1), cre, cim)
    per_dir = lambda shape: pl.BlockSpec((shape[0] // n_dirs,) + tuple(shape[1:]),
                                         lambda i: (i,) + (0,) * (len(shape) - 1))
    n_slabs = n_dirs * SSM_BUNDLES
    outs = [jax.ShapeDtypeStruct((rows, STATE_P), F32)] * 2 + [
        jax.ShapeDtypeStruct((n_slabs, 2, LANES, 2 * BUNDLE_STATES), BF16),
        jax.ShapeDtypeStruct((n_slabs, 2, BUNDLE_STATES, 2 * LANES), BF16),
        jax.ShapeDtypeStruct((n_slabs, LANES, LANES), BF16)]
    l2_re, l2_im, bw, cw, k0 = pl.pallas_call(
        _zoh_kernel,
        grid=(n_dirs,),
        in_specs=[per_dir(a.shape) for a in args],
        out_specs=[per_dir(o.shape) for o in outs],
        out_shape=outs,
        compiler_params=_params("parallel"),
        name="s5_zoh",
    )(*args)
    lead = (DEPTH, 2, SSM_BUNDLES)
    return (l2_re.reshape(DEPTH, 2, 1, N_STATES), l2_im.reshape(DEPTH, 2, 1, N_STATES),
            bw.reshape(*lead, 2 * LANES, 2 * BUNDLE_STATES), cw.reshape(*lead, 2 * BUNDLE_STATES, 2 * LANES),
            k0.reshape(*lead, LANES, LANES))


def _state_slabs(h, k):
    return (h[:, k * BUNDLE_STATES:(k + 1) * BUNDLE_STATES],
            h[:, N_STATES + k * BUNDLE_STATES:N_STATES + (k + 1) * BUNDLE_STATES])


def _scan_project_state(h_re, h_im, cw_ref, k):
    return (_dot(h_re.astype(BF16), cw_ref[k, 0:BUNDLE_STATES, :])
            + _dot(h_im.astype(BF16), cw_ref[k, BUNDLE_STATES:, :]))


def _scan_init(h0_ref, cw_ref, hc, wp):
    hc[...] = h0_ref[...]
    for k in range(SSM_BUNDLES):
        h_re, h_im = _state_slabs(h0_ref, k)
        wp[:, k * LANES:(k + 1) * LANES] = _scan_project_state(h_re, h_im, cw_ref, k)[:, LANES:]


def _scan_direction(u_ref, bw_ref, cw_ref, k0_ref, l2re_ref, l2im_ref, y_ref, hs, hc, wp, *, batch, reverse):
    rows = u_ref.shape[0]
    pairs = rows // (2 * batch)
    half = pairs * batch
    u4 = u_ref[...].reshape(pairs, 2, batch, SSM_W)
    early, late = u4[:, 0].reshape(half, SSM_W), u4[:, 1].reshape(half, SSM_W)
    first, second = (late, early) if reverse else (early, late)
    first, second = first.astype(BF16), second.astype(BF16)

    for k in range(SSM_BUNDLES):
        slab = slice(k * LANES, (k + 1) * LANES)
        r = _dot(jnp.concatenate([first[:, slab], second[:, slab]], axis=1), bw_ref[k])
        hs[:, k * BUNDLE_STATES:(k + 1) * BUNDLE_STATES] = r[:, 0:BUNDLE_STATES]
        hs[:, N_STATES + k * BUNDLE_STATES:N_STATES + (k + 1) * BUNDLE_STATES] = r[:, BUNDLE_STATES:]

    for blk in range(N_STATES // SCAN_LANES):
        re = slice(blk * SCAN_LANES, (blk + 1) * SCAN_LANES)
        im = slice(N_STATES + blk * SCAN_LANES, N_STATES + (blk + 1) * SCAN_LANES)
        lr = jnp.broadcast_to(l2re_ref[:, re], (batch, SCAN_LANES))
        li = jnp.broadcast_to(l2im_ref[:, re], (batch, SCAN_LANES))
        hr, hi = hc[:, re], hc[:, im]
        for j in range(pairs):
            jj = pairs - 1 - j if reverse else j
            prow = slice(jj * batch, (jj + 1) * batch)
            hr, hi = (lr * hr - li * hi + hs[prow, re], lr * hi + li * hr + hs[prow, im])
            hs[prow, re] = hr
            hs[prow, im] = hi
        hc[:, re] = hr
        hc[:, im] = hi

    y_second, w, local = [], [], []
    for k in range(SSM_BUNDLES):
        h_re, h_im = _state_slabs(hs, k)
        z = _scan_project_state(h_re, h_im, cw_ref, k)
        y_second.append(z[:, 0:LANES])
        w.append(z[:, LANES:])
        local.append(_dot(first[:, k * LANES:(k + 1) * LANES], k0_ref[k]))
    y_second = jnp.concatenate(y_second, axis=1)
    w = jnp.concatenate(w, axis=1)
    if reverse:
        w_prev = jnp.concatenate([w[batch:], wp[...]], axis=0)
        wp[...] = w[0:batch]
    else:
        w_prev = jnp.concatenate([wp[...], w[0:half - batch]], axis=0)
        wp[...] = w[half - batch:]
    y_first = w_prev + jnp.concatenate(local, axis=1)
    y_early, y_late = (y_second, y_first) if reverse else (y_first, y_second)
    y = jnp.concatenate([y_early.reshape(pairs, 1, batch, SSM_W), y_late.reshape(pairs, 1, batch, SSM_W)], axis=1)
    y_ref[...] = y.reshape(rows, SSM_W).astype(y_ref.dtype)


def _s5_scan_kernel(uf_ref, ub_ref, bwf_ref, cwf_ref, k0f_ref, l2rf_ref, l2if_ref, bwb_ref, cwb_ref, k0b_ref,
                    l2rb_ref, l2ib_ref, h0f_ref, h0b_ref, yf_ref, yb_ref, hff_ref, hfb_ref,
                    hsf, hsb, hcf, hcb, wpf, wpb, *, batch):
    @pl.when(pl.program_id(0) == 0)
    def _():
        _scan_init(h0f_ref, cwf_ref, hcf, wpf)
        _scan_init(h0b_ref, cwb_ref, hcb, wpb)

    _scan_direction(uf_ref, bwf_ref, cwf_ref, k0f_ref, l2rf_ref, l2if_ref, yf_ref, hsf, hcf, wpf,
                    batch=batch, reverse=False)
    _scan_direction(ub_ref, bwb_ref, cwb_ref, k0b_ref, l2rb_ref, l2ib_ref, yb_ref, hsb, hcb, wpb,
                    batch=batch, reverse=True)
    hff_ref[...] = hcf[...]
    hfb_ref[...] = hcb[...]


def _gelu_tanh(x):
    return 0.5 * x * (1.0 + jnp.tanh(math.sqrt(2.0 / math.pi) * (x + 0.044715 * (x * x * x))))


def _s5_post_kernel(u_ref, yf_ref, yb_ref, d_ref, wg_ref, bg_ref, o_ref):
    batch, steps, _ = o_ref.shape
    y = d_ref[...] * u_ref[...].astype(F32) + yf_ref[...].astype(F32) + yb_ref[...].astype(F32)
    y = _gelu_tanh(y)
    gate = _sigmoid(_dot(y.astype(BF16), wg_ref[...]) + bg_ref[...])
    out = jnp.swapaxes((y * gate).reshape(steps, batch, SSM_W), 0, 1)
    o_ref[...] = out.astype(o_ref.dtype)


def _s5_calls(u_tm, batch, seq_len, layer, s5w, h0, d_skip, w_glu, b_glu):
    lam2_re, lam2_im, bw, cw, k0 = s5w
    rows = u_tm.shape[0]
    steps = SCAN_ROWS // batch
    n_chunks = seq_len // steps
    fwd_chunk = pl.BlockSpec((SCAN_ROWS, SSM_W), lambda i: (i, 0))
    bwd_chunk = pl.BlockSpec((SCAN_ROWS, SSM_W), lambda i: (n_chunks - 1 - i, 0))
    dir_spec = lambda a, d: pl.BlockSpec((None, None) + a.shape[2:], lambda i: (layer, d) + (0,) * (a.ndim - 2))
    weights = lambda d: [dir_spec(a, d) for a in (bw, cw, k0, lam2_re, lam2_im)]
    h0_spec = lambda d: pl.BlockSpec((None,) + h0.shape[1:], lambda i: (d, 0, 0))
    state_spec = pl.BlockSpec((batch, 2 * N_STATES), lambda i: (0, 0))
    state = jax.ShapeDtypeStruct((batch, 2 * N_STATES), F32)
    hs = lambda: pltpu.VMEM((SCAN_ROWS // 2, 2 * N_STATES), F32)
    hc = lambda: pltpu.VMEM((batch, 2 * N_STATES), F32)
    wp = lambda: pltpu.VMEM((batch, SSM_W), F32)

    yf, yb, hfin_f, hfin_b = pl.pallas_call(
        functools.partial(_s5_scan_kernel, batch=batch),
        grid=(n_chunks,),
        in_specs=[fwd_chunk, bwd_chunk] + weights(0) + weights(1) + [h0_spec(0), h0_spec(1)],
        out_specs=[fwd_chunk, bwd_chunk, state_spec, state_spec],
        out_shape=[jax.ShapeDtypeStruct((rows, SSM_W), BF16)] * 2 + [state, state],
        scratch_shapes=[hs(), hs(), hc(), hc(), wp(), wp()],
        compiler_params=_params("arbitrary"),
        name="s5_scan",
    )(u_tm, u_tm, bw, cw, k0, lam2_re, lam2_im, bw, cw, k0, lam2_re, lam2_im, h0, h0)

    out = pl.pallas_call(
        _s5_post_kernel,
        grid=(n_chunks,),
        in_specs=[fwd_chunk, fwd_chunk, fwd_chunk, _layer_spec(d_skip, layer, 1), _layer_spec(w_glu, layer, 1),
                  _layer_spec(b_glu, layer, 1)],
        out_specs=pl.BlockSpec((batch, steps, SSM_W), lambda i: (0, i, 0)),
        out_shape=jax.ShapeDtypeStruct((batch, seq_len, SSM_W), BF16),
        compiler_params=_params("parallel"),
        name="s5_post",
    )(u_tm, yf, yb, d_skip, w_glu, b_glu)
    return out.reshape(rows, SSM_W), hfin_f, hfin_b


def _layer_norm_rows(y, g, b):
    mu = jnp.mean(y, axis=-1, keepdims=True)
    yc = y - mu
    var = jnp.mean(yc * yc, axis=-1, keepdims=True)
    return yc * lax.rsqrt(var + EPS) * g + b


def _merge_kernel(x_ref, a_ref, s_ref, f_ref, sc_ref, sh_ref, g1_ref, wg_ref, wa_ref, ws_ref, wf_ref, wo_ref,
                  lg_ref, lb_ref, o_ref, wgb):
    @pl.when(pl.program_id(0) == 0)
    def _():
        wgb[...] = wg_ref[0].astype(BF16)

    rb = x_ref.shape[0] // MERGE_ROW_BLOCKS
    merged = {}

    def project(i):
        r = slice(i * rb, (i + 1) * rb)
        u = (x_ref[r, :] * (1.0 + sc_ref[...]) + sh_ref[...]).astype(BF16)
        merged[i] = (_sigmoid(_dot(u, wgb[:, 0:D_MODEL])) * _dot(a_ref[r, :], wa_ref[...])
                     + _sigmoid(_dot(u, wgb[:, D_MODEL:2 * D_MODEL])) * _dot(s_ref[r, :], ws_ref[...])
                     + _sigmoid(_dot(u, wgb[:, 2 * D_MODEL:3 * D_MODEL])) * _dot(f_ref[r, :], wf_ref[...])
                     ).astype(BF16)

    def finish(i):
        r = slice(i * rb, (i + 1) * rb)
        y = ALPHA * x_ref[r, :] + g1_ref[...] * _dot(merged.pop(i), wo_ref[...])
        o_ref[r, :] = _layer_norm_rows(y, lg_ref[...], lb_ref[...])

    for i in range(MERGE_ROW_BLOCKS + 1):
        if i < MERGE_ROW_BLOCKS:
            project(i)
        if i >= 1:
            finish(i - 1)


def _merge_call(x, attn, ssm, four, mod3, w_in, w_ba, w_bs, w_bf, w_out, ln_g, ln_b, layer, seq_len):
    rows = x.shape[0]
    tps = seq_len // ROW_TILE
    per_seq = mod3.shape[0] > 1
    row = lambda w: pl.BlockSpec((ROW_TILE, w), lambda i: (i, 0))
    lay = lambda a: _layer_spec(a, layer, 1)
    return pl.pallas_call(
        _merge_kernel,
        grid=(rows // ROW_TILE,),
        in_specs=[row(D_MODEL), row(ATTN_W), row(SSM_W), row(FOURIER_W),
                  _mod_spec(1, tps, per_seq), _mod_spec(0, tps, per_seq), _mod_spec(2, tps, per_seq),
                  _weight_slab_spec(layer, OFF_G, GATE_W), lay(w_ba), lay(w_bs), lay(w_bf), lay(w_out), lay(ln_g),
                  lay(ln_b)],
        out_specs=row(D_MODEL),
        out_shape=jax.ShapeDtypeStruct((rows, D_MODEL), F32),
        scratch_shapes=[pltpu.VMEM((D_MODEL, GATE_W), BF16)],
        compiler_params=_params("arbitrary"),
        name="merge",
    )(x, attn, ssm, four, mod3, mod3, mod3, w_in, w_ba, w_bs, w_bf, w_out, ln_g, ln_b)


def _shift_rows(h, seq_len):
    rows = h.shape[0]
    prev = pltpu.roll(h, 1, axis=0)
    nxt = pltpu.roll(h, rows - 1, axis=0)
    sub = lax.broadcasted_iota(jnp.int32, (SUBLANES, h.shape[1]), 0)
    p_parts, n_parts = [], []
    for s0 in range(0, rows, seq_len):
        e = s0 + seq_len
        p_parts += [jnp.where(sub == 0, 0.0, prev[s0:s0 + SUBLANES]), prev[s0 + SUBLANES:e]]
        n_parts += [nxt[s0:e - SUBLANES], jnp.where(sub == SUBLANES - 1, 0.0, nxt[e - SUBLANES:e])]
    return jnp.concatenate(p_parts, axis=0), jnp.concatenate(n_parts, axis=0)


def _ffn_kernel(x_ref, sc_ref, sh_ref, g2_ref, wa_ref, wb_ref, cwa_ref, cwb_ref, cba_ref, cbb_ref, wd_ref,
                lg_ref, lb_ref, o_ref, u2, acc, *, seq_len):
    c = pl.program_id(1)
    rb = x_ref.shape[0] // FFN_ROW_BLOCKS
    blocks = [slice(i * rb, (i + 1) * rb) for i in range(FFN_ROW_BLOCKS)]

    @pl.when(c == 0)
    def _():
        u2[...] = (x_ref[...] * (1.0 + sc_ref[...]) + sh_ref[...]).astype(BF16)
        acc[...] = jnp.zeros_like(acc)

    wa = wa_ref[...].astype(BF16)
    wb = wb_ref[...].astype(BF16)
    wd = wd_ref[...].astype(BF16)
    ha = jnp.concatenate([_dot(u2[blk, :], wa) for blk in blocks], axis=0)
    hb = jnp.concatenate([_dot(u2[blk, :], wb) for blk in blocks], axis=0)

    def conv(h, cw_ref, cb_ref):
        prev, nxt = _shift_rows(h, seq_len)
        return prev * cw_ref[0:1, :] + h * cw_ref[1:2, :] + nxt * cw_ref[2:3, :] + cb_ref[...]

    a = conv(ha, cwa_ref, cba_ref)
    b = conv(hb, cwb_ref, cbb_ref)
    gated = ((a * _sigmoid(a)) * b).astype(BF16)
    for blk in blocks:
        acc[blk, :] += _dot(gated[blk, :], wd)

    @pl.when(c == pl.num_programs(1) - 1)
    def _():
        y = ALPHA * x_ref[...] + g2_ref[...] * acc[...]
        o_ref[...] = _layer_norm_rows(y, lg_ref[...], lb_ref[...])


def _ffn_call(x, mod3, w_up, conv_w, conv_b, w_down, ln_g, ln_b, layer, seq_len):
    rows = x.shape[0]
    tile = FFN_ROWS
    n_chunks = D_FF // FFN_CHUNK
    seqs_per_tile = tile // seq_len
    per_seq = mod3.shape[0] > 1
    if per_seq:
        mod = lambda ch: pl.BlockSpec((None, 1, D_MODEL), lambda i, c: (i * seqs_per_tile, 0, ch))
    else:
        mod = lambda ch: pl.BlockSpec((None, 1, D_MODEL), lambda i, c: (0, 0, ch))
    rowspec = pl.BlockSpec((tile, D_MODEL), lambda i, c: (i, 0))
    cols = lambda r, off: pl.BlockSpec((None, r, FFN_CHUNK), lambda i, c: (layer, 0, off + c))
    return pl.pallas_call(
        functools.partial(_ffn_kernel, seq_len=seq_len),
        grid=(rows // tile, n_chunks),
        in_specs=[rowspec, mod(4), mod(3), mod(5),
                  cols(D_MODEL, 0), cols(D_MODEL, n_chunks), cols(3, 0), cols(3, n_chunks), cols(1, 0), cols(1, n_chunks),
                  pl.BlockSpec((None, FFN_CHUNK, D_MODEL), lambda i, c: (layer, c, 0)),
                  _layer_spec(ln_g, layer, 2), _layer_spec(ln_b, layer, 2)],
        out_specs=rowspec,
        out_shape=jax.ShapeDtypeStruct((rows, D_MODEL), F32),
        scratch_shapes=[pltpu.VMEM((tile, D_MODEL), BF16), pltpu.VMEM((tile, D_MODEL), F32)],
        compiler_params=_params("parallel", "arbitrary"),
        name="conv_mlp",
    )(x, mod3, mod3, mod3, w_up, w_up, conv_w, conv_w, conv_b, conv_b, w_down, ln_g, ln_b)


def _rope_tables(n_tokens):
    rows = n_tokens // GRID_W
    row = np.repeat(np.arange(rows, dtype=np.float64), GRID_W)
    col = np.tile(np.arange(GRID_W, dtype=np.float64), rows)
    freqs = ROPE_THETA ** (-np.arange(ROPE_AXIS_PAIRS, dtype=np.float64) / ROPE_AXIS_PAIRS)
    ang = np.concatenate([row[:, None] * freqs, col[:, None] * freqs], axis=-1)
    cos = np.repeat(np.cos(ang), 2, axis=-1)
    sin = np.repeat(np.sin(ang), 2, axis=-1) * np.tile(np.array([-1.0, 1.0]), HEAD_DIM // 2)
    return (jnp.asarray(np.tile(cos, (1, N_Q_HEADS)), F32), jnp.asarray(np.tile(sin, (1, N_Q_HEADS)), F32))


def _trunk_layer(x, mod3, w, layer, batch, seq_len, rope_tabs, cache_k, cache_v, h0, kv_dtype):
    q, k, v, s_tm, f_in = _inproj_call(x, mod3, w["w_in"], w["gq"], w["gk"], w["ones"], rope_tabs, layer,
                                       batch, seq_len, kv_dtype)
    attn = _attn_call(q, k, v, cache_k, cache_v, layer, batch, seq_len)
    four = _fourier_call(f_in, batch, seq_len)
    ssm, hfin_f, hfin_b = _s5_calls(s_tm, batch, seq_len, layer, w["s5"], h0, w["d"], w["w_glu"], w["b_glu"])
    x1 = _merge_call(x, attn, ssm, four, mod3, w["w_in"], w["w_ba"], w["w_bs"], w["w_bf"], w["w_out"],
                     w["ln1_g"], w["ln1_b"], layer, seq_len)
    x2 = _ffn_call(x1, mod3, w["w_up"], w["conv_w"], w["conv_b"], w["w_down"], w["ln2_g"], w["ln2_b"], layer, seq_len)
    return x2, k, v, hfin_f, hfin_b


def _split_state(h):
    b = h.shape[0]
    re = h[..., 0].reshape(b, 2, N_STATES)
    im = h[..., 1].reshape(b, 2, N_STATES)
    return jnp.concatenate([re, im], axis=-1).transpose(1, 0, 2)


def _join_state(hf, hb):
    def one(h):
        b = h.shape[0]
        re = h[:, 0:N_STATES].reshape(b, N_SSM_GROUPS, STATE_P)
        im = h[:, N_STATES:].reshape(b, N_SSM_GROUPS, STATE_P)
        return jnp.stack([re, im], axis=-1)
    return jnp.stack([one(hf), one(hb)], axis=1)


def kernel(x_prompt, x_sample, c, cache_k, cache_v, state_ssm, c_ctx, w_ada, b_ada, w_in, q_norm_g, k_norm_g, ssm_a_re, ssm_a_im, ssm_log_dt, ssm_b_re, ssm_b_im, ssm_c_re, ssm_c_im, ssm_d, w_glu, b_glu, w_br_attn, w_br_ssm, w_br_four, w_out, ln1_g, ln1_b, w_up, conv_w, conv_b, w_down, ln2_g, ln2_b):
    bc, lc, _ = x_prompt.shape
    bd, ld, _ = x_sample.shape
    past = cache_k.shape[2]

    cvec = jnp.zeros((MOD_ROWS, D_MODEL), F32).at[0:bd].set(c).at[bd].set(c_ctx)
    mod = _mod_call(cvec, w_ada, b_ada)

    s5w = _zoh_call(ssm_a_re, ssm_a_im, ssm_log_dt, ssm_b_re, ssm_b_im, ssm_c_re, ssm_c_im)

    head_mean = np.kron(np.eye(N_Q_HEADS), np.full((HEAD_DIM, HEAD_DIM), 1.0 / HEAD_DIM))
    rope_tabs = _rope_tables(ld)
    vec = lambda a: a.reshape(DEPTH, 1, -1)
    w = dict(
        w_in=w_in, gq=vec(jnp.tile(q_norm_g, (1, N_Q_HEADS))), gk=vec(jnp.tile(k_norm_g, (1, N_KV_HEADS))),
        ones=jnp.asarray(head_mean, F32).astype(BF16), s5=s5w, d=vec(ssm_d),
        w_glu=w_glu.astype(BF16), b_glu=vec(b_glu), w_ba=w_br_attn.astype(BF16), w_bs=w_br_ssm.astype(BF16),
        w_bf=w_br_four.astype(BF16), w_out=w_out.astype(BF16), ln1_g=vec(ln1_g), ln1_b=vec(ln1_b),
        w_up=w_up, conv_w=conv_w, conv_b=vec(conv_b), w_down=w_down, ln2_g=vec(ln2_g), ln2_b=vec(ln2_b))

    h = x_prompt.reshape(bc * lc, D_MODEL)
    h0_ctx = jnp.zeros((2, bc, 2 * N_STATES), F32)
    ks, vs, ss = [], [], []
    for l in range(DEPTH):
        mod_ctx = mod[l, bd:bd + 1].reshape(1, 1, 6 * D_MODEL)
        h, k_l, v_l, hf, hb = _trunk_layer(h, mod_ctx, w, l, bc, lc, None, None, None, h0_ctx, F32)
        ks.append(k_l.reshape(bc, lc, N_KV_HEADS, HEAD_DIM))
        vs.append(v_l.reshape(bc, lc, N_KV_HEADS, HEAD_DIM))
        ss.append(_join_state(hf, hb))
    new_cache_k = jnp.stack(ks, axis=1)
    new_cache_v = jnp.stack(vs, axis=1)
    new_state = jnp.stack(ss, axis=1)

    zx = x_sample.reshape(bd * ld, D_MODEL)
    ck = cache_k.reshape(bd, DEPTH, past, KV_W).astype(BF16)
    cv = cache_v.reshape(bd, DEPTH, past, KV_W).astype(BF16)
    for l in range(DEPTH):
        mod_dec = mod[l, 0:bd].reshape(bd, 1, 6 * D_MODEL)
        zx, _, _, _, _ = _trunk_layer(zx, mod_dec, w, l, bd, ld, rope_tabs, ck, cv,
                                      _split_state(state_ssm[:, l]), BF16)

    return (h.reshape(bc, lc, D_MODEL), zx.reshape(bd, ld, D_MODEL), new_cache_k, new_cache_v, new_state)
```

```python
import functools
import math

import jax
import jax.numpy as jnp
import numpy as np
from jax import lax
from jax.experimental import pallas as pl
from jax.experimental.pallas import tpu as pltpu

F32 = jnp.float32
BF16 = jnp.bfloat16

D_MODEL = 1024
DEPTH = 2
GRID_W = 64
HEAD_DIM = 64
N_Q_HEADS = 8
N_KV_HEADS = 2
GQA_GROUP = N_Q_HEADS // N_KV_HEADS
ATTN_W = N_Q_HEADS * HEAD_DIM
KV_W = N_KV_HEADS * HEAD_DIM
ROPE_AXIS_PAIRS = HEAD_DIM // 4
ROPE_THETA = 10000.0
SSM_W = D_MODEL // 2
SSM_GROUP = 16
N_SSM_GROUPS = SSM_W // SSM_GROUP
STATE_P = 64
N_STATES = N_SSM_GROUPS * STATE_P
FOURIER_W = D_MODEL // 2
FOURIER_GROUPS = 4
FOURIER_GW = FOURIER_W // FOURIER_GROUPS
N_BRANCH = 3
GATE_W = N_BRANCH * D_MODEL
D_IN = ATTN_W + 2 * KV_W + SSM_W + FOURIER_W + GATE_W
OFF_K = ATTN_W
OFF_V = ATTN_W + KV_W
OFF_S = ATTN_W + 2 * KV_W
OFF_F = OFF_S + SSM_W
OFF_G = OFF_F + FOURIER_W
D_FF = ((8 * D_MODEL // 3 + 127) // 128) * 128
EPS = 1e-6
ALPHA = (2 * DEPTH) ** 0.25

V7X_VMEM_BYTES = 64 * 1024 * 1024
VMEM_LIMIT_BYTES = V7X_VMEM_BYTES - 8 * 1024 * 1024
LANES = 128
SUBLANES = 8
SSM_BUNDLE_GROUPS = LANES // SSM_GROUP
SSM_BUNDLES = N_SSM_GROUPS // SSM_BUNDLE_GROUPS
BUNDLE_STATES = SSM_BUNDLE_GROUPS * STATE_P

ROW_TILE = 512
MERGE_ROW_BLOCKS = 2
ATTN_Q_ROWS = 512
FFN_ROWS = 256
FFN_CHUNK = 256
SCAN_ROWS = 512
SCAN_LANES = 512
MOD_COLS = 1536
MOD_ROWS = 16


def _params(*sem):
    return pltpu.CompilerParams(dimension_semantics=sem, vmem_limit_bytes=VMEM_LIMIT_BYTES)


def _dot(a, b):
    return jnp.dot(a, b, preferred_element_type=F32)


def _sigmoid(x):
    return 1.0 / (1.0 + jnp.exp(-x))


def _layer_spec(arr, layer, n_grid):
    zeros = (0,) * (arr.ndim - 1)
    if n_grid == 1:
        return pl.BlockSpec((None,) + arr.shape[1:], lambda i: (layer,) + zeros)
    return pl.BlockSpec((None,) + arr.shape[1:], lambda i, j: (layer,) + zeros)


def _mod_spec(chunk, tiles_per_seq, per_seq):
    if per_seq:
        return pl.BlockSpec((None, 1, D_MODEL), lambda i: (i // tiles_per_seq, 0, chunk))
    return pl.BlockSpec((None, 1, D_MODEL), lambda i: (0, 0, chunk))


def _mod_kernel(c_ref, w_ref, b_ref, o_ref):
    c = c_ref[...]
    s = (c * _sigmoid(c)).astype(BF16)
    o_ref[...] = _dot(s, w_ref[...].astype(BF16)) + b_ref[...]


def _mod_call(cvec, w_ada, b_ada):
    n = 6 * D_MODEL
    return pl.pallas_call(
        _mod_kernel,
        grid=(DEPTH, n // MOD_COLS),
        in_specs=[
            pl.BlockSpec((MOD_ROWS, D_MODEL), lambda l, j: (0, 0)),
            pl.BlockSpec((None, D_MODEL, MOD_COLS), lambda l, j: (l, 0, j)),
            pl.BlockSpec((None, 1, MOD_COLS), lambda l, j: (l, 0, j)),
        ],
        out_specs=pl.BlockSpec((None, MOD_ROWS, MOD_COLS), lambda l, j: (l, 0, j)),
        out_shape=jax.ShapeDtypeStruct((DEPTH, MOD_ROWS, n), F32),
        compiler_params=_params("parallel", "parallel"),
        name="mod",
    )(cvec, w_ada, b_ada.reshape(DEPTH, 1, n))


def _swap_pairs(x):
    w = x.shape[-1]
    nxt = pltpu.roll(x, w - 1, axis=1)
    prv = pltpu.roll(x, 1, axis=1)
    lane = lax.broadcasted_iota(jnp.int32, x.shape, 1)
    return jnp.where((lane & 1) == 0, nxt, prv)


def _weight_slab_spec(layer, col0, width):
    return pl.BlockSpec((pl.Element(1), pl.Element(D_MODEL), pl.Element(width)), lambda i: (layer, 0, col0),
                        pipeline_mode=pl.Buffered(1))


def _inproj_kernel(*refs, rope):
    if rope:
        (x_ref, sc_ref, sh_ref, w_ref, gq_ref, gk_ref, ones_ref, cos_ref, sin_ref,
         q_ref, k_ref, v_ref, s_ref, f_ref, wbf) = refs
    else:
        (x_ref, sc_ref, sh_ref, w_ref, gq_ref, gk_ref, ones_ref,
         q_ref, k_ref, v_ref, s_ref, f_ref, wbf) = refs
    batch, steps, _ = x_ref.shape
    rows = batch * steps

    @pl.when(pl.program_id(0) == 0)
    def _():
        wbf[...] = w_ref[0].astype(BF16)

    u = (x_ref[...] * (1.0 + sc_ref[...]) + sh_ref[...]).reshape(rows, D_MODEL).astype(BF16)

    zq = _dot(u, wbf[:, 0:OFF_K])
    zk = _dot(u, wbf[:, OFF_K:OFF_V])
    zv = _dot(u, wbf[:, OFF_V:OFF_S])
    msq_q = _dot((zq * zq).astype(BF16), ones_ref[...])
    msq_k = _dot((zk * zk).astype(BF16), ones_ref[0:KV_W, 0:KV_W])
    q = zq * lax.rsqrt(msq_q + EPS) * gq_ref[...]
    k = zk * lax.rsqrt(msq_k + EPS) * gk_ref[...]
    if rope:
        cos = jnp.concatenate([cos_ref[...]] * batch, axis=0)
        sin = jnp.concatenate([sin_ref[...]] * batch, axis=0)
        q = q * cos + _swap_pairs(q) * sin
        k = k * cos[:, 0:KV_W] + _swap_pairs(k) * sin[:, 0:KV_W]
    q_ref[...] = (q * (HEAD_DIM ** -0.5)).reshape(batch, steps, ATTN_W).astype(q_ref.dtype)
    k_ref[...] = k.reshape(batch, steps, KV_W).astype(k_ref.dtype)
    v_ref[...] = zv.reshape(batch, steps, KV_W).astype(v_ref.dtype)
    f_ref[...] = _dot(u, wbf[:, OFF_F:OFF_G]).reshape(batch, steps, FOURIER_W).astype(f_ref.dtype)
    s = _dot(u, wbf[:, OFF_S:OFF_F]).reshape(batch, steps, SSM_W)
    s_ref[...] = jnp.swapaxes(s, 0, 1).reshape(rows, SSM_W).astype(s_ref.dtype)


def _inproj_call(x, mod3, w_in, gq, gk, ones_blk, rope_tabs, layer, batch, seq_len, kv_dtype):
    steps = ROW_TILE // batch
    rope = rope_tabs is not None
    x3 = x.reshape(batch, seq_len, D_MODEL)
    seq = lambda w: pl.BlockSpec((batch, steps, w), lambda i: (0, i, 0))
    mod = lambda ch: pl.BlockSpec((mod3.shape[0], 1, D_MODEL), lambda i: (0, 0, ch))
    in_specs = [seq(D_MODEL), mod(1), mod(0), _weight_slab_spec(layer, 0, OFF_G),
                _layer_spec(gq, layer, 1), _layer_spec(gk, layer, 1), pl.BlockSpec(ones_blk.shape, lambda i: (0, 0))]
    args = [x3, mod3, mod3, w_in, gq, gk, ones_blk]
    if rope:
        tab = pl.BlockSpec((steps, ATTN_W), lambda i: (i, 0))
        in_specs += [tab, tab]
        args += list(rope_tabs)
    widths = (ATTN_W, KV_W, KV_W, FOURIER_W)
    dtypes = (BF16, kv_dtype, kv_dtype, BF16)
    outs = pl.pallas_call(
        functools.partial(_inproj_kernel, rope=rope),
        grid=(seq_len // steps,),
        in_specs=in_specs,
        out_specs=[seq(w) for w in widths[:3]] + [pl.BlockSpec((ROW_TILE, SSM_W), lambda i: (i, 0)), seq(FOURIER_W)],
        out_shape=([jax.ShapeDtypeStruct((batch, seq_len, w), d) for w, d in zip(widths[:3], dtypes[:3])]
                   + [jax.ShapeDtypeStruct((seq_len * batch, SSM_W), F32),
                      jax.ShapeDtypeStruct((batch, seq_len, FOURIER_W), BF16)]),
        scratch_shapes=[pltpu.VMEM((D_MODEL, OFF_G), BF16)],
        compiler_params=_params("arbitrary"),
        name="in_proj",
    )(*args)
    q, k, v, s_tm, f = outs
    flat = lambda a: a.reshape(batch * seq_len, a.shape[-1])
    return flat(q), flat(k), flat(v), s_tm, flat(f)


def _attn_kernel(*refs, has_cache):
    if has_cache:
        q_ref, k_ref, v_ref, ck_ref, cv_ref, o_ref = refs
    else:
        q_ref, k_ref, v_ref, o_ref = refs
    q = q_ref[...].astype(F32)
    contract_last = (((1,), (1,)), ((), ()))
    kv = lambda h: slice((h // GQA_GROUP) * HEAD_DIM, (h // GQA_GROUP + 1) * HEAD_DIM)
    scores = {}
    outs = []

    def score(h):
        qh = q[:, h * HEAD_DIM:(h + 1) * HEAD_DIM].astype(BF16)
        s1 = lax.dot_general(qh, k_ref[:, kv(h)].astype(BF16), contract_last, preferred_element_type=F32)
        s2 = None
        if has_cache:
            s2 = lax.dot_general(qh, ck_ref[:, kv(h)], contract_last, preferred_element_type=F32)
        scores[h] = (s1, s2)

    def attend(h):
        s1, s2 = scores.pop(h)
        m = jnp.max(s1, axis=-1, keepdims=True)
        if has_cache:
            m = jnp.maximum(m, jnp.max(s2, axis=-1, keepdims=True))
        p1 = jnp.exp(s1 - m)
        den = jnp.sum(p1, axis=-1, keepdims=True)
        o = _dot(p1.astype(BF16), v_ref[:, kv(h)].astype(BF16))
        if has_cache:
            p2 = jnp.exp(s2 - m)
            den = den + jnp.sum(p2, axis=-1, keepdims=True)
            o = o + _dot(p2.astype(BF16), cv_ref[:, kv(h)])
        outs.append(o * (1.0 / den))

    for h in range(N_Q_HEADS + 1):
        if h < N_Q_HEADS:
            score(h)
        if h >= 1:
            attend(h - 1)
    o_ref[...] = jnp.concatenate(outs, axis=1).astype(o_ref.dtype)


def _attn_call(q, k, v, cache_k, cache_v, layer, batch, seq_len):
    rows = q.shape[0]
    qb = min(ATTN_Q_ROWS, seq_len)
    nq = seq_len // qb
    has_cache = cache_k is not None
    in_specs = [pl.BlockSpec((qb, ATTN_W), lambda b, i: (b * nq + i, 0)),
                pl.BlockSpec((seq_len, KV_W), lambda b, i: (b, 0)),
                pl.BlockSpec((seq_len, KV_W), lambda b, i: (b, 0))]
    args = [q, k, v]
    if has_cache:
        past = cache_k.shape[2]
        in_specs += [pl.BlockSpec((None, None, past, KV_W), lambda b, i: (b, layer, 0, 0))] * 2
        args += [cache_k, cache_v]
    return pl.pallas_call(
        functools.partial(_attn_kernel, has_cache=has_cache),
        grid=(batch, nq),
        in_specs=in_specs,
        out_specs=pl.BlockSpec((qb, ATTN_W), lambda b, i: (b * nq + i, 0)),
        out_shape=jax.ShapeDtypeStruct((rows, ATTN_W), BF16),
        compiler_params=_params("parallel", "parallel"),
        name="attention",
    )(*args)


def _fourier_kernel(f_ref, wc_ref, wl_ref, o_ref):
    f = f_ref[...]
    parts = [_dot(f[:, g * FOURIER_GW:(g + 1) * FOURIER_GW], wc_ref[...]) for g in range(FOURIER_GROUPS)]
    xc = jnp.concatenate([p[:, 0:FOURIER_GW] for p in parts], axis=1)
    xs = jnp.concatenate([p[:, FOURIER_GW:2 * FOURIER_GW] for p in parts], axis=1)
    stacked = jnp.concatenate([xc, xs], axis=0).astype(BF16)
    o_ref[...] = _dot(wl_ref[...], stacked).astype(o_ref.dtype)


def _dft_tables(seq_len):
    c = np.arange(FOURIER_GW)
    ang_c = 2.0 * np.pi * np.outer(c, c) / FOURIER_GW
    wc = np.concatenate([np.cos(ang_c), np.sin(ang_c)], axis=1) / math.sqrt(FOURIER_GW)
    t = np.arange(seq_len)
    ang_l = 2.0 * np.pi * np.outer(t, t) / seq_len
    wl = np.concatenate([np.cos(ang_l), -np.sin(ang_l)], axis=1) / math.sqrt(seq_len)
    return (jnp.asarray(wc, F32).astype(BF16), jnp.asarray(wl, F32).astype(BF16))


def _fourier_call(f, batch, seq_len):
    wc, wl = _dft_tables(seq_len)
    return pl.pallas_call(
        _fourier_kernel,
        grid=(batch,),
        in_specs=[pl.BlockSpec((seq_len, FOURIER_W), lambda b: (b, 0)),
                  pl.BlockSpec(wc.shape, lambda b: (0, 0)),
                  pl.BlockSpec(wl.shape, lambda b: (0, 0))],
        out_specs=pl.BlockSpec((seq_len, FOURIER_W), lambda b: (b, 0)),
        out_shape=jax.ShapeDtypeStruct(f.shape, BF16),
        compiler_params=_params("parallel"),
        name="fourier",
    )(f, wc, wl)


def _block_diagonal(compact, group_rows, group_cols):
    rows, w = compact.shape
    log2 = lambda n: n.bit_length() - 1
    sel = (lax.broadcasted_iota(jnp.int32, (w, SSM_BUNDLE_GROUPS * w), 0)
           == (lax.broadcasted_iota(jnp.int32, (w, SSM_BUNDLE_GROUPS * w), 1) & (w - 1)))
    wide = _dot(compact.astype(BF16), jnp.where(sel, 1.0, 0.0).astype(BF16))
    row_group = (lax.broadcasted_iota(jnp.int32, wide.shape, 0) >> log2(group_rows)) & (SSM_BUNDLE_GROUPS - 1)
    col_group = lax.broadcasted_iota(jnp.int32, wide.shape, 1) >> log2(group_cols)
    return jnp.where(row_group == col_group, wide, 0.0)


def _zoh_lambda(a_re, a_im, log_dt):
    dt = jnp.exp(log_dt)
    mag = jnp.exp(a_re * dt)
    ang = a_im * dt
    return mag * jnp.cos(ang), mag * jnp.sin(ang)


def _zoh_kernel(are_ref, aim_ref, ldt_ref, bre_ref, bim_ref, arec_ref, aimc_ref, ldtc_ref, cre_ref, cim_ref,
                l2re_ref, l2im_ref, bw_ref, cw_ref, k0_ref):
    a_re = are_ref[...]
    a_im = aim_ref[...]
    lb_re, lb_im = _zoh_lambda(a_re, a_im, ldt_ref[...])
    den = a_re * a_re + a_im * a_im
    n_re = lb_re - 1.0
    n_im = lb_im
    f_re = (n_re * a_re + n_im * a_im) / den
    f_im = (n_im * a_re - n_re * a_im) / den
    l2re_ref[...] = lb_re * lb_re - lb_im * lb_im
    l2im_ref[...] = 2.0 * lb_re * lb_im
    b_re = bre_ref[...]
    b_im = bim_ref[...]
    bb_re = f_re[:, None, :] * b_re - f_im[:, None, :] * b_im
    bb_im = f_re[:, None, :] * b_im + f_im[:, None, :] * b_re
    lbb_re = lb_re[:, None, :] * bb_re - lb_im[:, None, :] * bb_im
    lbb_im = lb_re[:, None, :] * bb_im + lb_im[:, None, :] * bb_re
    flat = lambda a: a.reshape(a.shape[0] * a.shape[1], a.shape[2])
    n_slabs = bw_ref.shape[0]
    in_slab = lambda a: _block_diagonal(flat(a), SSM_GROUP, STATE_P).reshape(n_slabs, LANES, BUNDLE_STATES)
    bd_re, bd_im = in_slab(bb_re), in_slab(bb_im)
    bw_ref[:, 0, :, 0:BUNDLE_STATES] = in_slab(lbb_re).astype(BF16)
    bw_ref[:, 0, :, BUNDLE_STATES:] = in_slab(lbb_im).astype(BF16)
    bw_ref[:, 1, :, 0:BUNDLE_STATES] = bd_re.astype(BF16)
    bw_ref[:, 1, :, BUNDLE_STATES:] = bd_im.astype(BF16)
    lc_re, lc_im = _zoh_lambda(arec_ref[...], aimc_ref[...], ldtc_ref[...])
    c_re = cre_ref[...]
    c_im = cim_ref[...]
    cl_re = c_re * lc_re - c_im * lc_im
    cl_im = c_re * lc_im + c_im * lc_re
    out_slab = lambda a: _block_diagonal(flat(a), STATE_P, SSM_GROUP).reshape(n_slabs, BUNDLE_STATES, LANES)
    cd_re, cd_im = out_slab(c_re), out_slab(c_im)
    cw_ref[:, 0, :, 0:LANES] = cd_re.astype(BF16)
    cw_ref[:, 0, :, LANES:] = out_slab(cl_re).astype(BF16)
    cw_ref[:, 1, :, 0:LANES] = (-cd_im).astype(BF16)
    cw_ref[:, 1, :, LANES:] = (-out_slab(cl_im)).astype(BF16)
    for s in range(n_slabs):
        k0 = _dot(bd_re[s].astype(BF16), cd_re[s].astype(BF16)) - _dot(bd_im[s].astype(BF16), cd_im[s].astype(BF16))
        k0_ref[s] = k0.astype(BF16)


def _zoh_call(a_re, a_im, log_dt, b_re, b_im, c_re, c_im):
    rows = DEPTH * 2 * N_SSM_GROUPS
    n_dirs = DEPTH * 2
    are = a_re.reshape(rows, STATE_P)
    aim = a_im.reshape(rows, STATE_P)
    ldt = log_dt.reshape(rows, 1)
    bre = jnp.swapaxes(b_re, -1, -2).reshape(rows, SSM_GROUP, STATE_P)
    bim = jnp.swapaxes(b_im, -1, -2).reshape(rows, SSM_GROUP, STATE_P)
    cre = jnp.swapaxes(c_re, -1, -2).reshape(rows, STATE_P, SSM_GROUP)
    cim = jnp.swapaxes(c_im, -1, -2).reshape(rows, STATE_P, SSM_GROUP)
    args = (are, aim, ldt, bre, bim, are.reshape(rows, STATE_P, 1), aim.reshape(rows, STATE_P, 1),
            ldt.reshape(rows, 1, 1), cre, cim)
    per_dir = lambda shape: pl.BlockSpec((shape[0] // n_dirs,) + tuple(shape[1:]),
                                         lambda i: (i,) + (0,) * (len(shape) - 1))
    n_slabs = n_dirs * SSM_BUNDLES
    outs = [jax.ShapeDtypeStruct((rows, STATE_P), F32)] * 2 + [
        jax.ShapeDtypeStruct((n_slabs, 2, LANES, 2 * BUNDLE_STATES), BF16),
        jax.ShapeDtypeStruct((n_slabs, 2, BUNDLE_STATES, 2 * LANES), BF16),
        jax.ShapeDtypeStruct((n_slabs, LANES, LANES), BF16)]
    l2_re, l2_im, bw, cw, k0 = pl.pallas_call(
        _zoh_kernel,
        grid=(n_dirs,),
        in_specs=[per_dir(a.shape) for a in args],
        out_specs=[per_dir(o.shape) for o in outs],
        out_shape=outs,
        compiler_params=_params("parallel"),
        name="s5_zoh",
    )(*args)
    lead = (DEPTH, 2, SSM_BUNDLES)
    return (l2_re.reshape(DEPTH, 2, 1, N_STATES), l2_im.reshape(DEPTH, 2, 1, N_STATES),
            bw.reshape(*lead, 2 * LANES, 2 * BUNDLE_STATES), cw.reshape(*lead, 2 * BUNDLE_STATES, 2 * LANES),
            k0.reshape(*lead, LANES, LANES))


def _state_slabs(h, k):
    return (h[:, k * BUNDLE_STATES:(k + 1) * BUNDLE_STATES],
            h[:, N_STATES + k * BUNDLE_STATES:N_STATES + (k + 1) * BUNDLE_STATES])


def _scan_project_state(h_re, h_im, cw_ref, k):
    return (_dot(h_re.astype(BF16), cw_ref[k, 0:BUNDLE_STATES, :])
            + _dot(h_im.astype(BF16), cw_ref[k, BUNDLE_STATES:, :]))


def _scan_init(h0_ref, cw_ref, hc, wp):
    hc[...] = h0_ref[...]
    for k in range(SSM_BUNDLES):
        h_re, h_im = _state_slabs(h0_ref, k)
        wp[:, k * LANES:(k + 1) * LANES] = _scan_project_state(h_re, h_im, cw_ref, k)[:, LANES:]


def _scan_direction(u_ref, bw_ref, cw_ref, k0_ref, l2re_ref, l2im_ref, y_ref, hs, hc, wp, *, batch, reverse):
    rows = u_ref.shape[0]
    pairs = rows // (2 * batch)
    half = pairs * batch
    u4 = u_ref[...].reshape(pairs, 2, batch, SSM_W)
    early, late = u4[:, 0].reshape(half, SSM_W), u4[:, 1].reshape(half, SSM_W)
    first, second = (late, early) if reverse else (early, late)
    first, second = first.astype(BF16), second.astype(BF16)

    for k in range(SSM_BUNDLES):
        slab = slice(k * LANES, (k + 1) * LANES)
        r = _dot(jnp.concatenate([first[:, slab], second[:, slab]], axis=1), bw_ref[k])
        hs[:, k * BUNDLE_STATES:(k + 1) * BUNDLE_STATES] = r[:, 0:BUNDLE_STATES]
        hs[:, N_STATES + k * BUNDLE_STATES:N_STATES + (k + 1) * BUNDLE_STATES] = r[:, BUNDLE_STATES:]

    for blk in range(N_STATES // SCAN_LANES):
        re = slice(blk * SCAN_LANES, (blk + 1) * SCAN_LANES)
        im = slice(N_STATES + blk * SCAN_LANES, N_STATES + (blk + 1) * SCAN_LANES)
        lr = jnp.broadcast_to(l2re_ref[:, re], (batch, SCAN_LANES))
        li = jnp.broadcast_to(l2im_ref[:, re], (batch, SCAN_LANES))
        hr, hi = hc[:, re], hc[:, im]
        for j in range(pairs):
            jj = pairs - 1 - j if reverse else j
            prow = slice(jj * batch, (jj + 1) * batch)
            hr, hi = (lr * hr - li * hi + hs[prow, re], lr * hi + li * hr + hs[prow, im])
            hs[prow, re] = hr
            hs[prow, im] = hi
        hc[:, re] = hr
        hc[:, im] = hi

    y_second, w, local = [], [], []
    for k in range(SSM_BUNDLES):
        h_re, h_im = _state_slabs(hs, k)
        z = _scan_project_state(h_re, h_im, cw_ref, k)
        y_second.append(z[:, 0:LANES])
        w.append(z[:, LANES:])
        local.append(_dot(first[:, k * LANES:(k + 1) * LANES], k0_ref[k]))
    y_second = jnp.concatenate(y_second, axis=1)
    w = jnp.concatenate(w, axis=1)
    if reverse:
        w_prev = jnp.concatenate([w[batch:], wp[...]], axis=0)
        wp[...] = w[0:batch]
    else:
        w_prev = jnp.concatenate([wp[...], w[0:half - batch]], axis=0)
        wp[...] = w[half - batch:]
    y_first = w_prev + jnp.concatenate(local, axis=1)
    y_early, y_late = (y_second, y_first) if reverse else (y_first, y_second)
    y = jnp.concatenate([y_early.reshape(pairs, 1, batch, SSM_W), y_late.reshape(pairs, 1, batch, SSM_W)], axis=1)
    y_ref[...] = y.reshape(rows, SSM_W).astype(y_ref.dtype)


def _s5_scan_kernel(uf_ref, ub_ref, bwf_ref, cwf_ref, k0f_ref, l2rf_ref, l2if_ref, bwb_ref, cwb_ref, k0b_ref,
                    l2rb_ref, l2ib_ref, h0f_ref, h0b_ref, yf_ref, yb_ref, hff_ref, hfb_ref,
                    hsf, hsb, hcf, hcb, wpf, wpb, *, batch):
    @pl.when(pl.program_id(0) == 0)
    def _():
        _scan_init(h0f_ref, cwf_ref, hcf, wpf)
        _scan_init(h0b_ref, cwb_ref, hcb, wpb)

    _scan_direction(uf_ref, bwf_ref, cwf_ref, k0f_ref, l2rf_ref, l2if_ref, yf_ref, hsf, hcf, wpf,
                    batch=batch, reverse=False)
    _scan_direction(ub_ref, bwb_ref, cwb_ref, k0b_ref, l2rb_ref, l2ib_ref, yb_ref, hsb, hcb, wpb,
                    batch=batch, reverse=True)
    hff_ref[...] = hcf[...]
    hfb_ref[...] = hcb[...]


def _gelu_tanh(x):
    return 0.5 * x * (1.0 + jnp.tanh(math.sqrt(2.0 / math.pi) * (x + 0.044715 * (x * x * x))))


def _s5_post_kernel(u_ref, yf_ref, yb_ref, d_ref, wg_ref, bg_ref, o_ref):
    batch, steps, _ = o_ref.shape
    y = d_ref[...] * u_ref[...].astype(F32) + yf_ref[...].astype(F32) + yb_ref[...].astype(F32)
    y = _gelu_tanh(y)
    gate = _sigmoid(_dot(y.astype(BF16), wg_ref[...]) + bg_ref[...])
    out = jnp.swapaxes((y * gate).reshape(steps, batch, SSM_W), 0, 1)
    o_ref[...] = out.astype(o_ref.dtype)


def _s5_calls(u_tm, batch, seq_len, layer, s5w, h0, d_skip, w_glu, b_glu):
    lam2_re, lam2_im, bw, cw, k0 = s5w
    rows = u_tm.shape[0]
    steps = SCAN_ROWS // batch
    n_chunks = seq_len // steps
    fwd_chunk = pl.BlockSpec((SCAN_ROWS, SSM_W), lambda i: (i, 0))
    bwd_chunk = pl.BlockSpec((SCAN_ROWS, SSM_W), lambda i: (n_chunks - 1 - i, 0))
    dir_spec = lambda a, d: pl.BlockSpec((None, None) + a.shape[2:], lambda i: (layer, d) + (0,) * (a.ndim - 2))
    weights = lambda d: [dir_spec(a, d) for a in (bw, cw, k0, lam2_re, lam2_im)]
    h0_spec = lambda d: pl.BlockSpec((None,) + h0.shape[1:], lambda i: (d, 0, 0))
    state_spec = pl.BlockSpec((batch, 2 * N_STATES), lambda i: (0, 0))
    state = jax.ShapeDtypeStruct((batch, 2 * N_STATES), F32)
    hs = lambda: pltpu.VMEM((SCAN_ROWS // 2, 2 * N_STATES), F32)
    hc = lambda: pltpu.VMEM((batch, 2 * N_STATES), F32)
    wp = lambda: pltpu.VMEM((batch, SSM_W), F32)

    yf, yb, hfin_f, hfin_b = pl.pallas_call(
        functools.partial(_s5_scan_kernel, batch=batch),
        grid=(n_chunks,),
        in_specs=[fwd_chunk, bwd_chunk] + weights(0) + weights(1) + [h0_spec(0), h0_spec(1)],
        out_specs=[fwd_chunk, bwd_chunk, state_spec, state_spec],
        out_shape=[jax.ShapeDtypeStruct((rows, SSM_W), BF16)] * 2 + [state, state],
        scratch_shapes=[hs(), hs(), hc(), hc(), wp(), wp()],
        compiler_params=_params("arbitrary"),
        name="s5_scan",
    )(u_tm, u_tm, bw, cw, k0, lam2_re, lam2_im, bw, cw, k0, lam2_re, lam2_im, h0, h0)

    out = pl.pallas_call(
        _s5_post_kernel,
        grid=(n_chunks,),
        in_specs=[fwd_chunk, fwd_chunk, fwd_chunk, _layer_spec(d_skip, layer, 1), _layer_spec(w_glu, layer, 1),
                  _layer_spec(b_glu, layer, 1)],
        out_specs=pl.BlockSpec((batch, steps, SSM_W), lambda i: (0, i, 0)),
        out_shape=jax.ShapeDtypeStruct((batch, seq_len, SSM_W), BF16),
        compiler_params=_params("parallel"),
        name="s5_post",
    )(u_tm, yf, yb, d_skip, w_glu, b_glu)
    return out.reshape(rows, SSM_W), hfin_f, hfin_b


def _layer_norm_rows(y, g, b):
    mu = jnp.mean(y, axis=-1, keepdims=True)
    yc = y - mu
    var = jnp.mean(yc * yc, axis=-1, keepdims=True)
    return yc * lax.rsqrt(var + EPS) * g + b


def _merge_kernel(x_ref, a_ref, s_ref, f_ref, sc_ref, sh_ref, g1_ref, wg_ref, wa_ref, ws_ref, wf_ref, wo_ref,
                  lg_ref, lb_ref, o_ref, wgb):
    @pl.when(pl.program_id(0) == 0)
    def _():
        wgb[...] = wg_ref[0].astype(BF16)

    rb = x_ref.shape[0] // MERGE_ROW_BLOCKS
    merged = {}

    def project(i):
        r = slice(i * rb, (i + 1) * rb)
        u = (x_ref[r, :] * (1.0 + sc_ref[...]) + sh_ref[...]).astype(BF16)
        merged[i] = (_sigmoid(_dot(u, wgb[:, 0:D_MODEL])) * _dot(a_ref[r, :], wa_ref[...])
                     + _sigmoid(_dot(u, wgb[:, D_MODEL:2 * D_MODEL])) * _dot(s_ref[r, :], ws_ref[...])
                     + _sigmoid(_dot(u, wgb[:, 2 * D_MODEL:3 * D_MODEL])) * _dot(f_ref[r, :], wf_ref[...])
                     ).astype(BF16)

    def finish(i):
        r = slice(i * rb, (i + 1) * rb)
        y = ALPHA * x_ref[r, :] + g1_ref[...] * _dot(merged.pop(i), wo_ref[...])
        o_ref[r, :] = _layer_norm_rows(y, lg_ref[...], lb_ref[...])

    for i in range(MERGE_ROW_BLOCKS + 1):
        if i < MERGE_ROW_BLOCKS:
            project(i)
        if i >= 1:
            finish(i - 1)


def _merge_call(x, attn, ssm, four, mod3, w_in, w_ba, w_bs, w_bf, w_out, ln_g, ln_b, layer, seq_len):
    rows = x.shape[0]
    tps = seq_len // ROW_TILE
    per_seq = mod3.shape[0] > 1
    row = lambda w: pl.BlockSpec((ROW_TILE, w), lambda i: (i, 0))
    lay = lambda a: _layer_spec(a, layer, 1)
    return pl.pallas_call(
        _merge_kernel,
        grid=(rows // ROW_TILE,),
        in_specs=[row(D_MODEL), row(ATTN_W), row(SSM_W), row(FOURIER_W),
                  _mod_spec(1, tps, per_seq), _mod_spec(0, tps, per_seq), _mod_spec(2, tps, per_seq),
                  _weight_slab_spec(layer, OFF_G, GATE_W), lay(w_ba), lay(w_bs), lay(w_bf), lay(w_out), lay(ln_g),
                  lay(ln_b)],
        out_specs=row(D_MODEL),
        out_shape=jax.ShapeDtypeStruct((rows, D_MODEL), F32),
        scratch_shapes=[pltpu.VMEM((D_MODEL, GATE_W), BF16)],
        compiler_params=_params("arbitrary"),
        name="merge",
    )(x, attn, ssm, four, mod3, mod3, mod3, w_in, w_ba, w_bs, w_bf, w_out, ln_g, ln_b)


def _ffn_kernel(xp_ref, x_ref, xn_ref, sc_ref, sh_ref, g2_ref, wu_ref, cw_ref, cb_ref, wd_ref, lg_ref, lb_ref, o_ref,
                *, tiles_per_seq):
    pos = pl.program_id(0) % tiles_per_seq
    keep_prev = jnp.where(pos == 0, 0.0, 1.0)
    keep_next = jnp.where(pos == tiles_per_seq - 1, 0.0, 1.0)
    mod = lambda v: v * (1.0 + sc_ref[...]) + sh_ref[...]
    x = x_ref[...]
    u = jnp.concatenate([mod(xp_ref[...]) * keep_prev, mod(x), mod(xn_ref[...]) * keep_next], axis=0).astype(BF16)
    rows = u.shape[0]
    tile = x.shape[0]
    inner = slice(SUBLANES, SUBLANES + tile)
    n_chunks = D_FF // FFN_CHUNK

    def up(c):
        ca = slice(c * FFN_CHUNK, (c + 1) * FFN_CHUNK)
        cb = slice(D_FF + c * FFN_CHUNK, D_FF + (c + 1) * FFN_CHUNK)
        return _dot(u, wu_ref[:, ca]), _dot(u, wu_ref[:, cb])

    def conv(h, col):
        prev = pltpu.roll(h, 1, axis=0)[inner]
        nxt = pltpu.roll(h, rows - 1, axis=0)[inner]
        return prev * cw_ref[0:1, col] + h[inner] * cw_ref[1:2, col] + nxt * cw_ref[2:3, col] + cb_ref[:, col]

    acc = jnp.zeros((tile, D_MODEL), F32)
    ha, hb = up(0)
    for c in range(n_chunks):
        if c + 1 < n_chunks:
            ha_next, hb_next = up(c + 1)
        ca = slice(c * FFN_CHUNK, (c + 1) * FFN_CHUNK)
        a = conv(ha, ca)
        b = conv(hb, slice(D_FF + c * FFN_CHUNK, D_FF + (c + 1) * FFN_CHUNK))
        gated = ((a * _sigmoid(a)) * b).astype(BF16)
        acc = acc + _dot(gated, wd_ref[ca, :])
        if c + 1 < n_chunks:
            ha, hb = ha_next, hb_next
    y = ALPHA * x + g2_ref[...] * acc
    o_ref[...] = _layer_norm_rows(y, lg_ref[...], lb_ref[...])


def _ffn_call(x, mod3, w_up, conv_w, conv_b, w_down, ln_g, ln_b, layer, seq_len):
    rows = x.shape[0]
    tile = FFN_ROWS
    tps = seq_len // tile
    halo_per_tile = tile // SUBLANES
    n_halo = rows // SUBLANES
    per_seq = mod3.shape[0] > 1
    mod = lambda ch: pl.BlockSpec((None, 1, D_MODEL), lambda i: ((i // tps) if per_seq else 0, 0, ch))
    res = lambda a: pl.BlockSpec((None,) + a.shape[1:], lambda i: (layer,) + (0,) * (a.ndim - 1),
                                 pipeline_mode=pl.Buffered(1))
    return pl.pallas_call(
        functools.partial(_ffn_kernel, tiles_per_seq=tps),
        grid=(rows // tile,),
        in_specs=[pl.BlockSpec((SUBLANES, D_MODEL), lambda i: (jnp.maximum(i * halo_per_tile - 1, 0), 0)),
                  pl.BlockSpec((tile, D_MODEL), lambda i: (i, 0)),
                  pl.BlockSpec((SUBLANES, D_MODEL), lambda i: (jnp.minimum((i + 1) * halo_per_tile, n_halo - 1), 0)),
                  mod(4), mod(3), mod(5), res(w_up), res(conv_w), res(conv_b), res(w_down), res(ln_g), res(ln_b)],
        out_specs=pl.BlockSpec((tile, D_MODEL), lambda i: (i, 0)),
        out_shape=jax.ShapeDtypeStruct((rows, D_MODEL), F32),
        compiler_params=_params("parallel"),
        name="conv_mlp",
    )(x, x, x, mod3, mod3, mod3, w_up, conv_w, conv_b, w_down, ln_g, ln_b)


def _rope_tables(n_tokens):
    rows = n_tokens // GRID_W
    row = np.repeat(np.arange(rows, dtype=np.float64), GRID_W)
    col = np.tile(np.arange(GRID_W, dtype=np.float64), rows)
    freqs = ROPE_THETA ** (-np.arange(ROPE_AXIS_PAIRS, dtype=np.float64) / ROPE_AXIS_PAIRS)
    ang = np.concatenate([row[:, None] * freqs, col[:, None] * freqs], axis=-1)
    cos = np.repeat(np.cos(ang), 2, axis=-1)
    sin = np.repeat(np.sin(ang), 2, axis=-1) * np.tile(np.array([-1.0, 1.0]), HEAD_DIM // 2)
    return (jnp.asarray(np.tile(cos, (1, N_Q_HEADS)), F32), jnp.asarray(np.tile(sin, (1, N_Q_HEADS)), F32))


def _trunk_layer(x, mod3, w, layer, batch, seq_len, rope_tabs, cache_k, cache_v, h0, kv_dtype):
    q, k, v, s_tm, f_in = _inproj_call(x, mod3, w["w_in"], w["gq"], w["gk"], w["ones"], rope_tabs, layer,
                                       batch, seq_len, kv_dtype)
    attn = _attn_call(q, k, v, cache_k, cache_v, layer, batch, seq_len)
    four = _fourier_call(f_in, batch, seq_len)
    ssm, hfin_f, hfin_b = _s5_calls(s_tm, batch, seq_len, layer, w["s5"], h0, w["d"], w["w_glu"], w["b_glu"])
    x1 = _merge_call(x, attn, ssm, four, mod3, w["w_in"], w["w_ba"], w["w_bs"], w["w_bf"], w["w_out"],
                     w["ln1_g"], w["ln1_b"], layer, seq_len)
    x2 = _ffn_call(x1, mod3, w["w_up"], w["conv_w"], w["conv_b"], w["w_down"], w["ln2_g"], w["ln2_b"], layer, seq_len)
    return x2, k, v, hfin_f, hfin_b


def _split_state(h):
    b = h.shape[0]
    re = h[..., 0].reshape(b, 2, N_STATES)
    im = h[..., 1].reshape(b, 2, N_STATES)
    return jnp.concatenate([re, im], axis=-1).transpose(1, 0, 2)


def _join_state(hf, hb):
    def one(h):
        b = h.shape[0]
        re = h[:, 0:N_STATES].reshape(b, N_SSM_GROUPS, STATE_P)
        im = h[:, N_STATES:].reshape(b, N_SSM_GROUPS, STATE_P)
        return jnp.stack([re, im], axis=-1)
    return jnp.stack([one(hf), one(hb)], axis=1)


def kernel(x_prompt, x_sample, c, cache_k, cache_v, state_ssm, c_ctx, w_ada, b_ada, w_in, q_norm_g, k_norm_g, ssm_a_re, ssm_a_im, ssm_log_dt, ssm_b_re, ssm_b_im, ssm_c_re, ssm_c_im, ssm_d, w_glu, b_glu, w_br_attn, w_br_ssm, w_br_four, w_out, ln1_g, ln1_b, w_up, conv_w, conv_b, w_down, ln2_g, ln2_b):
    bc, lc, _ = x_prompt.shape
    bd, ld, _ = x_sample.shape
    past = cache_k.shape[2]

    cvec = jnp.zeros((MOD_ROWS, D_MODEL), F32).at[0:bd].set(c).at[bd].set(c_ctx)
    mod = _mod_call(cvec, w_ada, b_ada)

    s5w = _zoh_call(ssm_a_re, ssm_a_im, ssm_log_dt, ssm_b_re, ssm_b_im, ssm_c_re, ssm_c_im)

    head_mean = np.kron(np.eye(N_Q_HEADS), np.full((HEAD_DIM, HEAD_DIM), 1.0 / HEAD_DIM))
    rope_tabs = _rope_tables(ld)
    vec = lambda a: a.reshape(DEPTH, 1, -1)
    w = dict(
        w_in=w_in, gq=vec(jnp.tile(q_norm_g, (1, N_Q_HEADS))), gk=vec(jnp.tile(k_norm_g, (1, N_KV_HEADS))),
        ones=jnp.asarray(head_mean, F32).astype(BF16), s5=s5w, d=vec(ssm_d),
        w_glu=w_glu.astype(BF16), b_glu=vec(b_glu), w_ba=w_br_attn.astype(BF16), w_bs=w_br_ssm.astype(BF16),
        w_bf=w_br_four.astype(BF16), w_out=w_out.astype(BF16), ln1_g=vec(ln1_g), ln1_b=vec(ln1_b),
        w_up=w_up.astype(BF16), conv_w=conv_w, conv_b=vec(conv_b), w_down=w_down.astype(BF16), ln2_g=vec(ln2_g),
        ln2_b=vec(ln2_b))

    h = x_prompt.reshape(bc * lc, D_MODEL)
    h0_ctx = jnp.zeros((2, bc, 2 * N_STATES), F32)
    ks, vs, ss = [], [], []
    for l in range(DEPTH):
        mod_ctx = mod[l, bd:bd + 1].reshape(1, 1, 6 * D_MODEL)
        h, k_l, v_l, hf, hb = _trunk_layer(h, mod_ctx, w, l, bc, lc, None, None, None, h0_ctx, F32)
        ks.append(k_l.reshape(bc, lc, N_KV_HEADS, HEAD_DIM))
        vs.append(v_l.reshape(bc, lc, N_KV_HEADS, HEAD_DIM))
        ss.append(_join_state(hf, hb))
    new_cache_k = jnp.stack(ks, axis=1)
    new_cache_v = jnp.stack(vs, axis=1)
    new_state = jnp.stack(ss, axis=1)

    zx = x_sample.reshape(bd * ld, D_MODEL)
    ck = cache_k.reshape(bd, DEPTH, past, KV_W).astype(BF16)
    cv = cache_v.reshape(bd, DEPTH, past, KV_W).astype(BF16)
    for l in range(DEPTH):
        mod_dec = mod[l, 0:bd].reshape(bd, 1, 6 * D_MODEL)
        zx, _, _, _, _ = _trunk_layer(zx, mod_dec, w, l, bd, ld, rope_tabs, ck, cv,
                                      _split_state(state_ssm[:, l]), BF16)

    return (h.reshape(bc, lc, D_MODEL), zx.reshape(bd, ld, D_MODEL), new_cache_k, new_cache_v, new_state)
```

```python
import functools
import math

import jax
import jax.numpy as jnp
import numpy as np
from jax import lax
from jax.experimental import pallas as pl
from jax.experimental.pallas import tpu as pltpu

F32 = jnp.float32
BF16 = jnp.bfloat16

D_MODEL = 1024
DEPTH = 2
GRID_W = 64
HEAD_DIM = 64
N_Q_HEADS = 8
N_KV_HEADS = 2
GQA_GROUP = N_Q_HEADS // N_KV_HEADS
ATTN_W = N_Q_HEADS * HEAD_DIM
KV_W = N_KV_HEADS * HEAD_DIM
ROPE_AXIS_PAIRS = HEAD_DIM // 4
ROPE_THETA = 10000.0
SSM_W = D_MODEL // 2
SSM_GROUP = 16
N_SSM_GROUPS = SSM_W // SSM_GROUP
STATE_P = 64
N_STATES = N_SSM_GROUPS * STATE_P
FOURIER_W = D_MODEL // 2
FOURIER_GROUPS = 4
FOURIER_GW = FOURIER_W // FOURIER_GROUPS
N_BRANCH = 3
GATE_W = N_BRANCH * D_MODEL
D_IN = ATTN_W + 2 * KV_W + SSM_W + FOURIER_W + GATE_W
OFF_K = ATTN_W
OFF_V = ATTN_W + KV_W
OFF_S = ATTN_W + 2 * KV_W
OFF_F = OFF_S + SSM_W
OFF_G = OFF_F + FOURIER_W
D_FF = ((8 * D_MODEL // 3 + 127) // 128) * 128
EPS = 1e-6
ALPHA = (2 * DEPTH) ** 0.25

V7X_VMEM_BYTES = 64 * 1024 * 1024
VMEM_LIMIT_BYTES = V7X_VMEM_BYTES - 8 * 1024 * 1024
LANES = 128
SUBLANES = 8
SSM_BUNDLE_GROUPS = LANES // SSM_GROUP
SSM_BUNDLES = N_SSM_GROUPS // SSM_BUNDLE_GROUPS
BUNDLE_STATES = SSM_BUNDLE_GROUPS * STATE_P

ROW_TILE = 512
MERGE_ROW_BLOCKS = 2
ATTN_Q_ROWS = 512
FFN_ROWS = 256
FFN_CHUNK = 256
FFN_LEAD = 2
SCAN_ROWS = 512
SCAN_LANES = 512
MOD_COLS = 1536
MOD_ROWS = 16


def _params(*sem):
    return pltpu.CompilerParams(dimension_semantics=sem, vmem_limit_bytes=VMEM_LIMIT_BYTES)


def _dot(a, b):
    return jnp.dot(a, b, preferred_element_type=F32)


def _sigmoid(x):
    return 1.0 / (1.0 + jnp.exp(-x))


def _layer_spec(arr, layer, n_grid):
    zeros = (0,) * (arr.ndim - 1)
    if n_grid == 1:
        return pl.BlockSpec((None,) + arr.shape[1:], lambda i: (layer,) + zeros)
    return pl.BlockSpec((None,) + arr.shape[1:], lambda i, j: (layer,) + zeros)


def _mod_spec(chunk, tiles_per_seq, per_seq):
    if per_seq:
        return pl.BlockSpec((None, 1, D_MODEL), lambda i: (i // tiles_per_seq, 0, chunk))
    return pl.BlockSpec((None, 1, D_MODEL), lambda i: (0, 0, chunk))


def _mod_kernel(c_ref, w_ref, b_ref, o_ref):
    c = c_ref[...]
    s = (c * _sigmoid(c)).astype(BF16)
    o_ref[...] = _dot(s, w_ref[...].astype(BF16)) + b_ref[...]


def _mod_call(cvec, w_ada, b_ada):
    n = 6 * D_MODEL
    return pl.pallas_call(
        _mod_kernel,
        grid=(DEPTH, n // MOD_COLS),
        in_specs=[
            pl.BlockSpec((MOD_ROWS, D_MODEL), lambda l, j: (0, 0)),
            pl.BlockSpec((None, D_MODEL, MOD_COLS), lambda l, j: (l, 0, j)),
            pl.BlockSpec((None, 1, MOD_COLS), lambda l, j: (l, 0, j)),
        ],
        out_specs=pl.BlockSpec((None, MOD_ROWS, MOD_COLS), lambda l, j: (l, 0, j)),
        out_shape=jax.ShapeDtypeStruct((DEPTH, MOD_ROWS, n), F32),
        compiler_params=_params("parallel", "parallel"),
        name="mod",
    )(cvec, w_ada, b_ada.reshape(DEPTH, 1, n))


def _swap_pairs(x):
    w = x.shape[-1]
    nxt = pltpu.roll(x, w - 1, axis=1)
    prv = pltpu.roll(x, 1, axis=1)
    lane = lax.broadcasted_iota(jnp.int32, x.shape, 1)
    return jnp.where((lane & 1) == 0, nxt, prv)


def _weight_slab_spec(layer, col0, width):
    return pl.BlockSpec((pl.Element(1), pl.Element(D_MODEL), pl.Element(width)), lambda i: (layer, 0, col0),
                        pipeline_mode=pl.Buffered(1))


def _inproj_kernel(*refs, rope):
    if rope:
        (x_ref, sc_ref, sh_ref, w_ref, gq_ref, gk_ref, ones_ref, cos_ref, sin_ref,
         q_ref, k_ref, v_ref, s_ref, f_ref, wbf) = refs
    else:
        (x_ref, sc_ref, sh_ref, w_ref, gq_ref, gk_ref, ones_ref,
         q_ref, k_ref, v_ref, s_ref, f_ref, wbf) = refs
    batch, steps, _ = x_ref.shape
    rows = batch * steps

    @pl.when(pl.program_id(0) == 0)
    def _():
        wbf[...] = w_ref[0].astype(BF16)

    u = (x_ref[...] * (1.0 + sc_ref[...]) + sh_ref[...]).reshape(rows, D_MODEL).astype(BF16)

    zq = _dot(u, wbf[:, 0:OFF_K])
    zk = _dot(u, wbf[:, OFF_K:OFF_V])
    zv = _dot(u, wbf[:, OFF_V:OFF_S])
    msq_q = _dot((zq * zq).astype(BF16), ones_ref[...])
    msq_k = _dot((zk * zk).astype(BF16), ones_ref[0:KV_W, 0:KV_W])
    q = zq * lax.rsqrt(msq_q + EPS) * gq_ref[...]
    k = zk * lax.rsqrt(msq_k + EPS) * gk_ref[...]
    if rope:
        cos = jnp.concatenate([cos_ref[...]] * batch, axis=0)
        sin = jnp.concatenate([sin_ref[...]] * batch, axis=0)
        q = q * cos + _swap_pairs(q) * sin
        k = k * cos[:, 0:KV_W] + _swap_pairs(k) * sin[:, 0:KV_W]
    q_ref[...] = (q * (HEAD_DIM ** -0.5)).reshape(batch, steps, ATTN_W).astype(q_ref.dtype)
    k_ref[...] = k.reshape(batch, steps, KV_W).astype(k_ref.dtype)
    v_ref[...] = zv.reshape(batch, steps, KV_W).astype(v_ref.dtype)
    f_ref[...] = _dot(u, wbf[:, OFF_F:OFF_G]).reshape(batch, steps, FOURIER_W).astype(f_ref.dtype)
    s = _dot(u, wbf[:, OFF_S:OFF_F]).reshape(batch, steps, SSM_W)
    s_ref[...] = jnp.swapaxes(s, 0, 1).reshape(rows, SSM_W).astype(s_ref.dtype)


def _inproj_call(x, mod3, w_in, gq, gk, ones_blk, rope_tabs, layer, batch, seq_len, kv_dtype):
    steps = ROW_TILE // batch
    rope = rope_tabs is not None
    x3 = x.reshape(batch, seq_len, D_MODEL)
    seq = lambda w: pl.BlockSpec((batch, steps, w), lambda i: (0, i, 0))
    mod = lambda ch: pl.BlockSpec((mod3.shape[0], 1, D_MODEL), lambda i: (0, 0, ch))
    in_specs = [seq(D_MODEL), mod(1), mod(0), _weight_slab_spec(layer, 0, OFF_G),
                _layer_spec(gq, layer, 1), _layer_spec(gk, layer, 1), pl.BlockSpec(ones_blk.shape, lambda i: (0, 0))]
    args = [x3, mod3, mod3, w_in, gq, gk, ones_blk]
    if rope:
        tab = pl.BlockSpec((steps, ATTN_W), lambda i: (i, 0))
        in_specs += [tab, tab]
        args += list(rope_tabs)
    widths = (ATTN_W, KV_W, KV_W, FOURIER_W)
    dtypes = (BF16, kv_dtype, kv_dtype, BF16)
    outs = pl.pallas_call(
        functools.partial(_inproj_kernel, rope=rope),
        grid=(seq_len // steps,),
        in_specs=in_specs,
        out_specs=[seq(w) for w in widths[:3]] + [pl.BlockSpec((ROW_TILE, SSM_W), lambda i: (i, 0)), seq(FOURIER_W)],
        out_shape=([jax.ShapeDtypeStruct((batch, seq_len, w), d) for w, d in zip(widths[:3], dtypes[:3])]
                   + [jax.ShapeDtypeStruct((seq_len * batch, SSM_W), F32),
                      jax.ShapeDtypeStruct((batch, seq_len, FOURIER_W), BF16)]),
        scratch_shapes=[pltpu.VMEM((D_MODEL, OFF_G), BF16)],
        compiler_params=_params("arbitrary"),
        name="in_proj",
    )(*args)
    q, k, v, s_tm, f = outs
    flat = lambda a: a.reshape(batch * seq_len, a.shape[-1])
    return flat(q), flat(k), flat(v), s_tm, flat(f)


def _attn_kernel(*refs, has_cache):
    if has_cache:
        q_ref, k_ref, v_ref, ck_ref, cv_ref, o_ref = refs
    else:
        q_ref, k_ref, v_ref, o_ref = refs
    q = q_ref[...].astype(F32)
    contract_last = (((1,), (1,)), ((), ()))
    kv = lambda h: slice((h // GQA_GROUP) * HEAD_DIM, (h // GQA_GROUP + 1) * HEAD_DIM)
    scores = {}
    outs = []

    def score(h):
        qh = q[:, h * HEAD_DIM:(h + 1) * HEAD_DIM].astype(BF16)
        s1 = lax.dot_general(qh, k_ref[:, kv(h)].astype(BF16), contract_last, preferred_element_type=F32)
        s2 = None
        if has_cache:
            s2 = lax.dot_general(qh, ck_ref[:, kv(h)], contract_last, preferred_element_type=F32)
        scores[h] = (s1, s2)

    def attend(h):
        s1, s2 = scores.pop(h)
        m = jnp.max(s1, axis=-1, keepdims=True)
        if has_cache:
            m = jnp.maximum(m, jnp.max(s2, axis=-1, keepdims=True))
        p1 = jnp.exp(s1 - m)
        den = jnp.sum(p1, axis=-1, keepdims=True)
        o = _dot(p1.astype(BF16), v_ref[:, kv(h)].astype(BF16))
        if has_cache:
            p2 = jnp.exp(s2 - m)
            den = den + jnp.sum(p2, axis=-1, keepdims=True)
            o = o + _dot(p2.astype(BF16), cv_ref[:, kv(h)])
        outs.append(o * (1.0 / den))

    for h in range(N_Q_HEADS + 1):
        if h < N_Q_HEADS:
            score(h)
        if h >= 1:
            attend(h - 1)
    o_ref[...] = jnp.concatenate(outs, axis=1).astype(o_ref.dtype)


def _attn_call(q, k, v, cache_k, cache_v, layer, batch, seq_len):
    rows = q.shape[0]
    qb = min(ATTN_Q_ROWS, seq_len)
    nq = seq_len // qb
    has_cache = cache_k is not None
    in_specs = [pl.BlockSpec((qb, ATTN_W), lambda b, i: (b * nq + i, 0)),
                pl.BlockSpec((seq_len, KV_W), lambda b, i: (b, 0)),
                pl.BlockSpec((seq_len, KV_W), lambda b, i: (b, 0))]
    args = [q, k, v]
    if has_cache:
        past = cache_k.shape[2]
        in_specs += [pl.BlockSpec((None, None, past, KV_W), lambda b, i: (b, layer, 0, 0))] * 2
        args += [cache_k, cache_v]
    return pl.pallas_call(
        functools.partial(_attn_kernel, has_cache=has_cache),
        grid=(batch, nq),
        in_specs=in_specs,
        out_specs=pl.BlockSpec((qb, ATTN_W), lambda b, i: (b * nq + i, 0)),
        out_shape=jax.ShapeDtypeStruct((rows, ATTN_W), BF16),
        compiler_params=_params("parallel", "parallel"),
        name="attention",
    )(*args)


def _fourier_kernel(f_ref, wc_ref, wl_ref, o_ref):
    f = f_ref[...]
    parts = [_dot(f[:, g * FOURIER_GW:(g + 1) * FOURIER_GW], wc_ref[...]) for g in range(FOURIER_GROUPS)]
    xc = jnp.concatenate([p[:, 0:FOURIER_GW] for p in parts], axis=1)
    xs = jnp.concatenate([p[:, FOURIER_GW:2 * FOURIER_GW] for p in parts], axis=1)
    stacked = jnp.concatenate([xc, xs], axis=0).astype(BF16)
    o_ref[...] = _dot(wl_ref[...], stacked).astype(o_ref.dtype)


def _dft_tables(seq_len):
    c = np.arange(FOURIER_GW)
    ang_c = 2.0 * np.pi * np.outer(c, c) / FOURIER_GW
    wc = np.concatenate([np.cos(ang_c), np.sin(ang_c)], axis=1) / math.sqrt(FOURIER_GW)
    t = np.arange(seq_len)
    ang_l = 2.0 * np.pi * np.outer(t, t) / seq_len
    wl = np.concatenate([np.cos(ang_l), -np.sin(ang_l)], axis=1) / math.sqrt(seq_len)
    return (jnp.asarray(wc, F32).astype(BF16), jnp.asarray(wl, F32).astype(BF16))


def _fourier_call(f, batch, seq_len):
    wc, wl = _dft_tables(seq_len)
    return pl.pallas_call(
        _fourier_kernel,
        grid=(batch,),
        in_specs=[pl.BlockSpec((seq_len, FOURIER_W), lambda b: (b, 0)),
                  pl.BlockSpec(wc.shape, lambda b: (0, 0)),
                  pl.BlockSpec(wl.shape, lambda b: (0, 0))],
        out_specs=pl.BlockSpec((seq_len, FOURIER_W), lambda b: (b, 0)),
        out_shape=jax.ShapeDtypeStruct(f.shape, BF16),
        compiler_params=_params("parallel"),
        name="fourier",
    )(f, wc, wl)


def _block_diagonal(compact, group_rows, group_cols):
    rows, w = compact.shape
    log2 = lambda n: n.bit_length() - 1
    sel = (lax.broadcasted_iota(jnp.int32, (w, SSM_BUNDLE_GROUPS * w), 0)
           == (lax.broadcasted_iota(jnp.int32, (w, SSM_BUNDLE_GROUPS * w), 1) & (w - 1)))
    wide = _dot(compact.astype(BF16), jnp.where(sel, 1.0, 0.0).astype(BF16))
    row_group = (lax.broadcasted_iota(jnp.int32, wide.shape, 0) >> log2(group_rows)) & (SSM_BUNDLE_GROUPS - 1)
    col_group = lax.broadcasted_iota(jnp.int32, wide.shape, 1) >> log2(group_cols)
    return jnp.where(row_group == col_group, wide, 0.0)


def _zoh_lambda(a_re, a_im, log_dt):
    dt = jnp.exp(log_dt)
    mag = jnp.exp(a_re * dt)
    ang = a_im * dt
    return mag * jnp.cos(ang), mag * jnp.sin(ang)


def _zoh_kernel(are_ref, aim_ref, ldt_ref, bre_ref, bim_ref, cre_ref, cim_ref,
                l2re_ref, l2im_ref, bw_ref, cw_ref, k0_ref):
    a_re = are_ref[...]
    a_im = aim_ref[...]
    lb_re, lb_im = _zoh_lambda(a_re, a_im, ldt_ref[...])
    den = a_re * a_re + a_im * a_im
    n_re = lb_re - 1.0
    n_im = lb_im
    f_re = (n_re * a_re + n_im * a_im) / den
    f_im = (n_im * a_re - n_re * a_im) / den
    l2re_ref[...] = lb_re * lb_re - lb_im * lb_im
    l2im_ref[...] = 2.0 * lb_re * lb_im
    b_re = bre_ref[...]
    b_im = bim_ref[...]
    bb_re = f_re[:, None, :] * b_re - f_im[:, None, :] * b_im
    bb_im = f_re[:, None, :] * b_im + f_im[:, None, :] * b_re
    lbb_re = lb_re[:, None, :] * bb_re - lb_im[:, None, :] * bb_im
    lbb_im = lb_re[:, None, :] * bb_im + lb_im[:, None, :] * bb_re
    flat = lambda a: a.reshape(a.shape[0] * a.shape[1], a.shape[2])
    n_slabs = bw_ref.shape[0]
    in_slab = lambda a: _block_diagonal(flat(a), SSM_GROUP, STATE_P).reshape(n_slabs, LANES, BUNDLE_STATES)
    bd_re, bd_im = in_slab(bb_re), in_slab(bb_im)
    bw_ref[:, 0, :, 0:BUNDLE_STATES] = in_slab(lbb_re).astype(BF16)
    bw_ref[:, 0, :, BUNDLE_STATES:] = in_slab(lbb_im).astype(BF16)
    bw_ref[:, 1, :, 0:BUNDLE_STATES] = bd_re.astype(BF16)
    bw_ref[:, 1, :, BUNDLE_STATES:] = bd_im.astype(BF16)
    c_re = cre_ref[...]
    c_im = cim_ref[...]
    cl_re = c_re * lb_re[:, None, :] - c_im * lb_im[:, None, :]
    cl_im = c_re * lb_im[:, None, :] + c_im * lb_re[:, None, :]
    cd_re, cd_im, cld_re, cld_im = in_slab(c_re), in_slab(c_im), in_slab(cl_re), in_slab(cl_im)
    for s in range(n_slabs):
        t_re, t_im = cd_re[s].T, cd_im[s].T
        cw_ref[s, 0, :, 0:LANES] = t_re.astype(BF16)
        cw_ref[s, 0, :, LANES:] = cld_re[s].T.astype(BF16)
        cw_ref[s, 1, :, 0:LANES] = (-t_im).astype(BF16)
        cw_ref[s, 1, :, LANES:] = (-cld_im[s].T).astype(BF16)
        k0 = _dot(bd_re[s].astype(BF16), t_re.astype(BF16)) - _dot(bd_im[s].astype(BF16), t_im.astype(BF16))
        k0_ref[s] = k0.astype(BF16)


def _zoh_call(a_re, a_im, log_dt, b_re, b_im, c_re, c_im):
    rows = DEPTH * 2 * N_SSM_GROUPS
    n_dirs = DEPTH * 2
    are = a_re.reshape(rows, STATE_P)
    aim = a_im.reshape(rows, STATE_P)
    ldt = log_dt.reshape(rows, 1)
    bre = jnp.swapaxes(b_re, -1, -2).reshape(rows, SSM_GROUP, STATE_P)
    bim = jnp.swapaxes(b_im, -1, -2).reshape(rows, SSM_GROUP, STATE_P)
    cre = c_re.reshape(rows, SSM_GROUP, STATE_P)
    cim = c_im.reshape(rows, SSM_GROUP, STATE_P)
    args = (are, aim, ldt, bre, bim, cre, cim)
    per_dir = lambda shape: pl.BlockSpec((shape[0] // n_dirs,) + tuple(shape[1:]),
                                         lambda i: (i,) + (0,) * (len(shape) - 1))
    n_slabs = n_dirs * SSM_BUNDLES
    outs = [jax.ShapeDtypeStruct((rows, STATE_P), F32)] * 2 + [
        jax.ShapeDtypeStruct((n_slabs, 2, LANES, 2 * BUNDLE_STATES), BF16),
        jax.ShapeDtypeStruct((n_slabs, 2, BUNDLE_STATES, 2 * LANES), BF16),
        jax.ShapeDtypeStruct((n_slabs, LANES, LANES), BF16)]
    l2_re, l2_im, bw, cw, k0 = pl.pallas_call(
        _zoh_kernel,
        grid=(n_dirs,),
        in_specs=[per_dir(a.shape) for a in args],
        out_specs=[per_dir(o.shape) for o in outs],
        out_shape=outs,
        compiler_params=_params("parallel"),
        name="s5_zoh",
    )(*args)
    lead = (DEPTH, 2, SSM_BUNDLES)
    return (l2_re.reshape(DEPTH, 2, 1, N_STATES), l2_im.reshape(DEPTH, 2, 1, N_STATES),
            bw.reshape(*lead, 2 * LANES, 2 * BUNDLE_STATES), cw.reshape(*lead, 2 * BUNDLE_STATES, 2 * LANES),
            k0.reshape(*lead, LANES, LANES))


def _state_slabs(h, k):
    return (h[:, k * BUNDLE_STATES:(k + 1) * BUNDLE_STATES],
            h[:, N_STATES + k * BUNDLE_STATES:N_STATES + (k + 1) * BUNDLE_STATES])


def _scan_project_state(h_re, h_im, cw_ref, k):
    return (_dot(h_re.astype(BF16), cw_ref[k, 0:BUNDLE_STATES, :])
            + _dot(h_im.astype(BF16), cw_ref[k, BUNDLE_STATES:, :]))


def _scan_init(h0_ref, cw_ref, hc, wp):
    hc[...] = h0_ref[...]
    for k in range(SSM_BUNDLES):
        h_re, h_im = _state_slabs(h0_ref, k)
        wp[:, k * LANES:(k + 1) * LANES] = _scan_project_state(h_re, h_im, cw_ref, k)[:, LANES:]


def _scan_direction(u_ref, bw_ref, cw_ref, k0_ref, l2re_ref, l2im_ref, y_ref, hs, hc, wp, *, batch, reverse):
    rows = u_ref.shape[0]
    pairs = rows // (2 * batch)
    half = pairs * batch
    u4 = u_ref[...].reshape(pairs, 2, batch, SSM_W)
    early, late = u4[:, 0].reshape(half, SSM_W), u4[:, 1].reshape(half, SSM_W)
    first, second = (late, early) if reverse else (early, late)
    first, second = first.astype(BF16), second.astype(BF16)

    for k in range(SSM_BUNDLES):
        slab = slice(k * LANES, (k + 1) * LANES)
        r = _dot(jnp.concatenate([first[:, slab], second[:, slab]], axis=1), bw_ref[k])
        hs[:, k * BUNDLE_STATES:(k + 1) * BUNDLE_STATES] = r[:, 0:BUNDLE_STATES]
        hs[:, N_STATES + k * BUNDLE_STATES:N_STATES + (k + 1) * BUNDLE_STATES] = r[:, BUNDLE_STATES:]

    for blk in range(N_STATES // SCAN_LANES):
        re = slice(blk * SCAN_LANES, (blk + 1) * SCAN_LANES)
        im = slice(N_STATES + blk * SCAN_LANES, N_STATES + (blk + 1) * SCAN_LANES)
        lr = jnp.broadcast_to(l2re_ref[:, re], (batch, SCAN_LANES))
        li = jnp.broadcast_to(l2im_ref[:, re], (batch, SCAN_LANES))
        hr, hi = hc[:, re], hc[:, im]
        for j in range(pairs):
            jj = pairs - 1 - j if reverse else j
            prow = slice(jj * batch, (jj + 1) * batch)
            hr, hi = (lr * hr - li * hi + hs[prow, re], lr * hi + li * hr + hs[prow, im])
            hs[prow, re] = hr
            hs[prow, im] = hi
        hc[:, re] = hr
        hc[:, im] = hi

    y_second, w, local = [], [], []
    for k in range(SSM_BUNDLES):
        h_re, h_im = _state_slabs(hs, k)
        z = _scan_project_state(h_re, h_im, cw_ref, k)
        y_second.append(z[:, 0:LANES])
        w.append(z[:, LANES:])
        local.append(_dot(first[:, k * LANES:(k + 1) * LANES], k0_ref[k]))
    y_second = jnp.concatenate(y_second, axis=1)
    w = jnp.concatenate(w, axis=1)
    if reverse:
        w_prev = jnp.concatenate([w[batch:], wp[...]], axis=0)
        wp[...] = w[0:batch]
    else:
        w_prev = jnp.concatenate([wp[...], w[0:half - batch]], axis=0)
        wp[...] = w[half - batch:]
    y_first = w_prev + jnp.concatenate(local, axis=1)
    y_early, y_late = (y_second, y_first) if reverse else (y_first, y_second)
    y = jnp.concatenate([y_early.reshape(pairs, 1, batch, SSM_W), y_late.reshape(pairs, 1, batch, SSM_W)], axis=1)
    y_ref[...] = y.reshape(rows, SSM_W).astype(y_ref.dtype)


def _s5_scan_kernel(uf_ref, ub_ref, bwf_ref, cwf_ref, k0f_ref, l2rf_ref, l2if_ref, bwb_ref, cwb_ref, k0b_ref,
                    l2rb_ref, l2ib_ref, h0f_ref, h0b_ref, yf_ref, yb_ref, hff_ref, hfb_ref,
                    hsf, hsb, hcf, hcb, wpf, wpb, *, batch):
    @pl.when(pl.program_id(0) == 0)
    def _():
        _scan_init(h0f_ref, cwf_ref, hcf, wpf)
        _scan_init(h0b_ref, cwb_ref, hcb, wpb)

    _scan_direction(uf_ref, bwf_ref, cwf_ref, k0f_ref, l2rf_ref, l2if_ref, yf_ref, hsf, hcf, wpf,
                    batch=batch, reverse=False)
    _scan_direction(ub_ref, bwb_ref, cwb_ref, k0b_ref, l2rb_ref, l2ib_ref, yb_ref, hsb, hcb, wpb,
                    batch=batch, reverse=True)
    hff_ref[...] = hcf[...]
    hfb_ref[...] = hcb[...]


def _gelu_tanh(x):
    return 0.5 * x * (1.0 + jnp.tanh(math.sqrt(2.0 / math.pi) * (x + 0.044715 * (x * x * x))))


def _s5_post_kernel(u_ref, yf_ref, yb_ref, d_ref, wg_ref, bg_ref, o_ref):
    batch, steps, _ = o_ref.shape
    y = d_ref[...] * u_ref[...].astype(F32) + yf_ref[...].astype(F32) + yb_ref[...].astype(F32)
    y = _gelu_tanh(y)
    gate = _sigmoid(_dot(y.astype(BF16), wg_ref[...]) + bg_ref[...])
    out = jnp.swapaxes((y * gate).reshape(steps, batch, SSM_W), 0, 1)
    o_ref[...] = out.astype(o_ref.dtype)


def _s5_calls(u_tm, batch, seq_len, layer, s5w, h0, d_skip, w_glu, b_glu):
    lam2_re, lam2_im, bw, cw, k0 = s5w
    rows = u_tm.shape[0]
    steps = SCAN_ROWS // batch
    n_chunks = seq_len // steps
    fwd_chunk = pl.BlockSpec((SCAN_ROWS, SSM_W), lambda i: (i, 0))
    bwd_chunk = pl.BlockSpec((SCAN_ROWS, SSM_W), lambda i: (n_chunks - 1 - i, 0))
    dir_spec = lambda a, d: pl.BlockSpec((None, None) + a.shape[2:], lambda i: (layer, d) + (0,) * (a.ndim - 2))
    weights = lambda d: [dir_spec(a, d) for a in (bw, cw, k0, lam2_re, lam2_im)]
    h0_spec = lambda d: pl.BlockSpec((None,) + h0.shape[1:], lambda i: (d, 0, 0))
    state_spec = pl.BlockSpec((batch, 2 * N_STATES), lambda i: (0, 0))
    state = jax.ShapeDtypeStruct((batch, 2 * N_STATES), F32)
    hs = lambda: pltpu.VMEM((SCAN_ROWS // 2, 2 * N_STATES), F32)
    hc = lambda: pltpu.VMEM((batch, 2 * N_STATES), F32)
    wp = lambda: pltpu.VMEM((batch, SSM_W), F32)

    yf, yb, hfin_f, hfin_b = pl.pallas_call(
        functools.partial(_s5_scan_kernel, batch=batch),
        grid=(n_chunks,),
        in_specs=[fwd_chunk, bwd_chunk] + weights(0) + weights(1) + [h0_spec(0), h0_spec(1)],
        out_specs=[fwd_chunk, bwd_chunk, state_spec, state_spec],
        out_shape=[jax.ShapeDtypeStruct((rows, SSM_W), BF16)] * 2 + [state, state],
        scratch_shapes=[hs(), hs(), hc(), hc(), wp(), wp()],
        compiler_params=_params("arbitrary"),
        name="s5_scan",
    )(u_tm, u_tm, bw, cw, k0, lam2_re, lam2_im, bw, cw, k0, lam2_re, lam2_im, h0, h0)

    out = pl.pallas_call(
        _s5_post_kernel,
        grid=(n_chunks,),
        in_specs=[fwd_chunk, fwd_chunk, fwd_chunk, _layer_spec(d_skip, layer, 1), _layer_spec(w_glu, layer, 1),
                  _layer_spec(b_glu, layer, 1)],
        out_specs=pl.BlockSpec((batch, steps, SSM_W), lambda i: (0, i, 0)),
        out_shape=jax.ShapeDtypeStruct((batch, seq_len, SSM_W), BF16),
        compiler_params=_params("parallel"),
        name="s5_post",
    )(u_tm, yf, yb, d_skip, w_glu, b_glu)
    return out.reshape(rows, SSM_W), hfin_f, hfin_b


def _layer_norm_rows(y, g, b):
    mu = jnp.mean(y, axis=-1, keepdims=True)
    yc = y - mu
    var = jnp.mean(yc * yc, axis=-1, keepdims=True)
    return yc * lax.rsqrt(var + EPS) * g + b


def _merge_kernel(x_ref, a_ref, s_ref, f_ref, sc_ref, sh_ref, g1_ref, wg_ref, wa_ref, ws_ref, wf_ref, wo_ref,
                  lg_ref, lb_ref, o_ref, wgb):
    @pl.when(pl.program_id(0) == 0)
    def _():
        wgb[...] = wg_ref[0].astype(BF16)

    rb = x_ref.shape[0] // MERGE_ROW_BLOCKS
    merged = {}

    def project(i):
        r = slice(i * rb, (i + 1) * rb)
        u = (x_ref[r, :] * (1.0 + sc_ref[...]) + sh_ref[...]).astype(BF16)
        merged[i] = (_sigmoid(_dot(u, wgb[:, 0:D_MODEL])) * _dot(a_ref[r, :], wa_ref[...])
                     + _sigmoid(_dot(u, wgb[:, D_MODEL:2 * D_MODEL])) * _dot(s_ref[r, :], ws_ref[...])
                     + _sigmoid(_dot(u, wgb[:, 2 * D_MODEL:3 * D_MODEL])) * _dot(f_ref[r, :], wf_ref[...])
                     ).astype(BF16)

    def finish(i):
        r = slice(i * rb, (i + 1) * rb)
        y = ALPHA * x_ref[r, :] + g1_ref[...] * _dot(merged.pop(i), wo_ref[...])
        o_ref[r, :] = _layer_norm_rows(y, lg_ref[...], lb_ref[...])

    for i in range(MERGE_ROW_BLOCKS + 1):
        if i < MERGE_ROW_BLOCKS:
            project(i)
        if i >= 1:
            finish(i - 1)


def _merge_call(x, attn, ssm, four, mod3, w_in, w_ba, w_bs, w_bf, w_out, ln_g, ln_b, layer, seq_len):
    rows = x.shape[0]
    tps = seq_len // ROW_TILE
    per_seq = mod3.shape[0] > 1
    row = lambda w: pl.BlockSpec((ROW_TILE, w), lambda i: (i, 0))
    lay = lambda a: _layer_spec(a, layer, 1)
    return pl.pallas_call(
        _merge_kernel,
        grid=(rows // ROW_TILE,),
        in_specs=[row(D_MODEL), row(ATTN_W), row(SSM_W), row(FOURIER_W),
                  _mod_spec(1, tps, per_seq), _mod_spec(0, tps, per_seq), _mod_spec(2, tps, per_seq),
                  _weight_slab_spec(layer, OFF_G, GATE_W), lay(w_ba), lay(w_bs), lay(w_bf), lay(w_out), lay(ln_g),
                  lay(ln_b)],
        out_specs=row(D_MODEL),
        out_shape=jax.ShapeDtypeStruct((rows, D_MODEL), F32),
        scratch_shapes=[pltpu.VMEM((D_MODEL, GATE_W), BF16)],
        compiler_params=_params("arbitrary"),
        name="merge",
    )(x, attn, ssm, four, mod3, mod3, mod3, w_in, w_ba, w_bs, w_bf, w_out, ln_g, ln_b)


def _ffn_kernel(xp_ref, x_ref, xn_ref, sc_ref, sh_ref, g2_ref, wu_ref, cw_ref, cb_ref, wd_ref, lg_ref, lb_ref, o_ref,
                *, tiles_per_seq):
    pos = pl.program_id(0) % tiles_per_seq
    keep_prev = jnp.where(pos == 0, 0.0, 1.0)
    keep_next = jnp.where(pos == tiles_per_seq - 1, 0.0, 1.0)
    mod = lambda v: v * (1.0 + sc_ref[...]) + sh_ref[...]
    x = x_ref[...]
    u = jnp.concatenate([mod(xp_ref[...]) * keep_prev, mod(x), mod(xn_ref[...]) * keep_next], axis=0).astype(BF16)
    rows = u.shape[0]
    tile = x.shape[0]
    inner = slice(SUBLANES, SUBLANES + tile)
    n_chunks = D_FF // FFN_CHUNK

    def up(c):
        ca = slice(c * FFN_CHUNK, (c + 1) * FFN_CHUNK)
        cb = slice(D_FF + c * FFN_CHUNK, D_FF + (c + 1) * FFN_CHUNK)
        return _dot(u, wu_ref[:, ca]), _dot(u, wu_ref[:, cb])

    def conv(h, col):
        prev = pltpu.roll(h, 1, axis=0)[inner]
        nxt = pltpu.roll(h, rows - 1, axis=0)[inner]
        return prev * cw_ref[0:1, col] + h[inner] * cw_ref[1:2, col] + nxt * cw_ref[2:3, col] + cb_ref[:, col]

    acc = jnp.zeros((tile, D_MODEL), F32)
    hidden = {c: up(c) for c in range(min(FFN_LEAD, n_chunks))}
    for c in range(n_chunks):
        if c + FFN_LEAD < n_chunks:
            hidden[c + FFN_LEAD] = up(c + FFN_LEAD)
        ha, hb = hidden.pop(c)
        ca = slice(c * FFN_CHUNK, (c + 1) * FFN_CHUNK)
        a = conv(ha, ca)
        b = conv(hb, slice(D_FF + c * FFN_CHUNK, D_FF + (c + 1) * FFN_CHUNK))
        gated = ((a * _sigmoid(a)) * b).astype(BF16)
        acc = acc + _dot(gated, wd_ref[ca, :])
    y = ALPHA * x + g2_ref[...] * acc
    o_ref[...] = _layer_norm_rows(y, lg_ref[...], lb_ref[...])


def _ffn_call(x, mod3, w_up, conv_w, conv_b, w_down, ln_g, ln_b, layer, seq_len):
    rows = x.shape[0]
    tile = FFN_ROWS
    tps = seq_len // tile
    halo_per_tile = tile // SUBLANES
    n_halo = rows // SUBLANES
    per_seq = mod3.shape[0] > 1
    mod = lambda ch: pl.BlockSpec((None, 1, D_MODEL), lambda i: ((i // tps) if per_seq else 0, 0, ch))
    res = lambda a: pl.BlockSpec((None,) + a.shape[1:], lambda i: (layer,) + (0,) * (a.ndim - 1),
                                 pipeline_mode=pl.Buffered(1))
    return pl.pallas_call(
        functools.partial(_ffn_kernel, tiles_per_seq=tps),
        grid=(rows // tile,),
        in_specs=[pl.BlockSpec((SUBLANES, D_MODEL), lambda i: (jnp.maximum(i * halo_per_tile - 1, 0), 0)),
                  pl.BlockSpec((tile, D_MODEL), lambda i: (i, 0)),
                  pl.BlockSpec((SUBLANES, D_MODEL), lambda i: (jnp.minimum((i + 1) * halo_per_tile, n_halo - 1), 0)),
                  mod(4), mod(3), mod(5), res(w_up), res(conv_w), res(conv_b), res(w_down), res(ln_g), res(ln_b)],
        out_specs=pl.BlockSpec((tile, D_MODEL), lambda i: (i, 0)),
        out_shape=jax.ShapeDtypeStruct((rows, D_MODEL), F32),
        compiler_params=_params("parallel"),
        name="conv_mlp",
    )(x, x, x, mod3, mod3, mod3, w_up, conv_w, conv_b, w_down, ln_g, ln_b)


def _rope_tables(n_tokens):
    rows = n_tokens // GRID_W
    row = np.repeat(np.arange(rows, dtype=np.float64), GRID_W)
    col = np.tile(np.arange(GRID_W, dtype=np.float64), rows)
    freqs = ROPE_THETA ** (-np.arange(ROPE_AXIS_PAIRS, dtype=np.float64) / ROPE_AXIS_PAIRS)
    ang = np.concatenate([row[:, None] * freqs, col[:, None] * freqs], axis=-1)
    cos = np.repeat(np.cos(ang), 2, axis=-1)
    sin = np.repeat(np.sin(ang), 2, axis=-1) * np.tile(np.array([-1.0, 1.0]), HEAD_DIM // 2)
    return (jnp.asarray(np.tile(cos, (1, N_Q_HEADS)), F32), jnp.asarray(np.tile(sin, (1, N_Q_HEADS)), F32))


def _trunk_layer(x, mod3, w, layer, batch, seq_len, rope_tabs, cache_k, cache_v, h0, kv_dtype):
    q, k, v, s_tm, f_in = _inproj_call(x, mod3, w["w_in"], w["gq"], w["gk"], w["ones"], rope_tabs, layer,
                                       batch, seq_len, kv_dtype)
    attn = _attn_call(q, k, v, cache_k, cache_v, layer, batch, seq_len)
    four = _fourier_call(f_in, batch, seq_len)
    ssm, hfin_f, hfin_b = _s5_calls(s_tm, batch, seq_len, layer, w["s5"], h0, w["d"], w["w_glu"], w["b_glu"])
    x1 = _merge_call(x, attn, ssm, four, mod3, w["w_in"], w["w_ba"], w["w_bs"], w["w_bf"], w["w_out"],
                     w["ln1_g"], w["ln1_b"], layer, seq_len)
    x2 = _ffn_call(x1, mod3, w["w_up"], w["conv_w"], w["conv_b"], w["w_down"], w["ln2_g"], w["ln2_b"], layer, seq_len)
    return x2, k, v, hfin_f, hfin_b


def _split_state(h):
    b = h.shape[0]
    re = h[..., 0].reshape(b, 2, N_STATES)
    im = h[..., 1].reshape(b, 2, N_STATES)
    return jnp.concatenate([re, im], axis=-1).transpose(1, 0, 2)


def _join_state(hf, hb):
    def one(h):
        b = h.shape[0]
        re = h[:, 0:N_STATES].reshape(b, N_SSM_GROUPS, STATE_P)
        im = h[:, N_STATES:].reshape(b, N_SSM_GROUPS, STATE_P)
        return jnp.stack([re, im], axis=-1)
    return jnp.stack([one(hf), one(hb)], axis=1)


def kernel(x_prompt, x_sample, c, cache_k, cache_v, state_ssm, c_ctx, w_ada, b_ada, w_in, q_norm_g, k_norm_g, ssm_a_re, ssm_a_im, ssm_log_dt, ssm_b_re, ssm_b_im, ssm_c_re, ssm_c_im, ssm_d, w_glu, b_glu, w_br_attn, w_br_ssm, w_br_four, w_out, ln1_g, ln1_b, w_up, conv_w, conv_b, w_down, ln2_g, ln2_b):
    bc, lc, _ = x_prompt.shape
    bd, ld, _ = x_sample.shape
    past = cache_k.shape[2]

    cvec = jnp.zeros((MOD_ROWS, D_MODEL), F32).at[0:bd].set(c).at[bd].set(c_ctx)
    mod = _mod_call(cvec, w_ada, b_ada)

    s5w = _zoh_call(ssm_a_re, ssm_a_im, ssm_log_dt, ssm_b_re, ssm_b_im, ssm_c_re, ssm_c_im)

    head_mean = np.kron(np.eye(N_Q_HEADS), np.full((HEAD_DIM, HEAD_DIM), 1.0 / HEAD_DIM))
    rope_tabs = _rope_tables(ld)
    vec = lambda a: a.reshape(DEPTH, 1, -1)
    w = dict(
        w_in=w_in, gq=vec(jnp.tile(q_norm_g, (1, N_Q_HEADS))), gk=vec(jnp.tile(k_norm_g, (1, N_KV_HEADS))),
        ones=jnp.asarray(head_mean, F32).astype(BF16), s5=s5w, d=vec(ssm_d),
        w_glu=w_glu.astype(BF16), b_glu=vec(b_glu), w_ba=w_br_attn.astype(BF16), w_bs=w_br_ssm.astype(BF16),
        w_bf=w_br_four.astype(BF16), w_out=w_out.astype(BF16), ln1_g=vec(ln1_g), ln1_b=vec(ln1_b),
        w_up=w_up.astype(BF16), conv_w=conv_w, conv_b=vec(conv_b), w_down=w_down.astype(BF16), ln2_g=vec(ln2_g),
        ln2_b=vec(ln2_b))

    h = x_prompt.reshape(bc * lc, D_MODEL)
    h0_ctx = jnp.zeros((2, bc, 2 * N_STATES), F32)
    ks, vs, ss = [], [], []
    for l in range(DEPTH):
        mod_ctx = mod[l, bd:bd + 1].reshape(1, 1, 6 * D_MODEL)
        h, k_l, v_l, hf, hb = _trunk_layer(h, mod_ctx, w, l, bc, lc, None, None, None, h0_ctx, F32)
        ks.append(k_l.reshape(bc, lc, N_KV_HEADS, HEAD_DIM))
        vs.append(v_l.reshape(bc, lc, N_KV_HEADS, HEAD_DIM))
        ss.append(_join_state(hf, hb))
    new_cache_k = jnp.stack(ks, axis=1)
    new_cache_v = jnp.stack(vs, axis=1)
    new_state = jnp.stack(ss, axis=1)

    zx = x_sample.reshape(bd * ld, D_MODEL)
    ck = cache_k.reshape(bd, DEPTH, past, KV_W).astype(BF16)
    cv = cache_v.reshape(bd, DEPTH, past, KV_W).astype(BF16)
    for l in range(DEPTH):
        mod_dec = mod[l, 0:bd].reshape(bd, 1, 6 * D_MODEL)
        zx, _, _, _, _ = _trunk_layer(zx, mod_dec, w, l, bd, ld, rope_tabs, ck, cv,
                                      _split_state(state_ssm[:, l]), BF16)

    return (h.reshape(bc, lc, D_MODEL), zx.reshape(bd, ld, D_MODEL), new_cache_k, new_cache_v, new_state)
```

```python
import functools
import math

import jax
import jax.numpy as jnp
import numpy as np
from jax import lax
from jax.experimental import pallas as pl
from jax.experimental.pallas import tpu as pltpu

F32 = jnp.float32
BF16 = jnp.bfloat16

D_MODEL = 1024
DEPTH = 2
GRID_W = 64
HEAD_DIM = 64
N_Q_HEADS = 8
N_KV_HEADS = 2
GQA_GROUP = N_Q_HEADS // N_KV_HEADS
ATTN_W = N_Q_HEADS * HEAD_DIM
KV_W = N_KV_HEADS * HEAD_DIM
ROPE_AXIS_PAIRS = HEAD_DIM // 4
ROPE_THETA = 10000.0
SSM_W = D_MODEL // 2
SSM_GROUP = 16
N_SSM_GROUPS = SSM_W // SSM_GROUP
STATE_P = 64
N_STATES = N_SSM_GROUPS * STATE_P
FOURIER_W = D_MODEL // 2
FOURIER_GROUPS = 4
FOURIER_GW = FOURIER_W // FOURIER_GROUPS
N_BRANCH = 3
GATE_W = N_BRANCH * D_MODEL
D_IN = ATTN_W + 2 * KV_W + SSM_W + FOURIER_W + GATE_W
OFF_K = ATTN_W
OFF_V = ATTN_W + KV_W
OFF_S = ATTN_W + 2 * KV_W
OFF_F = OFF_S + SSM_W
OFF_G = OFF_F + FOURIER_W
D_FF = ((8 * D_MODEL // 3 + 127) // 128) * 128
EPS = 1e-6
ALPHA = (2 * DEPTH) ** 0.25

V7X_VMEM_BYTES = 64 * 1024 * 1024
VMEM_LIMIT_BYTES = V7X_VMEM_BYTES - 8 * 1024 * 1024
LANES = 128
SUBLANES = 8
SSM_BUNDLE_GROUPS = LANES // SSM_GROUP
SSM_BUNDLES = N_SSM_GROUPS // SSM_BUNDLE_GROUPS
BUNDLE_STATES = SSM_BUNDLE_GROUPS * STATE_P

ROW_TILE = 512
MERGE_ROW_BLOCKS = 2
ATTN_Q_ROWS = 512
FFN_ROWS = 256
FFN_CHUNK = 256
FFN_LEAD = 2
SCAN_ROWS = 512
SCAN_LANES = 512
MOD_COLS = 1536
MOD_ROWS = 16


def _params(*sem):
    return pltpu.CompilerParams(dimension_semantics=sem, vmem_limit_bytes=VMEM_LIMIT_BYTES)


def _dot(a, b):
    return jnp.dot(a, b, preferred_element_type=F32)


def _sigmoid(x):
    return 1.0 / (1.0 + jnp.exp(-x))


def _layer_spec(arr, layer, n_grid):
    zeros = (0,) * (arr.ndim - 1)
    if n_grid == 1:
        return pl.BlockSpec((None,) + arr.shape[1:], lambda i: (layer,) + zeros)
    return pl.BlockSpec((None,) + arr.shape[1:], lambda i, j: (layer,) + zeros)


def _mod_kernel(c_ref, w_ref, b_ref, o_ref):
    c = c_ref[...]
    s = (c * _sigmoid(c)).astype(BF16)
    o_ref[...] = _dot(s, w_ref[...].astype(BF16)) + b_ref[...]


def _mod_call(cvec, w_ada, b_ada):
    n = 6 * D_MODEL
    return pl.pallas_call(
        _mod_kernel,
        grid=(DEPTH, n // MOD_COLS),
        in_specs=[
            pl.BlockSpec((MOD_ROWS, D_MODEL), lambda l, j: (0, 0)),
            pl.BlockSpec((None, D_MODEL, MOD_COLS), lambda l, j: (l, 0, j)),
            pl.BlockSpec((None, 1, MOD_COLS), lambda l, j: (l, 0, j)),
        ],
        out_specs=pl.BlockSpec((None, MOD_ROWS, MOD_COLS), lambda l, j: (l, 0, j)),
        out_shape=jax.ShapeDtypeStruct((DEPTH, MOD_ROWS, n), F32),
        compiler_params=_params("parallel", "parallel"),
        name="mod",
    )(cvec, w_ada, b_ada.reshape(DEPTH, 1, n))


def _swap_pairs(x):
    w = x.shape[-1]
    nxt = pltpu.roll(x, w - 1, axis=1)
    prv = pltpu.roll(x, 1, axis=1)
    lane = lax.broadcasted_iota(jnp.int32, x.shape, 1)
    return jnp.where((lane & 1) == 0, nxt, prv)


def _weight_slab_spec(layer, col0, width):
    return pl.BlockSpec((pl.Element(1), pl.Element(D_MODEL), pl.Element(width)), lambda i: (layer, 0, col0),
                        pipeline_mode=pl.Buffered(1))


def _inproj_kernel(*refs, rope):
    if rope:
        (x_ref, sc_ref, sh_ref, w_ref, gq_ref, gk_ref, ones_ref, cos_ref, sin_ref,
         q_ref, k_ref, v_ref, s_ref, f_ref, wbf) = refs
    else:
        (x_ref, sc_ref, sh_ref, w_ref, gq_ref, gk_ref, ones_ref,
         q_ref, k_ref, v_ref, s_ref, f_ref, wbf) = refs
    batch, steps, _ = x_ref.shape
    rows = batch * steps

    @pl.when(pl.program_id(0) == 0)
    def _():
        wbf[...] = w_ref[0].astype(BF16)

    u = (x_ref[...] * (1.0 + sc_ref[...]) + sh_ref[...]).reshape(rows, D_MODEL).astype(BF16)

    zq = _dot(u, wbf[:, 0:OFF_K])
    zk = _dot(u, wbf[:, OFF_K:OFF_V])
    zv = _dot(u, wbf[:, OFF_V:OFF_S])
    msq_q = _dot((zq * zq).astype(BF16), ones_ref[...])
    msq_k = _dot((zk * zk).astype(BF16), ones_ref[0:KV_W, 0:KV_W])
    q = zq * lax.rsqrt(msq_q + EPS) * gq_ref[...]
    k = zk * lax.rsqrt(msq_k + EPS) * gk_ref[...]
    if rope:
        cos = jnp.concatenate([cos_ref[...]] * batch, axis=0)
        sin = jnp.concatenate([sin_ref[...]] * batch, axis=0)
        q = q * cos + _swap_pairs(q) * sin
        k = k * cos[:, 0:KV_W] + _swap_pairs(k) * sin[:, 0:KV_W]
    q_ref[...] = (q * (HEAD_DIM ** -0.5)).reshape(batch, steps, ATTN_W).astype(q_ref.dtype)
    k_ref[...] = k.reshape(batch, steps, KV_W).astype(k_ref.dtype)
    v_ref[...] = zv.reshape(batch, steps, KV_W).astype(v_ref.dtype)
    f_ref[...] = _dot(u, wbf[:, OFF_F:OFF_G]).reshape(batch, steps, FOURIER_W).astype(f_ref.dtype)
    s = _dot(u, wbf[:, OFF_S:OFF_F]).reshape(batch, steps, SSM_W)
    s_ref[...] = jnp.swapaxes(s, 0, 1).reshape(rows, SSM_W).astype(s_ref.dtype)


def _inproj_call(x, mod3, w_in, gq, gk, ones_blk, rope_tabs, layer, batch, seq_len, kv_dtype):
    steps = ROW_TILE // batch
    rope = rope_tabs is not None
    x3 = x.reshape(batch, seq_len, D_MODEL)
    seq = lambda w: pl.BlockSpec((batch, steps, w), lambda i: (0, i, 0))
    mod = lambda ch: pl.BlockSpec((mod3.shape[0], 1, D_MODEL), lambda i: (0, 0, ch))
    in_specs = [seq(D_MODEL), mod(1), mod(0), _weight_slab_spec(layer, 0, OFF_G),
                _layer_spec(gq, layer, 1), _layer_spec(gk, layer, 1), pl.BlockSpec(ones_blk.shape, lambda i: (0, 0))]
    args = [x3, mod3, mod3, w_in, gq, gk, ones_blk]
    if rope:
        tab = pl.BlockSpec((steps, ATTN_W), lambda i: (i, 0))
        in_specs += [tab, tab]
        args += list(rope_tabs)
    widths = (ATTN_W, KV_W, KV_W, FOURIER_W)
    dtypes = (BF16, kv_dtype, kv_dtype, BF16)
    outs = pl.pallas_call(
        functools.partial(_inproj_kernel, rope=rope),
        grid=(seq_len // steps,),
        in_specs=in_specs,
        out_specs=[seq(w) for w in widths[:3]] + [pl.BlockSpec((ROW_TILE, SSM_W), lambda i: (i, 0)), seq(FOURIER_W)],
        out_shape=([jax.ShapeDtypeStruct((batch, seq_len, w), d) for w, d in zip(widths[:3], dtypes[:3])]
                   + [jax.ShapeDtypeStruct((seq_len * batch, SSM_W), F32),
                      jax.ShapeDtypeStruct((batch, seq_len, FOURIER_W), BF16)]),
        scratch_shapes=[pltpu.VMEM((D_MODEL, OFF_G), BF16)],
        compiler_params=_params("arbitrary"),
        name="in_proj",
    )(*args)
    q, k, v, s_tm, f = outs
    flat = lambda a: a.reshape(batch * seq_len, a.shape[-1])
    return flat(q), flat(k), flat(v), s_tm, flat(f)


def _attn_kernel(*refs, has_cache):
    if has_cache:
        q_ref, k_ref, v_ref, ck_ref, cv_ref, o_ref = refs
    else:
        q_ref, k_ref, v_ref, o_ref = refs
    q = q_ref[...].astype(F32)
    contract_last = (((1,), (1,)), ((), ()))
    kv = lambda h: slice((h // GQA_GROUP) * HEAD_DIM, (h // GQA_GROUP + 1) * HEAD_DIM)
    scores = {}
    outs = []

    def score(h):
        qh = q[:, h * HEAD_DIM:(h + 1) * HEAD_DIM].astype(BF16)
        s1 = lax.dot_general(qh, k_ref[:, kv(h)].astype(BF16), contract_last, preferred_element_type=F32)
        s2 = None
        if has_cache:
            s2 = lax.dot_general(qh, ck_ref[:, kv(h)], contract_last, preferred_element_type=F32)
        scores[h] = (s1, s2)

    def attend(h):
        s1, s2 = scores.pop(h)
        m = jnp.max(s1, axis=-1, keepdims=True)
        if has_cache:
            m = jnp.maximum(m, jnp.max(s2, axis=-1, keepdims=True))
        p1 = jnp.exp(s1 - m)
        den = jnp.sum(p1, axis=-1, keepdims=True)
        o = _dot(p1.astype(BF16), v_ref[:, kv(h)].astype(BF16))
        if has_cache:
            p2 = jnp.exp(s2 - m)
            den = den + jnp.sum(p2, axis=-1, keepdims=True)
            o = o + _dot(p2.astype(BF16), cv_ref[:, kv(h)])
        outs.append(o * (1.0 / den))

    for h in range(N_Q_HEADS + 1):
        if h < N_Q_HEADS:
            score(h)
        if h >= 1:
            attend(h - 1)
    o_ref[...] = jnp.concatenate(outs, axis=1).astype(o_ref.dtype)


def _attn_call(q, k, v, cache_k, cache_v, layer, batch, seq_len):
    rows = q.shape[0]
    qb = min(ATTN_Q_ROWS, seq_len)
    nq = seq_len // qb
    has_cache = cache_k is not None
    in_specs = [pl.BlockSpec((qb, ATTN_W), lambda b, i: (b * nq + i, 0)),
                pl.BlockSpec((seq_len, KV_W), lambda b, i: (b, 0)),
                pl.BlockSpec((seq_len, KV_W), lambda b, i: (b, 0))]
    args = [q, k, v]
    if has_cache:
        past = cache_k.shape[2]
        in_specs += [pl.BlockSpec((None, None, past, KV_W), lambda b, i: (b, layer, 0, 0))] * 2
        args += [cache_k, cache_v]
    return pl.pallas_call(
        functools.partial(_attn_kernel, has_cache=has_cache),
        grid=(batch, nq),
        in_specs=in_specs,
        out_specs=pl.BlockSpec((qb, ATTN_W), lambda b, i: (b * nq + i, 0)),
        out_shape=jax.ShapeDtypeStruct((rows, ATTN_W), BF16),
        compiler_params=_params("parallel", "parallel"),
        name="attention",
    )(*args)


def _fourier_kernel(f_ref, wc_ref, wl_ref, o_ref):
    f = f_ref[...]
    parts = [_dot(f[:, g * FOURIER_GW:(g + 1) * FOURIER_GW], wc_ref[...]) for g in range(FOURIER_GROUPS)]
    xc = jnp.concatenate([p[:, 0:FOURIER_GW] for p in parts], axis=1)
    xs = jnp.concatenate([p[:, FOURIER_GW:2 * FOURIER_GW] for p in parts], axis=1)
    stacked = jnp.concatenate([xc, xs], axis=0).astype(BF16)
    o_ref[...] = _dot(wl_ref[...], stacked).astype(o_ref.dtype)


def _dft_tables(seq_len):
    c = np.arange(FOURIER_GW)
    ang_c = 2.0 * np.pi * np.outer(c, c) / FOURIER_GW
    wc = np.concatenate([np.cos(ang_c), np.sin(ang_c)], axis=1) / math.sqrt(FOURIER_GW)
    t = np.arange(seq_len)
    ang_l = 2.0 * np.pi * np.outer(t, t) / seq_len
    wl = np.concatenate([np.cos(ang_l), -np.sin(ang_l)], axis=1) / math.sqrt(seq_len)
    return (jnp.asarray(wc, F32).astype(BF16), jnp.asarray(wl, F32).astype(BF16))


def _fourier_call(f, batch, seq_len):
    wc, wl = _dft_tables(seq_len)
    return pl.pallas_call(
        _fourier_kernel,
        grid=(batch,),
        in_specs=[pl.BlockSpec((seq_len, FOURIER_W), lambda b: (b, 0)),
                  pl.BlockSpec(wc.shape, lambda b: (0, 0)),
                  pl.BlockSpec(wl.shape, lambda b: (0, 0))],
        out_specs=pl.BlockSpec((seq_len, FOURIER_W), lambda b: (b, 0)),
        out_shape=jax.ShapeDtypeStruct(f.shape, BF16),
        compiler_params=_params("parallel"),
        name="fourier",
    )(f, wc, wl)


def _block_diagonal(compact, group_rows, group_cols):
    rows, w = compact.shape
    log2 = lambda n: n.bit_length() - 1
    sel = (lax.broadcasted_iota(jnp.int32, (w, SSM_BUNDLE_GROUPS * w), 0)
           == (lax.broadcasted_iota(jnp.int32, (w, SSM_BUNDLE_GROUPS * w), 1) & (w - 1)))
    wide = _dot(compact.astype(BF16), jnp.where(sel, 1.0, 0.0).astype(BF16))
    row_group = (lax.broadcasted_iota(jnp.int32, wide.shape, 0) >> log2(group_rows)) & (SSM_BUNDLE_GROUPS - 1)
    col_group = lax.broadcasted_iota(jnp.int32, wide.shape, 1) >> log2(group_cols)
    return jnp.where(row_group == col_group, wide, 0.0)


def _zoh_lambda(a_re, a_im, log_dt):
    dt = jnp.exp(log_dt)
    mag = jnp.exp(a_re * dt)
    ang = a_im * dt
    return mag * jnp.cos(ang), mag * jnp.sin(ang)


def _zoh_kernel(are_ref, aim_ref, ldt_ref, bre_ref, bim_ref, cre_ref, cim_ref,
                l2re_ref, l2im_ref, bw_ref, cw_ref, k0_ref):
    a_re = are_ref[...]
    a_im = aim_ref[...]
    lb_re, lb_im = _zoh_lambda(a_re, a_im, ldt_ref[...])
    den = a_re * a_re + a_im * a_im
    n_re = lb_re - 1.0
    n_im = lb_im
    f_re = (n_re * a_re + n_im * a_im) / den
    f_im = (n_im * a_re - n_re * a_im) / den
    l2re_ref[...] = lb_re * lb_re - lb_im * lb_im
    l2im_ref[...] = 2.0 * lb_re * lb_im
    b_re = bre_ref[...]
    b_im = bim_ref[...]
    bb_re = f_re[:, None, :] * b_re - f_im[:, None, :] * b_im
    bb_im = f_re[:, None, :] * b_im + f_im[:, None, :] * b_re
    lbb_re = lb_re[:, None, :] * bb_re - lb_im[:, None, :] * bb_im
    lbb_im = lb_re[:, None, :] * bb_im + lb_im[:, None, :] * bb_re
    flat = lambda a: a.reshape(a.shape[0] * a.shape[1], a.shape[2])
    n_slabs = bw_ref.shape[0]
    in_slab = lambda a: _block_diagonal(flat(a), SSM_GROUP, STATE_P).reshape(n_slabs, LANES, BUNDLE_STATES)
    bd_re, bd_im = in_slab(bb_re), in_slab(bb_im)
    bw_ref[:, 0, :, 0:BUNDLE_STATES] = in_slab(lbb_re).astype(BF16)
    bw_ref[:, 0, :, BUNDLE_STATES:] = in_slab(lbb_im).astype(BF16)
    bw_ref[:, 1, :, 0:BUNDLE_STATES] = bd_re.astype(BF16)
    bw_ref[:, 1, :, BUNDLE_STATES:] = bd_im.astype(BF16)
    c_re = cre_ref[...]
    c_im = cim_ref[...]
    cl_re = c_re * lb_re[:, None, :] - c_im * lb_im[:, None, :]
    cl_im = c_re * lb_im[:, None, :] + c_im * lb_re[:, None, :]
    cd_re, cd_im, cld_re, cld_im = in_slab(c_re), in_slab(c_im), in_slab(cl_re), in_slab(cl_im)
    for s in range(n_slabs):
        t_re, t_im = cd_re[s].T, cd_im[s].T
        cw_ref[s, 0, :, 0:LANES] = t_re.astype(BF16)
        cw_ref[s, 0, :, LANES:] = cld_re[s].T.astype(BF16)
        cw_ref[s, 1, :, 0:LANES] = (-t_im).astype(BF16)
        cw_ref[s, 1, :, LANES:] = (-cld_im[s].T).astype(BF16)
        k0 = _dot(bd_re[s].astype(BF16), t_re.astype(BF16)) - _dot(bd_im[s].astype(BF16), t_im.astype(BF16))
        k0_ref[s] = k0.astype(BF16)


def _zoh_call(a_re, a_im, log_dt, b_re, b_im, c_re, c_im):
    rows = DEPTH * 2 * N_SSM_GROUPS
    n_dirs = DEPTH * 2
    are = a_re.reshape(rows, STATE_P)
    aim = a_im.reshape(rows, STATE_P)
    ldt = log_dt.reshape(rows, 1)
    bre = jnp.swapaxes(b_re, -1, -2).reshape(rows, SSM_GROUP, STATE_P)
    bim = jnp.swapaxes(b_im, -1, -2).reshape(rows, SSM_GROUP, STATE_P)
    cre = c_re.reshape(rows, SSM_GROUP, STATE_P)
    cim = c_im.reshape(rows, SSM_GROUP, STATE_P)
    args = (are, aim, ldt, bre, bim, cre, cim)
    per_dir = lambda shape: pl.BlockSpec((shape[0] // n_dirs,) + tuple(shape[1:]),
                                         lambda i: (i,) + (0,) * (len(shape) - 1))
    n_slabs = n_dirs * SSM_BUNDLES
    outs = [jax.ShapeDtypeStruct((rows, STATE_P), F32)] * 2 + [
        jax.ShapeDtypeStruct((n_slabs, 2, LANES, 2 * BUNDLE_STATES), BF16),
        jax.ShapeDtypeStruct((n_slabs, 2, BUNDLE_STATES, 2 * LANES), BF16),
        jax.ShapeDtypeStruct((n_slabs, LANES, LANES), BF16)]
    l2_re, l2_im, bw, cw, k0 = pl.pallas_call(
        _zoh_kernel,
        grid=(n_dirs,),
        in_specs=[per_dir(a.shape) for a in args],
        out_specs=[per_dir(o.shape) for o in outs],
        out_shape=outs,
        compiler_params=_params("parallel"),
        name="s5_zoh",
    )(*args)
    lead = (DEPTH, 2, SSM_BUNDLES)
    return (l2_re.reshape(DEPTH, 2, 1, N_STATES), l2_im.reshape(DEPTH, 2, 1, N_STATES),
            bw.reshape(*lead, 2 * LANES, 2 * BUNDLE_STATES), cw.reshape(*lead, 2 * BUNDLE_STATES, 2 * LANES),
            k0.reshape(*lead, LANES, LANES))


def _state_slabs(h, k):
    return (h[:, k * BUNDLE_STATES:(k + 1) * BUNDLE_STATES],
            h[:, N_STATES + k * BUNDLE_STATES:N_STATES + (k + 1) * BUNDLE_STATES])


def _scan_project_state(h_re, h_im, cw_ref, k):
    return (_dot(h_re.astype(BF16), cw_ref[k, 0:BUNDLE_STATES, :])
            + _dot(h_im.astype(BF16), cw_ref[k, BUNDLE_STATES:, :]))


def _scan_init(h0_ref, cw_ref, hc, wp):
    hc[...] = h0_ref[...]
    for k in range(SSM_BUNDLES):
        h_re, h_im = _state_slabs(h0_ref, k)
        wp[:, k * LANES:(k + 1) * LANES] = _scan_project_state(h_re, h_im, cw_ref, k)[:, LANES:]


def _scan_direction(u_ref, bw_ref, cw_ref, k0_ref, l2re_ref, l2im_ref, y_ref, hs, hc, wp, *, batch, reverse):
    rows = u_ref.shape[0]
    pairs = rows // (2 * batch)
    half = pairs * batch
    u4 = u_ref[...].reshape(pairs, 2, batch, SSM_W)
    early, late = u4[:, 0].reshape(half, SSM_W), u4[:, 1].reshape(half, SSM_W)
    first, second = (late, early) if reverse else (early, late)
    first, second = first.astype(BF16), second.astype(BF16)

    for k in range(SSM_BUNDLES):
        slab = slice(k * LANES, (k + 1) * LANES)
        r = _dot(jnp.concatenate([first[:, slab], second[:, slab]], axis=1), bw_ref[k])
        hs[:, k * BUNDLE_STATES:(k + 1) * BUNDLE_STATES] = r[:, 0:BUNDLE_STATES]
        hs[:, N_STATES + k * BUNDLE_STATES:N_STATES + (k + 1) * BUNDLE_STATES] = r[:, BUNDLE_STATES:]

    for blk in range(N_STATES // SCAN_LANES):
        re = slice(blk * SCAN_LANES, (blk + 1) * SCAN_LANES)
        im = slice(N_STATES + blk * SCAN_LANES, N_STATES + (blk + 1) * SCAN_LANES)
        lr = jnp.broadcast_to(l2re_ref[:, re], (batch, SCAN_LANES))
        li = jnp.broadcast_to(l2im_ref[:, re], (batch, SCAN_LANES))
        hr, hi = hc[:, re], hc[:, im]
        for j in range(pairs):
            jj = pairs - 1 - j if reverse else j
            prow = slice(jj * batch, (jj + 1) * batch)
            hr, hi = (lr * hr - li * hi + hs[prow, re], lr * hi + li * hr + hs[prow, im])
            hs[prow, re] = hr
            hs[prow, im] = hi
        hc[:, re] = hr
        hc[:, im] = hi

    y_second, w, local = [], [], []
    for k in range(SSM_BUNDLES):
        h_re, h_im = _state_slabs(hs, k)
        z = _scan_project_state(h_re, h_im, cw_ref, k)
        y_second.append(z[:, 0:LANES])
        w.append(z[:, LANES:])
        local.append(_dot(first[:, k * LANES:(k + 1) * LANES], k0_ref[k]))
    y_second = jnp.concatenate(y_second, axis=1)
    w = jnp.concatenate(w, axis=1)
    if reverse:
        w_prev = jnp.concatenate([w[batch:], wp[...]], axis=0)
        wp[...] = w[0:batch]
    else:
        w_prev = jnp.concatenate([wp[...], w[0:half - batch]], axis=0)
        wp[...] = w[half - batch:]
    y_first = w_prev + jnp.concatenate(local, axis=1)
    y_early, y_late = (y_second, y_first) if reverse else (y_first, y_second)
    y = jnp.concatenate([y_early.reshape(pairs, 1, batch, SSM_W), y_late.reshape(pairs, 1, batch, SSM_W)], axis=1)
    y_ref[...] = y.reshape(rows, SSM_W).astype(y_ref.dtype)


def _s5_scan_kernel(uf_ref, ub_ref, bwf_ref, cwf_ref, k0f_ref, l2rf_ref, l2if_ref, bwb_ref, cwb_ref, k0b_ref,
                    l2rb_ref, l2ib_ref, h0f_ref, h0b_ref, yf_ref, yb_ref, hff_ref, hfb_ref,
                    hsf, hsb, hcf, hcb, wpf, wpb, *, batch):
    @pl.when(pl.program_id(0) == 0)
    def _():
        _scan_init(h0f_ref, cwf_ref, hcf, wpf)
        _scan_init(h0b_ref, cwb_ref, hcb, wpb)

    _scan_direction(uf_ref, bwf_ref, cwf_ref, k0f_ref, l2rf_ref, l2if_ref, yf_ref, hsf, hcf, wpf,
                    batch=batch, reverse=False)
    _scan_direction(ub_ref, bwb_ref, cwb_ref, k0b_ref, l2rb_ref, l2ib_ref, yb_ref, hsb, hcb, wpb,
                    batch=batch, reverse=True)
    hff_ref[...] = hcf[...]
    hfb_ref[...] = hcb[...]


def _gelu_tanh(x):
    return 0.5 * x * (1.0 + jnp.tanh(math.sqrt(2.0 / math.pi) * (x + 0.044715 * (x * x * x))))


def _s5_calls(u_tm, batch, seq_len, layer, s5w, h0):
    lam2_re, lam2_im, bw, cw, k0 = s5w
    rows = u_tm.shape[0]
    steps = SCAN_ROWS // batch
    n_chunks = seq_len // steps
    fwd_chunk = pl.BlockSpec((SCAN_ROWS, SSM_W), lambda i: (i, 0))
    bwd_chunk = pl.BlockSpec((SCAN_ROWS, SSM_W), lambda i: (n_chunks - 1 - i, 0))
    dir_spec = lambda a, d: pl.BlockSpec((None, None) + a.shape[2:], lambda i: (layer, d) + (0,) * (a.ndim - 2))
    weights = lambda d: [dir_spec(a, d) for a in (bw, cw, k0, lam2_re, lam2_im)]
    h0_spec = lambda d: pl.BlockSpec((None,) + h0.shape[1:], lambda i: (d, 0, 0))
    state_spec = pl.BlockSpec((batch, 2 * N_STATES), lambda i: (0, 0))
    state = jax.ShapeDtypeStruct((batch, 2 * N_STATES), F32)
    hs = lambda: pltpu.VMEM((SCAN_ROWS // 2, 2 * N_STATES), F32)
    hc = lambda: pltpu.VMEM((batch, 2 * N_STATES), F32)
    wp = lambda: pltpu.VMEM((batch, SSM_W), F32)

    yf, yb, hfin_f, hfin_b = pl.pallas_call(
        functools.partial(_s5_scan_kernel, batch=batch),
        grid=(n_chunks,),
        in_specs=[fwd_chunk, bwd_chunk] + weights(0) + weights(1) + [h0_spec(0), h0_spec(1)],
        out_specs=[fwd_chunk, bwd_chunk, state_spec, state_spec],
        out_shape=[jax.ShapeDtypeStruct((rows, SSM_W), BF16)] * 2 + [state, state],
        scratch_shapes=[hs(), hs(), hc(), hc(), wp(), wp()],
        compiler_params=_params("arbitrary"),
        name="s5_scan",
    )(u_tm, u_tm, bw, cw, k0, lam2_re, lam2_im, bw, cw, k0, lam2_re, lam2_im, h0, h0)

    return yf, yb, hfin_f, hfin_b


def _layer_norm_rows(y, g, b):
    mu = jnp.mean(y, axis=-1, keepdims=True)
    yc = y - mu
    var = jnp.mean(yc * yc, axis=-1, keepdims=True)
    return yc * lax.rsqrt(var + EPS) * g + b


def _merge_kernel(x_ref, a_ref, f_ref, us_ref, yf_ref, yb_ref, d_ref, wglu_ref, bglu_ref, sc_ref, sh_ref, g1_ref,
                  wg_ref, wa_ref, ws_ref, wf_ref, wo_ref, lg_ref, lb_ref, o_ref, wgb):
    batch, steps, _ = x_ref.shape
    rows = batch * steps

    @pl.when(pl.program_id(0) == 0)
    def _():
        wgb[...] = wg_ref[0].astype(BF16)

    y = d_ref[...] * us_ref[...] + yf_ref[...].astype(F32) + yb_ref[...].astype(F32)
    y = _gelu_tanh(y)
    y = y * _sigmoid(_dot(y.astype(BF16), wglu_ref[...]) + bglu_ref[...])
    ssm = jnp.swapaxes(y.reshape(steps, batch, SSM_W), 0, 1).reshape(rows, SSM_W).astype(BF16)

    x = x_ref[...]
    u = (x * (1.0 + sc_ref[...]) + sh_ref[...]).reshape(rows, D_MODEL).astype(BF16)
    res = (ALPHA * x).reshape(rows, D_MODEL)
    g1 = jnp.broadcast_to(g1_ref[...], (batch, steps, D_MODEL)).reshape(rows, D_MODEL)
    attn = a_ref[...].reshape(rows, ATTN_W)
    four = f_ref[...].reshape(rows, FOURIER_W)
    rb = rows // MERGE_ROW_BLOCKS
    bb = batch // MERGE_ROW_BLOCKS
    merged = {}

    def project(i):
        r = slice(i * rb, (i + 1) * rb)
        merged[i] = (_sigmoid(_dot(u[r], wgb[:, 0:D_MODEL])) * _dot(attn[r], wa_ref[...])
                     + _sigmoid(_dot(u[r], wgb[:, D_MODEL:2 * D_MODEL])) * _dot(ssm[r], ws_ref[...])
                     + _sigmoid(_dot(u[r], wgb[:, 2 * D_MODEL:3 * D_MODEL])) * _dot(four[r], wf_ref[...])
                     ).astype(BF16)

    def finish(i):
        r = slice(i * rb, (i + 1) * rb)
        out = _layer_norm_rows(res[r] + g1[r] * _dot(merged.pop(i), wo_ref[...]), lg_ref[...], lb_ref[...])
        o_ref[i * bb:(i + 1) * bb] = out.reshape(bb, steps, D_MODEL)

    for i in range(MERGE_ROW_BLOCKS + 1):
        if i < MERGE_ROW_BLOCKS:
            project(i)
        if i >= 1:
            finish(i - 1)


def _merge_call(x, attn, four, u_tm, yf, yb, mod3, w_in, d_skip, w_glu, b_glu, w_ba, w_bs, w_bf, w_out, ln_g, ln_b,
                layer, batch, seq_len):
    steps = ROW_TILE // batch
    seq = lambda a: a.reshape(batch, seq_len, a.shape[-1])
    seq_spec = lambda w: pl.BlockSpec((batch, steps, w), lambda i: (0, i, 0))
    tm_spec = pl.BlockSpec((ROW_TILE, SSM_W), lambda i: (i, 0))
    mod = lambda ch: pl.BlockSpec((mod3.shape[0], 1, D_MODEL), lambda i: (0, 0, ch))
    lay = lambda a: _layer_spec(a, layer, 1)
    out = pl.pallas_call(
        _merge_kernel,
        grid=(seq_len // steps,),
        in_specs=[seq_spec(D_MODEL), seq_spec(ATTN_W), seq_spec(FOURIER_W), tm_spec, tm_spec, tm_spec,
                  lay(d_skip), lay(w_glu), lay(b_glu), mod(1), mod(0), mod(2),
                  _weight_slab_spec(layer, OFF_G, GATE_W), lay(w_ba), lay(w_bs), lay(w_bf), lay(w_out), lay(ln_g),
                  lay(ln_b)],
        out_specs=seq_spec(D_MODEL),
        out_shape=jax.ShapeDtypeStruct((batch, seq_len, D_MODEL), F32),
        scratch_shapes=[pltpu.VMEM((D_MODEL, GATE_W), BF16)],
        compiler_params=_params("arbitrary"),
        name="merge",
    )(seq(x), seq(attn), seq(four), u_tm, yf, yb, d_skip, w_glu, b_glu, mod3, mod3, mod3, w_in, w_ba, w_bs, w_bf,
      w_out, ln_g, ln_b)
    return out.reshape(batch * seq_len, D_MODEL)


def _ffn_kernel(xp_ref, x_ref, xn_ref, sc_ref, sh_ref, g2_ref, wu_ref, cw_ref, cb_ref, wd_ref, lg_ref, lb_ref, o_ref,
                *, tiles_per_seq):
    pos = pl.program_id(0) % tiles_per_seq
    keep_prev = jnp.where(pos == 0, 0.0, 1.0)
    keep_next = jnp.where(pos == tiles_per_seq - 1, 0.0, 1.0)
    mod = lambda v: v * (1.0 + sc_ref[...]) + sh_ref[...]
    x = x_ref[...]
    u = jnp.concatenate([mod(xp_ref[...]) * keep_prev, mod(x), mod(xn_ref[...]) * keep_next], axis=0).astype(BF16)
    rows = u.shape[0]
    tile = x.shape[0]
    inner = slice(SUBLANES, SUBLANES + tile)
    n_chunks = D_FF // FFN_CHUNK

    def up(c):
        ca = slice(c * FFN_CHUNK, (c + 1) * FFN_CHUNK)
        cb = slice(D_FF + c * FFN_CHUNK, D_FF + (c + 1) * FFN_CHUNK)
        return _dot(u, wu_ref[:, ca]), _dot(u, wu_ref[:, cb])

    def conv(h, col):
        prev = pltpu.roll(h, 1, axis=0)[inner]
        nxt = pltpu.roll(h, rows - 1, axis=0)[inner]
        return prev * cw_ref[0:1, col] + h[inner] * cw_ref[1:2, col] + nxt * cw_ref[2:3, col] + cb_ref[:, col]

    acc = jnp.zeros((tile, D_MODEL), F32)
    hidden = {c: up(c) for c in range(min(FFN_LEAD, n_chunks))}
    for c in range(n_chunks):
        if c + FFN_LEAD < n_chunks:
            hidden[c + FFN_LEAD] = up(c + FFN_LEAD)
        ha, hb = hidden.pop(c)
        ca = slice(c * FFN_CHUNK, (c + 1) * FFN_CHUNK)
        a = conv(ha, ca)
        b = conv(hb, slice(D_FF + c * FFN_CHUNK, D_FF + (c + 1) * FFN_CHUNK))
        gated = ((a * _sigmoid(a)) * b).astype(BF16)
        acc = acc + _dot(gated, wd_ref[ca, :])
    y = ALPHA * x + g2_ref[...] * acc
    o_ref[...] = _layer_norm_rows(y, lg_ref[...], lb_ref[...])


def _ffn_call(x, mod3, w_up, conv_w, conv_b, w_down, ln_g, ln_b, layer, seq_len):
    rows = x.shape[0]
    tile = FFN_ROWS
    tps = seq_len // tile
    halo_per_tile = tile // SUBLANES
    n_halo = rows // SUBLANES
    per_seq = mod3.shape[0] > 1
    mod = lambda ch: pl.BlockSpec((None, 1, D_MODEL), lambda i: ((i // tps) if per_seq else 0, 0, ch))
    res = lambda a: pl.BlockSpec((None,) + a.shape[1:], lambda i: (layer,) + (0,) * (a.ndim - 1),
                                 pipeline_mode=pl.Buffered(1))
    return pl.pallas_call(
        functools.partial(_ffn_kernel, tiles_per_seq=tps),
        grid=(rows // tile,),
        in_specs=[pl.BlockSpec((SUBLANES, D_MODEL), lambda i: (jnp.maximum(i * halo_per_tile - 1, 0), 0)),
                  pl.BlockSpec((tile, D_MODEL), lambda i: (i, 0)),
                  pl.BlockSpec((SUBLANES, D_MODEL), lambda i: (jnp.minimum((i + 1) * halo_per_tile, n_halo - 1), 0)),
                  mod(4), mod(3), mod(5), res(w_up), res(conv_w), res(conv_b), res(w_down), res(ln_g), res(ln_b)],
        out_specs=pl.BlockSpec((tile, D_MODEL), lambda i: (i, 0)),
        out_shape=jax.ShapeDtypeStruct((rows, D_MODEL), F32),
        compiler_params=_params("parallel"),
        name="conv_mlp",
    )(x, x, x, mod3, mod3, mod3, w_up, conv_w, conv_b, w_down, ln_g, ln_b)


def _rope_tables(n_tokens):
    rows = n_tokens // GRID_W
    row = np.repeat(np.arange(rows, dtype=np.float64), GRID_W)
    col = np.tile(np.arange(GRID_W, dtype=np.float64), rows)
    freqs = ROPE_THETA ** (-np.arange(ROPE_AXIS_PAIRS, dtype=np.float64) / ROPE_AXIS_PAIRS)
    ang = np.concatenate([row[:, None] * freqs, col[:, None] * freqs], axis=-1)
    cos = np.repeat(np.cos(ang), 2, axis=-1)
    sin = np.repeat(np.sin(ang), 2, axis=-1) * np.tile(np.array([-1.0, 1.0]), HEAD_DIM // 2)
    return (jnp.asarray(np.tile(cos, (1, N_Q_HEADS)), F32), jnp.asarray(np.tile(sin, (1, N_Q_HEADS)), F32))


def _trunk_layer(x, mod3, w, layer, batch, seq_len, rope_tabs, cache_k, cache_v, h0, kv_dtype):
    q, k, v, s_tm, f_in = _inproj_call(x, mod3, w["w_in"], w["gq"], w["gk"], w["ones"], rope_tabs, layer,
                                       batch, seq_len, kv_dtype)
    attn = _attn_call(q, k, v, cache_k, cache_v, layer, batch, seq_len)
    four = _fourier_call(f_in, batch, seq_len)
    yf, yb, hfin_f, hfin_b = _s5_calls(s_tm, batch, seq_len, layer, w["s5"], h0)
    x1 = _merge_call(x, attn, four, s_tm, yf, yb, mod3, w["w_in"], w["d"], w["w_glu"], w["b_glu"], w["w_ba"],
                     w["w_bs"], w["w_bf"], w["w_out"], w["ln1_g"], w["ln1_b"], layer, batch, seq_len)
    x2 = _ffn_call(x1, mod3, w["w_up"], w["conv_w"], w["conv_b"], w["w_down"], w["ln2_g"], w["ln2_b"], layer, seq_len)
    return x2, k, v, hfin_f, hfin_b


def _split_state(h):
    b = h.shape[0]
    re = h[..., 0].reshape(b, 2, N_STATES)
    im = h[..., 1].reshape(b, 2, N_STATES)
    return jnp.concatenate([re, im], axis=-1).transpose(1, 0, 2)


def _join_state(hf, hb):
    def one(h):
        b = h.shape[0]
        re = h[:, 0:N_STATES].reshape(b, N_SSM_GROUPS, STATE_P)
        im = h[:, N_STATES:].reshape(b, N_SSM_GROUPS, STATE_P)
        return jnp.stack([re, im], axis=-1)
    return jnp.stack([one(hf), one(hb)], axis=1)


def kernel(x_prompt, x_sample, c, cache_k, cache_v, state_ssm, c_ctx, w_ada, b_ada, w_in, q_norm_g, k_norm_g, ssm_a_re, ssm_a_im, ssm_log_dt, ssm_b_re, ssm_b_im, ssm_c_re, ssm_c_im, ssm_d, w_glu, b_glu, w_br_attn, w_br_ssm, w_br_four, w_out, ln1_g, ln1_b, w_up, conv_w, conv_b, w_down, ln2_g, ln2_b):
    bc, lc, _ = x_prompt.shape
    bd, ld, _ = x_sample.shape
    past = cache_k.shape[2]

    cvec = jnp.zeros((MOD_ROWS, D_MODEL), F32).at[0:bd].set(c).at[bd].set(c_ctx)
    mod = _mod_call(cvec, w_ada, b_ada)

    s5w = _zoh_call(ssm_a_re, ssm_a_im, ssm_log_dt, ssm_b_re, ssm_b_im, ssm_c_re, ssm_c_im)

    head_mean = np.kron(np.eye(N_Q_HEADS), np.full((HEAD_DIM, HEAD_DIM), 1.0 / HEAD_DIM))
    rope_tabs = _rope_tables(ld)
    vec = lambda a: a.reshape(DEPTH, 1, -1)
    w = dict(
        w_in=w_in, gq=vec(jnp.tile(q_norm_g, (1, N_Q_HEADS))), gk=vec(jnp.tile(k_norm_g, (1, N_KV_HEADS))),
        ones=jnp.asarray(head_mean, F32).astype(BF16), s5=s5w, d=vec(ssm_d),
        w_glu=w_glu.astype(BF16), b_glu=vec(b_glu), w_ba=w_br_attn.astype(BF16), w_bs=w_br_ssm.astype(BF16),
        w_bf=w_br_four.astype(BF16), w_out=w_out.astype(BF16), ln1_g=vec(ln1_g), ln1_b=vec(ln1_b),
        w_up=w_up.astype(BF16), conv_w=conv_w, conv_b=vec(conv_b), w_down=w_down.astype(BF16), ln2_g=vec(ln2_g),
        ln2_b=vec(ln2_b))

    h = x_prompt.reshape(bc * lc, D_MODEL)
    h0_ctx = jnp.zeros((2, bc, 2 * N_STATES), F32)
    ks, vs, ss = [], [], []
    for l in range(DEPTH):
        mod_ctx = mod[l, bd:bd + 1].reshape(1, 1, 6 * D_MODEL)
        h, k_l, v_l, hf, hb = _trunk_layer(h, mod_ctx, w, l, bc, lc, None, None, None, h0_ctx, F32)
        ks.append(k_l.reshape(bc, lc, N_KV_HEADS, HEAD_DIM))
        vs.append(v_l.reshape(bc, lc, N_KV_HEADS, HEAD_DIM))
        ss.append(_join_state(hf, hb))
    new_cache_k = jnp.stack(ks, axis=1)
    new_cache_v = jnp.stack(vs, axis=1)
    new_state = jnp.stack(ss, axis=1)

    zx = x_sample.reshape(bd * ld, D_MODEL)
    ck = cache_k.reshape(bd, DEPTH, past, KV_W).astype(BF16)
    cv = cache_v.reshape(bd, DEPTH, past, KV_W).astype(BF16)
    for l in range(DEPTH):
        mod_dec = mod[l, 0:bd].reshape(bd, 1, 6 * D_MODEL)
        zx, _, _, _, _ = _trunk_layer(zx, mod_dec, w, l, bd, ld, rope_tabs, ck, cv,
                                      _split_state(state_ssm[:, l]), BF16)

    return (h.reshape(bc, lc, D_MODEL), zx.reshape(bd, ld, D_MODEL), new_cache_k, new_cache_v, new_state)
```

```python
import functools
import math

import jax
import jax.numpy as jnp
import numpy as np
from jax import lax
from jax.experimental import pallas as pl
from jax.experimental.pallas import tpu as pltpu

F32 = jnp.float32
BF16 = jnp.bfloat16

D_MODEL = 1024
DEPTH = 2
GRID_W = 64
HEAD_DIM = 64
N_Q_HEADS = 8
N_KV_HEADS = 2
GQA_GROUP = N_Q_HEADS // N_KV_HEADS
ATTN_W = N_Q_HEADS * HEAD_DIM
KV_W = N_KV_HEADS * HEAD_DIM
ROPE_AXIS_PAIRS = HEAD_DIM // 4
ROPE_THETA = 10000.0
SSM_W = D_MODEL // 2
SSM_GROUP = 16
N_SSM_GROUPS = SSM_W // SSM_GROUP
STATE_P = 64
N_STATES = N_SSM_GROUPS * STATE_P
FOURIER_W = D_MODEL // 2
FOURIER_GROUPS = 4
FOURIER_GW = FOURIER_W // FOURIER_GROUPS
N_BRANCH = 3
GATE_W = N_BRANCH * D_MODEL
D_IN = ATTN_W + 2 * KV_W + SSM_W + FOURIER_W + GATE_W
OFF_K = ATTN_W
OFF_V = ATTN_W + KV_W
OFF_S = ATTN_W + 2 * KV_W
OFF_F = OFF_S + SSM_W
OFF_G = OFF_F + FOURIER_W
D_FF = ((8 * D_MODEL // 3 + 127) // 128) * 128
EPS = 1e-6
ALPHA = (2 * DEPTH) ** 0.25

V7X_VMEM_BYTES = 64 * 1024 * 1024
VMEM_LIMIT_BYTES = V7X_VMEM_BYTES - 8 * 1024 * 1024
LANES = 128
SUBLANES = 8
SSM_BUNDLE_GROUPS = LANES // SSM_GROUP
SSM_BUNDLES = N_SSM_GROUPS // SSM_BUNDLE_GROUPS
BUNDLE_STATES = SSM_BUNDLE_GROUPS * STATE_P

ROW_TILE = 512
MERGE_ROW_BLOCKS = 2
ATTN_Q_ROWS = 512
FFN_ROWS = 256
FFN_CHUNK = 256
FFN_LEAD = 2
SCAN_ROWS = 1024
SCAN_LANES = 512
MOD_COLS = 1536
MOD_ROWS = 16


def _params(*sem):
    return pltpu.CompilerParams(dimension_semantics=sem, vmem_limit_bytes=VMEM_LIMIT_BYTES)


def _dot(a, b):
    return jnp.dot(a, b, preferred_element_type=F32)


def _sigmoid(x):
    return 1.0 / (1.0 + jnp.exp(-x))


def _layer_spec(arr, layer, n_grid):
    zeros = (0,) * (arr.ndim - 1)
    if n_grid == 1:
        return pl.BlockSpec((None,) + arr.shape[1:], lambda i: (layer,) + zeros)
    return pl.BlockSpec((None,) + arr.shape[1:], lambda i, j: (layer,) + zeros)


def _mod_kernel(c_ref, w_ref, b_ref, o_ref):
    c = c_ref[...]
    s = (c * _sigmoid(c)).astype(BF16)
    o_ref[...] = _dot(s, w_ref[...].astype(BF16)) + b_ref[...]


def _mod_call(cvec, w_ada, b_ada):
    n = 6 * D_MODEL
    return pl.pallas_call(
        _mod_kernel,
        grid=(DEPTH, n // MOD_COLS),
        in_specs=[
            pl.BlockSpec((MOD_ROWS, D_MODEL), lambda l, j: (0, 0)),
            pl.BlockSpec((None, D_MODEL, MOD_COLS), lambda l, j: (l, 0, j)),
            pl.BlockSpec((None, 1, MOD_COLS), lambda l, j: (l, 0, j)),
        ],
        out_specs=pl.BlockSpec((None, MOD_ROWS, MOD_COLS), lambda l, j: (l, 0, j)),
        out_shape=jax.ShapeDtypeStruct((DEPTH, MOD_ROWS, n), F32),
        compiler_params=_params("parallel", "parallel"),
        name="mod",
    )(cvec, w_ada, b_ada.reshape(DEPTH, 1, n))


def _swap_pairs(x):
    w = x.shape[-1]
    nxt = pltpu.roll(x, w - 1, axis=1)
    prv = pltpu.roll(x, 1, axis=1)
    lane = lax.broadcasted_iota(jnp.int32, x.shape, 1)
    return jnp.where((lane & 1) == 0, nxt, prv)


def _weight_slab_spec(layer, col0, width):
    return pl.BlockSpec((pl.Element(1), pl.Element(D_MODEL), pl.Element(width)), lambda i: (layer, 0, col0),
                        pipeline_mode=pl.Buffered(1))


def _inproj_kernel(*refs, rope):
    if rope:
        (x_ref, sc_ref, sh_ref, w_ref, gq_ref, gk_ref, ones_ref, cos_ref, sin_ref,
         q_ref, k_ref, v_ref, s_ref, f_ref, wbf) = refs
    else:
        (x_ref, sc_ref, sh_ref, w_ref, gq_ref, gk_ref, ones_ref,
         q_ref, k_ref, v_ref, s_ref, f_ref, wbf) = refs
    batch, steps, _ = x_ref.shape
    rows = batch * steps

    @pl.when(pl.program_id(0) == 0)
    def _():
        wbf[...] = w_ref[0].astype(BF16)

    u = (x_ref[...] * (1.0 + sc_ref[...]) + sh_ref[...]).reshape(rows, D_MODEL).astype(BF16)

    zq = _dot(u, wbf[:, 0:OFF_K])
    zk = _dot(u, wbf[:, OFF_K:OFF_V])
    zv = _dot(u, wbf[:, OFF_V:OFF_S])
    msq_q = _dot((zq * zq).astype(BF16), ones_ref[...])
    msq_k = _dot((zk * zk).astype(BF16), ones_ref[0:KV_W, 0:KV_W])
    q = zq * lax.rsqrt(msq_q + EPS) * gq_ref[...]
    k = zk * lax.rsqrt(msq_k + EPS) * gk_ref[...]
    if rope:
        cos = jnp.concatenate([cos_ref[...]] * batch, axis=0)
        sin = jnp.concatenate([sin_ref[...]] * batch, axis=0)
        q = q * cos + _swap_pairs(q) * sin
        k = k * cos[:, 0:KV_W] + _swap_pairs(k) * sin[:, 0:KV_W]
    q_ref[...] = (q * (HEAD_DIM ** -0.5)).reshape(batch, steps, ATTN_W).astype(q_ref.dtype)
    k_ref[...] = k.reshape(batch, steps, KV_W).astype(k_ref.dtype)
    v_ref[...] = zv.reshape(batch, steps, KV_W).astype(v_ref.dtype)
    f_ref[...] = _dot(u, wbf[:, OFF_F:OFF_G]).reshape(batch, steps, FOURIER_W).astype(f_ref.dtype)
    s = _dot(u, wbf[:, OFF_S:OFF_F]).reshape(batch, steps, SSM_W)
    s_ref[...] = jnp.swapaxes(s, 0, 1).reshape(rows, SSM_W).astype(s_ref.dtype)


def _inproj_call(x, mod3, w_in, gq, gk, ones_blk, rope_tabs, layer, batch, seq_len, kv_dtype):
    steps = ROW_TILE // batch
    rope = rope_tabs is not None
    x3 = x.reshape(batch, seq_len, D_MODEL)
    seq = lambda w: pl.BlockSpec((batch, steps, w), lambda i: (0, i, 0))
    mod = lambda ch: pl.BlockSpec((mod3.shape[0], 1, D_MODEL), lambda i: (0, 0, ch))
    in_specs = [seq(D_MODEL), mod(1), mod(0), _weight_slab_spec(layer, 0, OFF_G),
                _layer_spec(gq, layer, 1), _layer_spec(gk, layer, 1), pl.BlockSpec(ones_blk.shape, lambda i: (0, 0))]
    args = [x3, mod3, mod3, w_in, gq, gk, ones_blk]
    if rope:
        tab = pl.BlockSpec((steps, ATTN_W), lambda i: (i, 0))
        in_specs += [tab, tab]
        args += list(rope_tabs)
    widths = (ATTN_W, KV_W, KV_W, FOURIER_W)
    dtypes = (BF16, kv_dtype, kv_dtype, BF16)
    outs = pl.pallas_call(
        functools.partial(_inproj_kernel, rope=rope),
        grid=(seq_len // steps,),
        in_specs=in_specs,
        out_specs=[seq(w) for w in widths[:3]] + [pl.BlockSpec((ROW_TILE, SSM_W), lambda i: (i, 0)), seq(FOURIER_W)],
        out_shape=([jax.ShapeDtypeStruct((batch, seq_len, w), d) for w, d in zip(widths[:3], dtypes[:3])]
                   + [jax.ShapeDtypeStruct((seq_len * batch, SSM_W), F32),
                      jax.ShapeDtypeStruct((batch, seq_len, FOURIER_W), BF16)]),
        scratch_shapes=[pltpu.VMEM((D_MODEL, OFF_G), BF16)],
        compiler_params=_params("arbitrary"),
        name="in_proj",
    )(*args)
    q, k, v, s_tm, f = outs
    flat = lambda a: a.reshape(batch * seq_len, a.shape[-1])
    return flat(q), flat(k), flat(v), s_tm, flat(f)


def _attn_kernel(*refs, has_cache):
    if has_cache:
        q_ref, k_ref, v_ref, ck_ref, cv_ref, o_ref = refs
    else:
        q_ref, k_ref, v_ref, o_ref = refs
    q = q_ref[...].astype(F32)
    contract_last = (((1,), (1,)), ((), ()))
    kv = lambda h: slice((h // GQA_GROUP) * HEAD_DIM, (h // GQA_GROUP + 1) * HEAD_DIM)
    scores = {}
    outs = []

    def score(h):
        qh = q[:, h * HEAD_DIM:(h + 1) * HEAD_DIM].astype(BF16)
        s1 = lax.dot_general(qh, k_ref[:, kv(h)].astype(BF16), contract_last, preferred_element_type=F32)
        s2 = None
        if has_cache:
            s2 = lax.dot_general(qh, ck_ref[:, kv(h)], contract_last, preferred_element_type=F32)
        scores[h] = (s1, s2)

    def attend(h):
        s1, s2 = scores.pop(h)
        m = jnp.max(s1, axis=-1, keepdims=True)
        if has_cache:
            m = jnp.maximum(m, jnp.max(s2, axis=-1, keepdims=True))
        p1 = jnp.exp(s1 - m)
        den = jnp.sum(p1, axis=-1, keepdims=True)
        o = _dot(p1.astype(BF16), v_ref[:, kv(h)].astype(BF16))
        if has_cache:
            p2 = jnp.exp(s2 - m)
            den = den + jnp.sum(p2, axis=-1, keepdims=True)
            o = o + _dot(p2.astype(BF16), cv_ref[:, kv(h)])
        outs.append(o * (1.0 / den))

    for h in range(N_Q_HEADS + 1):
        if h < N_Q_HEADS:
            score(h)
        if h >= 1:
            attend(h - 1)
    o_ref[...] = jnp.concatenate(outs, axis=1).astype(o_ref.dtype)


def _attn_call(q, k, v, cache_k, cache_v, layer, batch, seq_len):
    rows = q.shape[0]
    qb = min(ATTN_Q_ROWS, seq_len)
    nq = seq_len // qb
    has_cache = cache_k is not None
    in_specs = [pl.BlockSpec((qb, ATTN_W), lambda b, i: (b * nq + i, 0)),
                pl.BlockSpec((seq_len, KV_W), lambda b, i: (b, 0)),
                pl.BlockSpec((seq_len, KV_W), lambda b, i: (b, 0))]
    args = [q, k, v]
    if has_cache:
        past = cache_k.shape[2]
        in_specs += [pl.BlockSpec((None, None, past, KV_W), lambda b, i: (b, layer, 0, 0))] * 2
        args += [cache_k, cache_v]
    return pl.pallas_call(
        functools.partial(_attn_kernel, has_cache=has_cache),
        grid=(batch, nq),
        in_specs=in_specs,
        out_specs=pl.BlockSpec((qb, ATTN_W), lambda b, i: (b * nq + i, 0)),
        out_shape=jax.ShapeDtypeStruct((rows, ATTN_W), BF16),
        compiler_params=_params("parallel", "parallel"),
        name="attention",
    )(*args)


def _fourier_kernel(f_ref, wc_ref, wl_ref, o_ref):
    f = f_ref[...]
    parts = [_dot(f[:, g * FOURIER_GW:(g + 1) * FOURIER_GW], wc_ref[...]) for g in range(FOURIER_GROUPS)]
    xc = jnp.concatenate([p[:, 0:FOURIER_GW] for p in parts], axis=1)
    xs = jnp.concatenate([p[:, FOURIER_GW:2 * FOURIER_GW] for p in parts], axis=1)
    stacked = jnp.concatenate([xc, xs], axis=0).astype(BF16)
    o_ref[...] = _dot(wl_ref[...], stacked).astype(o_ref.dtype)


def _dft_tables(seq_len):
    c = np.arange(FOURIER_GW)
    ang_c = 2.0 * np.pi * np.outer(c, c) / FOURIER_GW
    wc = np.concatenate([np.cos(ang_c), np.sin(ang_c)], axis=1) / math.sqrt(FOURIER_GW)
    t = np.arange(seq_len)
    ang_l = 2.0 * np.pi * np.outer(t, t) / seq_len
    wl = np.concatenate([np.cos(ang_l), -np.sin(ang_l)], axis=1) / math.sqrt(seq_len)
    return (jnp.asarray(wc, F32).astype(BF16), jnp.asarray(wl, F32).astype(BF16))


def _fourier_call(f, batch, seq_len):
    wc, wl = _dft_tables(seq_len)
    return pl.pallas_call(
        _fourier_kernel,
        grid=(batch,),
        in_specs=[pl.BlockSpec((seq_len, FOURIER_W), lambda b: (b, 0)),
                  pl.BlockSpec(wc.shape, lambda b: (0, 0)),
                  pl.BlockSpec(wl.shape, lambda b: (0, 0))],
        out_specs=pl.BlockSpec((seq_len, FOURIER_W), lambda b: (b, 0)),
        out_shape=jax.ShapeDtypeStruct(f.shape, BF16),
        compiler_params=_params("parallel"),
        name="fourier",
    )(f, wc, wl)


def _block_diagonal(compact, group_rows, group_cols):
    rows, w = compact.shape
    log2 = lambda n: n.bit_length() - 1
    sel = (lax.broadcasted_iota(jnp.int32, (w, SSM_BUNDLE_GROUPS * w), 0)
           == (lax.broadcasted_iota(jnp.int32, (w, SSM_BUNDLE_GROUPS * w), 1) & (w - 1)))
    wide = _dot(compact.astype(BF16), jnp.where(sel, 1.0, 0.0).astype(BF16))
    row_group = (lax.broadcasted_iota(jnp.int32, wide.shape, 0) >> log2(group_rows)) & (SSM_BUNDLE_GROUPS - 1)
    col_group = lax.broadcasted_iota(jnp.int32, wide.shape, 1) >> log2(group_cols)
    return jnp.where(row_group == col_group, wide, 0.0)


def _zoh_lambda(a_re, a_im, log_dt):
    dt = jnp.exp(log_dt)
    mag = jnp.exp(a_re * dt)
    ang = a_im * dt
    return mag * jnp.cos(ang), mag * jnp.sin(ang)


def _zoh_kernel(are_ref, aim_ref, ldt_ref, bre_ref, bim_ref, cre_ref, cim_ref,
                l2re_ref, l2im_ref, bw_ref, cw_ref, k0_ref):
    a_re = are_ref[...]
    a_im = aim_ref[...]
    lb_re, lb_im = _zoh_lambda(a_re, a_im, ldt_ref[...])
    den = a_re * a_re + a_im * a_im
    n_re = lb_re - 1.0
    n_im = lb_im
    f_re = (n_re * a_re + n_im * a_im) / den
    f_im = (n_im * a_re - n_re * a_im) / den
    l2re_ref[...] = lb_re * lb_re - lb_im * lb_im
    l2im_ref[...] = 2.0 * lb_re * lb_im
    b_re = bre_ref[...]
    b_im = bim_ref[...]
    bb_re = f_re[:, None, :] * b_re - f_im[:, None, :] * b_im
    bb_im = f_re[:, None, :] * b_im + f_im[:, None, :] * b_re
    lbb_re = lb_re[:, None, :] * bb_re - lb_im[:, None, :] * bb_im
    lbb_im = lb_re[:, None, :] * bb_im + lb_im[:, None, :] * bb_re
    flat = lambda a: a.reshape(a.shape[0] * a.shape[1], a.shape[2])
    n_slabs = bw_ref.shape[0]
    in_slab = lambda a: _block_diagonal(flat(a), SSM_GROUP, STATE_P).reshape(n_slabs, LANES, BUNDLE_STATES)
    bd_re, bd_im = in_slab(bb_re), in_slab(bb_im)
    bw_ref[:, 0, :, 0:BUNDLE_STATES] = in_slab(lbb_re).astype(BF16)
    bw_ref[:, 0, :, BUNDLE_STATES:] = in_slab(lbb_im).astype(BF16)
    bw_ref[:, 1, :, 0:BUNDLE_STATES] = bd_re.astype(BF16)
    bw_ref[:, 1, :, BUNDLE_STATES:] = bd_im.astype(BF16)
    c_re = cre_ref[...]
    c_im = cim_ref[...]
    cl_re = c_re * lb_re[:, None, :] - c_im * lb_im[:, None, :]
    cl_im = c_re * lb_im[:, None, :] + c_im * lb_re[:, None, :]
    cd_re, cd_im, cld_re, cld_im = in_slab(c_re), in_slab(c_im), in_slab(cl_re), in_slab(cl_im)
    for s in range(n_slabs):
        t_re, t_im = cd_re[s].T, cd_im[s].T
        cw_ref[s, 0, :, 0:LANES] = t_re.astype(BF16)
        cw_ref[s, 0, :, LANES:] = cld_re[s].T.astype(BF16)
        cw_ref[s, 1, :, 0:LANES] = (-t_im).astype(BF16)
        cw_ref[s, 1, :, LANES:] = (-cld_im[s].T).astype(BF16)
        k0 = _dot(bd_re[s].astype(BF16), t_re.astype(BF16)) - _dot(bd_im[s].astype(BF16), t_im.astype(BF16))
        k0_ref[s] = k0.astype(BF16)


def _zoh_call(a_re, a_im, log_dt, b_re, b_im, c_re, c_im):
    rows = DEPTH * 2 * N_SSM_GROUPS
    n_dirs = DEPTH * 2
    are = a_re.reshape(rows, STATE_P)
    aim = a_im.reshape(rows, STATE_P)
    ldt = log_dt.reshape(rows, 1)
    bre = jnp.swapaxes(b_re, -1, -2).reshape(rows, SSM_GROUP, STATE_P)
    bim = jnp.swapaxes(b_im, -1, -2).reshape(rows, SSM_GROUP, STATE_P)
    cre = c_re.reshape(rows, SSM_GROUP, STATE_P)
    cim = c_im.reshape(rows, SSM_GROUP, STATE_P)
    args = (are, aim, ldt, bre, bim, cre, cim)
    per_dir = lambda shape: pl.BlockSpec((shape[0] // n_dirs,) + tuple(shape[1:]),
                                         lambda i: (i,) + (0,) * (len(shape) - 1))
    n_slabs = n_dirs * SSM_BUNDLES
    outs = [jax.ShapeDtypeStruct((rows, STATE_P), F32)] * 2 + [
        jax.ShapeDtypeStruct((n_slabs, 2, LANES, 2 * BUNDLE_STATES), BF16),
        jax.ShapeDtypeStruct((n_slabs, 2, BUNDLE_STATES, 2 * LANES), BF16),
        jax.ShapeDtypeStruct((n_slabs, LANES, LANES), BF16)]
    l2_re, l2_im, bw, cw, k0 = pl.pallas_call(
        _zoh_kernel,
        grid=(n_dirs,),
        in_specs=[per_dir(a.shape) for a in args],
        out_specs=[per_dir(o.shape) for o in outs],
        out_shape=outs,
        compiler_params=_params("parallel"),
        name="s5_zoh",
    )(*args)
    lead = (DEPTH, 2, SSM_BUNDLES)
    return (l2_re.reshape(DEPTH, 2, 1, N_STATES), l2_im.reshape(DEPTH, 2, 1, N_STATES),
            bw.reshape(*lead, 2 * LANES, 2 * BUNDLE_STATES), cw.reshape(*lead, 2 * BUNDLE_STATES, 2 * LANES),
            k0.reshape(*lead, LANES, LANES))


def _state_slabs(h, k):
    return (h[:, k * BUNDLE_STATES:(k + 1) * BUNDLE_STATES],
            h[:, N_STATES + k * BUNDLE_STATES:N_STATES + (k + 1) * BUNDLE_STATES])


def _scan_project_state(h_re, h_im, cw_ref, k):
    return (_dot(h_re.astype(BF16), cw_ref[k, 0:BUNDLE_STATES, :])
            + _dot(h_im.astype(BF16), cw_ref[k, BUNDLE_STATES:, :]))


def _scan_init(h0_ref, cw_ref, hc, wp):
    hc[...] = h0_ref[...]
    for k in range(SSM_BUNDLES):
        h_re, h_im = _state_slabs(h0_ref, k)
        wp[:, k * LANES:(k + 1) * LANES] = _scan_project_state(h_re, h_im, cw_ref, k)[:, LANES:]


def _scan_direction(u_ref, bw_ref, cw_ref, k0_ref, l2re_ref, l2im_ref, y_ref, hs, hc, wp, *, batch, reverse):
    rows = u_ref.shape[0]
    pairs = rows // (2 * batch)
    half = pairs * batch
    u4 = u_ref[...].reshape(pairs, 2, batch, SSM_W)
    early, late = u4[:, 0].reshape(half, SSM_W), u4[:, 1].reshape(half, SSM_W)
    first, second = (late, early) if reverse else (early, late)
    first, second = first.astype(BF16), second.astype(BF16)

    for k in range(SSM_BUNDLES):
        slab = slice(k * LANES, (k + 1) * LANES)
        r = _dot(jnp.concatenate([first[:, slab], second[:, slab]], axis=1), bw_ref[k])
        hs[:, k * BUNDLE_STATES:(k + 1) * BUNDLE_STATES] = r[:, 0:BUNDLE_STATES]
        hs[:, N_STATES + k * BUNDLE_STATES:N_STATES + (k + 1) * BUNDLE_STATES] = r[:, BUNDLE_STATES:]

    for blk in range(N_STATES // SCAN_LANES):
        re = slice(blk * SCAN_LANES, (blk + 1) * SCAN_LANES)
        im = slice(N_STATES + blk * SCAN_LANES, N_STATES + (blk + 1) * SCAN_LANES)
        lr = jnp.broadcast_to(l2re_ref[:, re], (batch, SCAN_LANES))
        li = jnp.broadcast_to(l2im_ref[:, re], (batch, SCAN_LANES))
        hr, hi = hc[:, re], hc[:, im]
        for j in range(pairs):
            jj = pairs - 1 - j if reverse else j
            prow = slice(jj * batch, (jj + 1) * batch)
            hr, hi = (lr * hr - li * hi + hs[prow, re], lr * hi + li * hr + hs[prow, im])
            hs[prow, re] = hr
            hs[prow, im] = hi
        hc[:, re] = hr
        hc[:, im] = hi

    y_second, w, local = [], [], []
    for k in range(SSM_BUNDLES):
        h_re, h_im = _state_slabs(hs, k)
        z = _scan_project_state(h_re, h_im, cw_ref, k)
        y_second.append(z[:, 0:LANES])
        w.append(z[:, LANES:])
        local.append(_dot(first[:, k * LANES:(k + 1) * LANES], k0_ref[k]))
    y_second = jnp.concatenate(y_second, axis=1)
    w = jnp.concatenate(w, axis=1)
    if reverse:
        w_prev = jnp.concatenate([w[batch:], wp[...]], axis=0)
        wp[...] = w[0:batch]
    else:
        w_prev = jnp.concatenate([wp[...], w[0:half - batch]], axis=0)
        wp[...] = w[half - batch:]
    y_first = w_prev + jnp.concatenate(local, axis=1)
    y_early, y_late = (y_second, y_first) if reverse else (y_first, y_second)
    y = jnp.concatenate([y_early.reshape(pairs, 1, batch, SSM_W), y_late.reshape(pairs, 1, batch, SSM_W)], axis=1)
    y_ref[...] = y.reshape(rows, SSM_W).astype(y_ref.dtype)


def _s5_scan_kernel(uf_ref, ub_ref, bwf_ref, cwf_ref, k0f_ref, l2rf_ref, l2if_ref, bwb_ref, cwb_ref, k0b_ref,
                    l2rb_ref, l2ib_ref, h0f_ref, h0b_ref, yf_ref, yb_ref, hff_ref, hfb_ref,
                    hsf, hsb, hcf, hcb, wpf, wpb, *, batch):
    @pl.when(pl.program_id(0) == 0)
    def _():
        _scan_init(h0f_ref, cwf_ref, hcf, wpf)
        _scan_init(h0b_ref, cwb_ref, hcb, wpb)

    _scan_direction(uf_ref, bwf_ref, cwf_ref, k0f_ref, l2rf_ref, l2if_ref, yf_ref, hsf, hcf, wpf,
                    batch=batch, reverse=False)
    _scan_direction(ub_ref, bwb_ref, cwb_ref, k0b_ref, l2rb_ref, l2ib_ref, yb_ref, hsb, hcb, wpb,
                    batch=batch, reverse=True)
    hff_ref[...] = hcf[...]
    hfb_ref[...] = hcb[...]


def _gelu_tanh(x):
    return 0.5 * x * (1.0 + jnp.tanh(math.sqrt(2.0 / math.pi) * (x + 0.044715 * (x * x * x))))


def _s5_calls(u_tm, batch, seq_len, layer, s5w, h0):
    lam2_re, lam2_im, bw, cw, k0 = s5w
    rows = u_tm.shape[0]
    steps = SCAN_ROWS // batch
    n_chunks = seq_len // steps
    fwd_chunk = pl.BlockSpec((SCAN_ROWS, SSM_W), lambda i: (i, 0))
    bwd_chunk = pl.BlockSpec((SCAN_ROWS, SSM_W), lambda i: (n_chunks - 1 - i, 0))
    dir_spec = lambda a, d: pl.BlockSpec((None, None) + a.shape[2:], lambda i: (layer, d) + (0,) * (a.ndim - 2))
    weights = lambda d: [dir_spec(a, d) for a in (bw, cw, k0, lam2_re, lam2_im)]
    h0_spec = lambda d: pl.BlockSpec((None,) + h0.shape[1:], lambda i: (d, 0, 0))
    state_spec = pl.BlockSpec((batch, 2 * N_STATES), lambda i: (0, 0))
    state = jax.ShapeDtypeStruct((batch, 2 * N_STATES), F32)
    hs = lambda: pltpu.VMEM((SCAN_ROWS // 2, 2 * N_STATES), F32)
    hc = lambda: pltpu.VMEM((batch, 2 * N_STATES), F32)
    wp = lambda: pltpu.VMEM((batch, SSM_W), F32)

    yf, yb, hfin_f, hfin_b = pl.pallas_call(
        functools.partial(_s5_scan_kernel, batch=batch),
        grid=(n_chunks,),
        in_specs=[fwd_chunk, bwd_chunk] + weights(0) + weights(1) + [h0_spec(0), h0_spec(1)],
        out_specs=[fwd_chunk, bwd_chunk, state_spec, state_spec],
        out_shape=[jax.ShapeDtypeStruct((rows, SSM_W), BF16)] * 2 + [state, state],
        scratch_shapes=[hs(), hs(), hc(), hc(), wp(), wp()],
        compiler_params=_params("arbitrary"),
        name="s5_scan",
    )(u_tm, u_tm, bw, cw, k0, lam2_re, lam2_im, bw, cw, k0, lam2_re, lam2_im, h0, h0)

    return yf, yb, hfin_f, hfin_b


def _layer_norm_rows(y, g, b):
    mu = jnp.mean(y, axis=-1, keepdims=True)
    yc = y - mu
    var = jnp.mean(yc * yc, axis=-1, keepdims=True)
    return yc * lax.rsqrt(var + EPS) * g + b


def _merge_kernel(x_ref, a_ref, f_ref, us_ref, yf_ref, yb_ref, d_ref, wglu_ref, bglu_ref, sc_ref, sh_ref, g1_ref,
                  wg_ref, wa_ref, ws_ref, wf_ref, wo_ref, lg_ref, lb_ref, o_ref, wgb):
    batch, steps, _ = x_ref.shape
    rows = batch * steps

    @pl.when(pl.program_id(0) == 0)
    def _():
        wgb[...] = wg_ref[0].astype(BF16)

    y = d_ref[...] * us_ref[...] + yf_ref[...].astype(F32) + yb_ref[...].astype(F32)
    y = _gelu_tanh(y)
    y = y * _sigmoid(_dot(y.astype(BF16), wglu_ref[...]) + bglu_ref[...])
    ssm = jnp.swapaxes(y.reshape(steps, batch, SSM_W), 0, 1).reshape(rows, SSM_W).astype(BF16)

    x = x_ref[...]
    u = (x * (1.0 + sc_ref[...]) + sh_ref[...]).reshape(rows, D_MODEL).astype(BF16)
    res = (ALPHA * x).reshape(rows, D_MODEL)
    g1 = jnp.broadcast_to(g1_ref[...], (batch, steps, D_MODEL)).reshape(rows, D_MODEL)
    attn = a_ref[...].reshape(rows, ATTN_W)
    four = f_ref[...].reshape(rows, FOURIER_W)
    rb = rows // MERGE_ROW_BLOCKS
    bb = batch // MERGE_ROW_BLOCKS
    merged = {}

    def project(i):
        r = slice(i * rb, (i + 1) * rb)
        merged[i] = (_sigmoid(_dot(u[r], wgb[:, 0:D_MODEL])) * _dot(attn[r], wa_ref[...])
                     + _sigmoid(_dot(u[r], wgb[:, D_MODEL:2 * D_MODEL])) * _dot(ssm[r], ws_ref[...])
                     + _sigmoid(_dot(u[r], wgb[:, 2 * D_MODEL:3 * D_MODEL])) * _dot(four[r], wf_ref[...])
                     ).astype(BF16)

    def finish(i):
        r = slice(i * rb, (i + 1) * rb)
        out = _layer_norm_rows(res[r] + g1[r] * _dot(merged.pop(i), wo_ref[...]), lg_ref[...], lb_ref[...])
        o_ref[i * bb:(i + 1) * bb] = out.reshape(bb, steps, D_MODEL)

    for i in range(MERGE_ROW_BLOCKS + 1):
        if i < MERGE_ROW_BLOCKS:
            project(i)
        if i >= 1:
            finish(i - 1)


def _merge_call(x, attn, four, u_tm, yf, yb, mod3, w_in, d_skip, w_glu, b_glu, w_ba, w_bs, w_bf, w_out, ln_g, ln_b,
                layer, batch, seq_len):
    steps = ROW_TILE // batch
    seq = lambda a: a.reshape(batch, seq_len, a.shape[-1])
    seq_spec = lambda w: pl.BlockSpec((batch, steps, w), lambda i: (0, i, 0))
    tm_spec = pl.BlockSpec((ROW_TILE, SSM_W), lambda i: (i, 0))
    mod = lambda ch: pl.BlockSpec((mod3.shape[0], 1, D_MODEL), lambda i: (0, 0, ch))
    lay = lambda a: _layer_spec(a, layer, 1)
    out = pl.pallas_call(
        _merge_kernel,
        grid=(seq_len // steps,),
        in_specs=[seq_spec(D_MODEL), seq_spec(ATTN_W), seq_spec(FOURIER_W), tm_spec, tm_spec, tm_spec,
                  lay(d_skip), lay(w_glu), lay(b_glu), mod(1), mod(0), mod(2),
                  _weight_slab_spec(layer, OFF_G, GATE_W), lay(w_ba), lay(w_bs), lay(w_bf), lay(w_out), lay(ln_g),
                  lay(ln_b)],
        out_specs=seq_spec(D_MODEL),
        out_shape=jax.ShapeDtypeStruct((batch, seq_len, D_MODEL), F32),
        scratch_shapes=[pltpu.VMEM((D_MODEL, GATE_W), BF16)],
        compiler_params=_params("arbitrary"),
        name="merge",
    )(seq(x), seq(attn), seq(four), u_tm, yf, yb, d_skip, w_glu, b_glu, mod3, mod3, mod3, w_in, w_ba, w_bs, w_bf,
      w_out, ln_g, ln_b)
    return out.reshape(batch * seq_len, D_MODEL)


def _ffn_kernel(xp_ref, x_ref, xn_ref, sc_ref, sh_ref, g2_ref, wu_ref, cw_ref, cb_ref, wd_ref, lg_ref, lb_ref, o_ref,
                *, tiles_per_seq):
    pos = pl.program_id(0) % tiles_per_seq
    keep_prev = jnp.where(pos == 0, 0.0, 1.0)
    keep_next = jnp.where(pos == tiles_per_seq - 1, 0.0, 1.0)
    mod = lambda v: v * (1.0 + sc_ref[...]) + sh_ref[...]
    x = x_ref[...]
    u = jnp.concatenate([mod(xp_ref[...]) * keep_prev, mod(x), mod(xn_ref[...]) * keep_next], axis=0).astype(BF16)
    rows = u.shape[0]
    tile = x.shape[0]
    inner = slice(SUBLANES, SUBLANES + tile)
    n_chunks = D_FF // FFN_CHUNK

    def up(c):
        ca = slice(c * FFN_CHUNK, (c + 1) * FFN_CHUNK)
        cb = slice(D_FF + c * FFN_CHUNK, D_FF + (c + 1) * FFN_CHUNK)
        return _dot(u, wu_ref[:, ca]), _dot(u, wu_ref[:, cb])

    def conv(h, col):
        prev = pltpu.roll(h, 1, axis=0)[inner]
        nxt = pltpu.roll(h, rows - 1, axis=0)[inner]
        return prev * cw_ref[0:1, col] + h[inner] * cw_ref[1:2, col] + nxt * cw_ref[2:3, col] + cb_ref[:, col]

    acc = jnp.zeros((tile, D_MODEL), F32)
    hidden = {c: up(c) for c in range(min(FFN_LEAD, n_chunks))}
    for c in range(n_chunks):
        if c + FFN_LEAD < n_chunks:
            hidden[c + FFN_LEAD] = up(c + FFN_LEAD)
        ha, hb = hidden.pop(c)
        ca = slice(c * FFN_CHUNK, (c + 1) * FFN_CHUNK)
        a = conv(ha, ca)
        b = conv(hb, slice(D_FF + c * FFN_CHUNK, D_FF + (c + 1) * FFN_CHUNK))
        gated = ((a * _sigmoid(a)) * b).astype(BF16)
        acc = acc + _dot(gated, wd_ref[ca, :])
    y = ALPHA * x + g2_ref[...] * acc
    o_ref[...] = _layer_norm_rows(y, lg_ref[...], lb_ref[...])


def _ffn_call(x, mod3, w_up, conv_w, conv_b, w_down, ln_g, ln_b, layer, seq_len):
    rows = x.shape[0]
    tile = FFN_ROWS
    tps = seq_len // tile
    halo_per_tile = tile // SUBLANES
    n_halo = rows // SUBLANES
    per_seq = mod3.shape[0] > 1
    mod = lambda ch: pl.BlockSpec((None, 1, D_MODEL), lambda i: ((i // tps) if per_seq else 0, 0, ch))
    res = lambda a: pl.BlockSpec((None,) + a.shape[1:], lambda i: (layer,) + (0,) * (a.ndim - 1),
                                 pipeline_mode=pl.Buffered(1))
    return pl.pallas_call(
        functools.partial(_ffn_kernel, tiles_per_seq=tps),
        grid=(rows // tile,),
        in_specs=[pl.BlockSpec((SUBLANES, D_MODEL), lambda i: (jnp.maximum(i * halo_per_tile - 1, 0), 0)),
                  pl.BlockSpec((tile, D_MODEL), lambda i: (i, 0)),
                  pl.BlockSpec((SUBLANES, D_MODEL), lambda i: (jnp.minimum((i + 1) * halo_per_tile, n_halo - 1), 0)),
                  mod(4), mod(3), mod(5), res(w_up), res(conv_w), res(conv_b), res(w_down), res(ln_g), res(ln_b)],
        out_specs=pl.BlockSpec((tile, D_MODEL), lambda i: (i, 0)),
        out_shape=jax.ShapeDtypeStruct((rows, D_MODEL), F32),
        compiler_params=_params("parallel"),
        name="conv_mlp",
    )(x, x, x, mod3, mod3, mod3, w_up, conv_w, conv_b, w_down, ln_g, ln_b)


def _rope_tables(n_tokens):
    rows = n_tokens // GRID_W
    row = np.repeat(np.arange(rows, dtype=np.float64), GRID_W)
    col = np.tile(np.arange(GRID_W, dtype=np.float64), rows)
    freqs = ROPE_THETA ** (-np.arange(ROPE_AXIS_PAIRS, dtype=np.float64) / ROPE_AXIS_PAIRS)
    ang = np.concatenate([row[:, None] * freqs, col[:, None] * freqs], axis=-1)
    cos = np.repeat(np.cos(ang), 2, axis=-1)
    sin = np.repeat(np.sin(ang), 2, axis=-1) * np.tile(np.array([-1.0, 1.0]), HEAD_DIM // 2)
    return (jnp.asarray(np.tile(cos, (1, N_Q_HEADS)), F32), jnp.asarray(np.tile(sin, (1, N_Q_HEADS)), F32))


def _trunk_layer(x, mod3, w, layer, batch, seq_len, rope_tabs, cache_k, cache_v, h0, kv_dtype):
    q, k, v, s_tm, f_in = _inproj_call(x, mod3, w["w_in"], w["gq"], w["gk"], w["ones"], rope_tabs, layer,
                                       batch, seq_len, kv_dtype)
    attn = _attn_call(q, k, v, cache_k, cache_v, layer, batch, seq_len)
    four = _fourier_call(f_in, batch, seq_len)
    yf, yb, hfin_f, hfin_b = _s5_calls(s_tm, batch, seq_len, layer, w["s5"], h0)
    x1 = _merge_call(x, attn, four, s_tm, yf, yb, mod3, w["w_in"], w["d"], w["w_glu"], w["b_glu"], w["w_ba"],
                     w["w_bs"], w["w_bf"], w["w_out"], w["ln1_g"], w["ln1_b"], layer, batch, seq_len)
    x2 = _ffn_call(x1, mod3, w["w_up"], w["conv_w"], w["conv_b"], w["w_down"], w["ln2_g"], w["ln2_b"], layer, seq_len)
    return x2, k, v, hfin_f, hfin_b


def _split_state(h):
    b = h.shape[0]
    re = h[..., 0].reshape(b, 2, N_STATES)
    im = h[..., 1].reshape(b, 2, N_STATES)
    return jnp.concatenate([re, im], axis=-1).transpose(1, 0, 2)


def _join_state(hf, hb):
    def one(h):
        b = h.shape[0]
        re = h[:, 0:N_STATES].reshape(b, N_SSM_GROUPS, STATE_P)
        im = h[:, N_STATES:].reshape(b, N_SSM_GROUPS, STATE_P)
        return jnp.stack([re, im], axis=-1)
    return jnp.stack([one(hf), one(hb)], axis=1)


def kernel(x_prompt, x_sample, c, cache_k, cache_v, state_ssm, c_ctx, w_ada, b_ada, w_in, q_norm_g, k_norm_g, ssm_a_re, ssm_a_im, ssm_log_dt, ssm_b_re, ssm_b_im, ssm_c_re, ssm_c_im, ssm_d, w_glu, b_glu, w_br_attn, w_br_ssm, w_br_four, w_out, ln1_g, ln1_b, w_up, conv_w, conv_b, w_down, ln2_g, ln2_b):
    bc, lc, _ = x_prompt.shape
    bd, ld, _ = x_sample.shape
    past = cache_k.shape[2]

    cvec = jnp.zeros((MOD_ROWS, D_MODEL), F32).at[0:bd].set(c).at[bd].set(c_ctx)
    mod = _mod_call(cvec, w_ada, b_ada)

    s5w = _zoh_call(ssm_a_re, ssm_a_im, ssm_log_dt, ssm_b_re, ssm_b_im, ssm_c_re, ssm_c_im)

    head_mean = np.kron(np.eye(N_Q_HEADS), np.full((HEAD_DIM, HEAD_DIM), 1.0 / HEAD_DIM))
    rope_tabs = _rope_tables(ld)
    vec = lambda a: a.reshape(DEPTH, 1, -1)
    w = dict(
        w_in=w_in, gq=vec(jnp.tile(q_norm_g, (1, N_Q_HEADS))), gk=vec(jnp.tile(k_norm_g, (1, N_KV_HEADS))),
        ones=jnp.asarray(head_mean, F32).astype(BF16), s5=s5w, d=vec(ssm_d),
        w_glu=w_glu.astype(BF16), b_glu=vec(b_glu), w_ba=w_br_attn.astype(BF16), w_bs=w_br_ssm.astype(BF16),
        w_bf=w_br_four.astype(BF16), w_out=w_out.astype(BF16), ln1_g=vec(ln1_g), ln1_b=vec(ln1_b),
        w_up=w_up.astype(BF16), conv_w=conv_w, conv_b=vec(conv_b), w_down=w_down.astype(BF16), ln2_g=vec(ln2_g),
        ln2_b=vec(ln2_b))

    h = x_prompt.reshape(bc * lc, D_MODEL)
    h0_ctx = jnp.zeros((2, bc, 2 * N_STATES), F32)
    ks, vs, ss = [], [], []
    for l in range(DEPTH):
        mod_ctx = mod[l, bd:bd + 1].reshape(1, 1, 6 * D_MODEL)
        h, k_l, v_l, hf, hb = _trunk_layer(h, mod_ctx, w, l, bc, lc, None, None, None, h0_ctx, F32)
        ks.append(k_l.reshape(bc, lc, N_KV_HEADS, HEAD_DIM))
        vs.append(v_l.reshape(bc, lc, N_KV_HEADS, HEAD_DIM))
        ss.append(_join_state(hf, hb))
    new_cache_k = jnp.stack(ks, axis=1)
    new_cache_v = jnp.stack(vs, axis=1)
    new_state = jnp.stack(ss, axis=1)

    zx = x_sample.reshape(bd * ld, D_MODEL)
    ck = cache_k.reshape(bd, DEPTH, past, KV_W).astype(BF16)
    cv = cache_v.reshape(bd, DEPTH, past, KV_W).astype(BF16)
    for l in range(DEPTH):
        mod_dec = mod[l, 0:bd].reshape(bd, 1, 6 * D_MODEL)
        zx, _, _, _, _ = _trunk_layer(zx, mod_dec, w, l, bd, ld, rope_tabs, ck, cv,
                                      _split_state(state_ssm[:, l]), BF16)

    return (h.reshape(bc, lc, D_MODEL), zx.reshape(bd, ld, D_MODEL), new_cache_k, new_cache_v, new_state)
```
